```python
import jax, jax.numpy as jnp
from jax import lax
import numpy as np

D_MODEL = 1024
BATCH = 2
SEQ = 8192
DEPTH = 2

BLOCK = 128
NORM_EPS = 1e-6
SWA_WINDOW = 128
SWA_HEADS = 4
SWA_KV_HEADS = 2
SWA_HEAD_DIM = 64
CONV_WIDTH = 256
CONV_K = 3
MLA_HEADS = 4
MLA_Q_RANK = 256
MLA_KV_RANK = 128
MLA_NOPE_DIM = 64
MLA_ROPE_DIM = 32
MLA_V_DIM = 64
ROPE_THETA = 10000.0
SB_HEADS = 4
SB_HEAD_DIM = 64
GROUP_WIDTH = 256
N_GROUPS = 4
D_MIX = GROUP_WIDTH * N_GROUPS

A_Q = SWA_HEADS * SWA_HEAD_DIM
A_KV = SWA_KV_HEADS * SWA_HEAD_DIM
SB_W = SB_HEADS * SB_HEAD_DIM
IN_SIZES = (A_Q, A_KV, A_KV,
            CONV_WIDTH, CONV_WIDTH, CONV_WIDTH,
            MLA_Q_RANK, MLA_KV_RANK, MLA_ROPE_DIM,
            SB_W, SB_W, SB_W,
            D_MIX)
D_IN = int(sum(IN_SIZES))
SPLIT_IDX = [int(v) for v in np.cumsum(IN_SIZES)[:-1]]

kernel_name = "hymba_style_four_group_hybrid"


def rmsnorm(x, g):
    x32 = x.astype(jnp.float32)
    y = x32 * lax.rsqrt(jnp.mean(x32 * x32, axis=-1, keepdims=True) + NORM_EPS)
    return (y * g.astype(jnp.float32)).astype(x.dtype)


def to_blocks(t):
    b, s = t.shape[:2]
    t = t.reshape((b, s // BLOCK, BLOCK) + t.shape[2:])
    return jnp.moveaxis(t, 1, 0)


def from_blocks(t):
    t = jnp.moveaxis(t, 0, 1)
    return t.reshape((t.shape[0], t.shape[1] * t.shape[2]) + t.shape[3:])


def rope(x, pos):
    half = x.shape[-1] // 2
    freqs = ROPE_THETA ** (-jnp.arange(half, dtype=jnp.float32) / half)
    ang = pos.astype(jnp.float32)[..., None] * freqs
    ang = ang.reshape(ang.shape[:2] + (1,) * (x.ndim - 3) + (half,))
    cos, sin = jnp.cos(ang), jnp.sin(ang)
    x32 = x.astype(jnp.float32)
    x1, x2 = x32[..., :half], x32[..., half:]
    return jnp.concatenate([x1 * cos - x2 * sin, x1 * sin + x2 * cos], axis=-1).astype(x.dtype)


def swa_sink_attention(q, k, v, sinks):
    b, s, h, d = q.shape
    kvh = k.shape[2]
    g = h // kvh
    nb = s // BLOCK
    qb = q.reshape(b, nb, BLOCK, kvh, g, d).astype(jnp.float32)
    kb = k.reshape(b, nb, BLOCK, kvh, d).astype(jnp.float32)
    vb = v.reshape(b, nb, BLOCK, kvh, d).astype(jnp.float32)
    prev = lambda t: jnp.concatenate([jnp.zeros_like(t[:, :1]), t[:, :-1]], axis=1)
    kk = jnp.concatenate([prev(kb), kb], axis=2)
    vv = jnp.concatenate([prev(vb), vb], axis=2)
    scores = jnp.einsum('bnqhgd,bnkhd->bnhgqk', qb, kk) * (d ** -0.5)
    qi = jnp.arange(BLOCK)[:, None]
    kj = jnp.arange(2 * BLOCK)[None, :]
    diff = qi + BLOCK - kj
    blk = jnp.arange(nb)[:, None, None]
    valid = (diff >= 0) & (diff < SWA_WINDOW) & (blk * BLOCK + kj - BLOCK >= 0)
    scores = jnp.where(valid[None, :, None, None], scores, -jnp.inf)
    sink = jnp.broadcast_to(sinks.astype(jnp.float32).reshape(1, 1, kvh, g, 1, 1), scores.shape[:-1] + (1,))
    probs = jax.nn.softmax(jnp.concatenate([scores, sink], axis=-1), axis=-1)[..., :-1]
    out = jnp.einsum('bnhgqk,bnkhd->bnqhgd', probs, vv)
    return out.reshape(b, s, h * d).astype(q.dtype)


def short_gated_conv(bg, cg, xin, conv_w, conv_b):
    u = cg * xin
    y = lax.conv_general_dilated(u, conv_w[:, None, :], window_strides=(1,),
                                 padding=[(CONV_K - 1, 0)],
                                 dimension_numbers=('NWC', 'WIO', 'NWC'),
                                 feature_group_count=u.shape[-1])
    return bg * (y + conv_b)


def causal_softmax_attention(q, k, v):
    b, s, h, dk = q.shape
    scale = dk ** -0.5
    kf = k.astype(jnp.float32)
    vf = v.astype(jnp.float32)
    kpos = jnp.arange(s)

    def step(args):
        qblk, i = args
        sc = jnp.einsum('bqhd,bkhd->bhqk', qblk.astype(jnp.float32), kf) * scale
        qpos = i * BLOCK + jnp.arange(BLOCK)
        sc = jnp.where(kpos[None, :] <= qpos[:, None], sc, -jnp.inf)
        p = jax.nn.softmax(sc, axis=-1)
        return jnp.einsum('bhqk,bkhd->bqhd', p, vf)

    out = lax.map(step, (to_blocks(q), jnp.arange(s // BLOCK)))
    return from_blocks(out).reshape(b, s, -1).astype(q.dtype)


def mla(cq, ckv, kr, pos, g_q, w_uq, g_kv, w_ukv):
    b, s, _ = cq.shape
    q = (rmsnorm(cq, g_q) @ w_uq).reshape(b, s, MLA_HEADS, MLA_NOPE_DIM + MLA_ROPE_DIM)
    q = jnp.concatenate([q[..., :MLA_NOPE_DIM], rope(q[..., MLA_NOPE_DIM:], pos)], axis=-1)
    kv = (rmsnorm(ckv, g_kv) @ w_ukv).reshape(b, s, MLA_HEADS, MLA_NOPE_DIM + MLA_V_DIM)
    k_nope, v = kv[..., :MLA_NOPE_DIM], kv[..., MLA_NOPE_DIM:]
    k_rope = jnp.broadcast_to(rope(kr, pos)[:, :, None, :], (b, s, MLA_HEADS, MLA_ROPE_DIM))
    k = jnp.concatenate([k_nope, k_rope], axis=-1)
    return causal_softmax_attention(q, k, v)


def stick_breaking_attention(q, k, v):
    b, s, h, d = q.shape
    scale = d ** -0.5
    kf = k.astype(jnp.float32)
    vf = v.astype(jnp.float32)
    kpos = jnp.arange(s)

    def step(args):
        qblk, i = args
        z = jnp.einsum('bqhd,bkhd->bhqk', qblk.astype(jnp.float32), kf) * scale
        qpos = i * BLOCK + jnp.arange(BLOCK)
        mask = kpos[None, :] < qpos[:, None]
        log_keep = jnp.where(mask, jax.nn.log_sigmoid(-z), 0.0)
        after = lax.cumsum(log_keep, axis=3, reverse=True) - log_keep
        a = jnp.where(mask, jnp.exp(jax.nn.log_sigmoid(z) + after), 0.0)
        return jnp.einsum('bhqk,bkhd->bqhd', a, vf)

    out = lax.map(step, (to_blocks(q), jnp.arange(s // BLOCK)))
    return from_blocks(out).reshape(b, s, -1).astype(q.dtype)


def hybrid_layer(x, pos, g_pre, w_in, sinks, conv_w, conv_b, g_cq, w_uq, g_ckv, w_ukv, g_grp, w_out, g_post):
    b, s, _ = x.shape
    h = rmsnorm(x, g_pre) @ w_in
    (a_q, a_k, a_v, b_b, b_c, b_x, c_q, c_kv, c_kr, d_q, d_k, d_v, gate) = jnp.split(h, SPLIT_IDX, axis=-1)
    ya = swa_sink_attention(a_q.reshape(b, s, SWA_HEADS, SWA_HEAD_DIM),
                            a_k.reshape(b, s, SWA_KV_HEADS, SWA_HEAD_DIM),
                            a_v.reshape(b, s, SWA_KV_HEADS, SWA_HEAD_DIM), sinks)
    yb = short_gated_conv(b_b, b_c, b_x, conv_w, conv_b)
    yc = mla(c_q, c_kv, c_kr, pos, g_cq, w_uq, g_ckv, w_ukv)
    yd = stick_breaking_attention(d_q.reshape(b, s, SB_HEADS, SB_HEAD_DIM),
                                  d_k.reshape(b, s, SB_HEADS, SB_HEAD_DIM),
                                  d_v.reshape(b, s, SB_HEADS, SB_HEAD_DIM))
    y = jnp.stack([ya, yb, yc, yd], axis=2)
    y = rmsnorm(y, g_grp.reshape(N_GROUPS, GROUP_WIDTH)).reshape(b, s, D_MIX)
    y = y * jax.nn.silu(gate)
    return x + rmsnorm(y @ w_out, g_post)


def setup_inputs(seed: int = 0) -> dict:
    key = jax.random.key(seed)
    ks = jax.random.split(key, 16)
    f32 = jnp.float32
    nrm = lambda k, shape, scale: jax.random.normal(k, shape, f32) * scale
    gain = lambda k, shape: 1.0 + 0.02 * jax.random.normal(k, shape, f32)
    x = jax.random.normal(ks[0], (BATCH, SEQ, D_MODEL), f32)
    positions = jnp.broadcast_to(jnp.arange(SEQ, dtype=jnp.int32)[None, :], (BATCH, SEQ))
    return {
        "x": x,
        "positions": positions,
        "norm_pre": gain(ks[1], (DEPTH, D_MODEL)),
        "w_in": nrm(ks[2], (DEPTH, D_MODEL, D_IN), D_MODEL ** -0.5),
        "attn_sinks": nrm(ks[3], (DEPTH, SWA_HEADS), 0.5),
        "conv_w": nrm(ks[4], (DEPTH, CONV_K, CONV_WIDTH), CONV_K ** -0.5),
        "conv_b": nrm(ks[5], (DEPTH, CONV_WIDTH), 0.01),
        "mla_q_norm": gain(ks[6], (DEPTH, MLA_Q_RANK)),
        "mla_w_uq": nrm(ks[7], (DEPTH, MLA_Q_RANK, MLA_HEADS * (MLA_NOPE_DIM + MLA_ROPE_DIM)), MLA_Q_RANK ** -0.5),
        "mla_kv_norm": gain(ks[8], (DEPTH, MLA_KV_RANK)),
        "mla_w_ukv": nrm(ks[9], (DEPTH, MLA_KV_RANK, MLA_HEADS * (MLA_NOPE_DIM + MLA_V_DIM)), MLA_KV_RANK ** -0.5),
        "group_norm": gain(ks[10], (DEPTH, D_MIX)),
        "w_out": nrm(ks[11], (DEPTH, D_MIX, D_MODEL), D_MIX ** -0.5),
        "norm_post": gain(ks[12], (DEPTH, D_MODEL)),
    }


def reference(x, positions, norm_pre, w_in, attn_sinks, conv_w, conv_b, mla_q_norm, mla_w_uq,
              mla_kv_norm, mla_w_ukv, group_norm, w_out, norm_post):
    for l in range(DEPTH):
        x = hybrid_layer(x, positions, norm_pre[l], w_in[l], attn_sinks[l], conv_w[l], conv_b[l],
                         mla_q_norm[l], mla_w_uq[l], mla_kv_norm[l], mla_w_ukv[l],
                         group_norm[l], w_out[l], norm_post[l])
    return x
```

```python
import functools

import numpy as np
import jax
import jax.numpy as jnp
from jax import lax
from jax.experimental import pallas as pl
from jax.experimental.pallas import tpu as pltpu

D_MODEL = 1024
BLOCK = 128
NORM_EPS = 1e-6
SWA_HEADS = 4
SWA_HEAD_DIM = 64
CONV_WIDTH = 256
CONV_K = 3
MLA_HEADS = 4
MLA_Q_RANK = 256
MLA_KV_RANK = 128
MLA_NOPE_DIM = 64
MLA_ROPE_DIM = 32
MLA_V_DIM = 64
ROPE_THETA = 10000.0
SB_HEADS = 4
SB_HEAD_DIM = 64
GROUP_WIDTH = 256
N_GROUPS = 4
D_MIX = GROUP_WIDTH * N_GROUPS

LANES = 128
HEAD_PAD = 128
NEG_BIG = -1e30

_IN_SIZES = (256, 128, 128, 256, 256, 256, 256, 128, 32, 256, 256, 256, 1024)
_IN_OFF = np.concatenate([[0], np.cumsum(_IN_SIZES)]).astype(int)
(_O_AQ, _O_AK, _O_AV, _O_BB, _O_BC, _O_BX, _O_CQ, _O_CKV, _O_CKR, _O_DQ, _O_DK, _O_DV, _O_GATE,
 _O_END) = [int(v) for v in _IN_OFF]

P_AQ, P_AK, P_AV = 0, 256, 384
P_BB, P_BC, P_BX = 512, 768, 1024
P_CQ, P_CKV, P_KR, P_KRS = 1280, 1536, 1664, 1792
P_DQ, P_DK, P_DV = 1920, 2176, 2432
P_GATE = 2688
P_END = 3712

_SWA_PERM = np.concatenate([np.arange(0, 64), np.arange(128, 192), np.arange(64, 128),
                            np.arange(192, 256)])

PROJ_ROWS = 256
SWA_ROWS = BLOCK
MLA_TQ = 256
MLA_TK = 256
SB_TQ = 256
SB_TK = 256
OUT_ROWS = 256
VMEM_LIMIT = 56 * 1024 * 1024


def _rms(v, g):
    return v * lax.rsqrt(jnp.mean(v * v, axis=-1, keepdims=True) + NORM_EPS) * g


def _dot(a, b):
    return jnp.dot(a, b, preferred_element_type=jnp.float32)


def _dot_nt(a, b):
    return lax.dot_general(a, b, (((1,), (1,)), ((), ())), preferred_element_type=jnp.float32)


def _proj_kernel(x_ref, pos_ref, gpre_ref, win_ref, convw_ref, convb_ref, gcq_ref, wuq_ref,
                 wuqs_ref, gckv_ref, wuk_ref, wuv_ref, freq_ref,
                 aq_ref, ak_ref, av_ref, yb_ref, qc_ref, kc_ref, vc_ref, dq_ref, dk_ref, dv_ref,
                 gate_ref, ubuf_ref, *, tiles_per_seq):
    tm = x_ref.shape[0]
    bf16 = jnp.bfloat16
    xb = _rms(x_ref[...], gpre_ref[...]).astype(bf16)

    def proj(lo, hi):
        return _dot(xb, win_ref[:, lo:hi])

    aq_ref[...] = proj(P_AQ, P_AK).astype(bf16)
    ak_ref[...] = proj(P_AK, P_AV).astype(bf16)
    av_ref[...] = proj(P_AV, P_BB).astype(bf16)

    dq_ref[...] = proj(P_DQ, P_DK).astype(bf16)
    dk_ref[...] = proj(P_DK, P_DV).astype(bf16)
    dv_ref[...] = proj(P_DV, P_GATE).astype(bf16)
    gate_ref[...] = proj(P_GATE, P_END)

    u = proj(P_BC, P_BX) * proj(P_BX, P_CQ)

    @pl.when(pl.program_id(0) % tiles_per_seq == 0)
    def _():
        ubuf_ref[0:8, :] = jnp.zeros((8, CONV_WIDTH), jnp.float32)

    ubuf_ref[8:tm + 8, :] = u
    u1 = ubuf_ref[7:tm + 7, :]
    u2 = ubuf_ref[6:tm + 6, :]
    conv = (convw_ref[0:1, :] * u2 + convw_ref[1:2, :] * u1 + convw_ref[2:3, :] * u
            + convb_ref[...])
    yb_ref[...] = proj(P_BB, P_BC) * conv
    ubuf_ref[0:8, :] = ubuf_ref[tm:tm + 8, :]

    ang = pos_ref[...] * freq_ref[...]
    cosv = jnp.cos(ang)
    sinv = jnp.sin(ang)
    qscale = (MLA_NOPE_DIM + MLA_ROPE_DIM) ** -0.5
    cosq = cosv * qscale
    sinq = sinv * qscale

    cq = _rms(proj(P_CQ, P_CKV), gcq_ref[...]).astype(bf16)
    qraw = _dot(cq, wuq_ref[...])
    qswp = _dot(cq, wuqs_ref[...])
    ckv = _rms(proj(P_CKV, P_KR), gckv_ref[...]).astype(bf16)
    knope = _dot(ckv, wuk_ref[...])
    vc_ref[...] = _dot(ckv, wuv_ref[...]).astype(bf16)
    krope = proj(P_KR, P_KRS) * cosv + proj(P_KRS, P_DQ) * sinv
    for h in range(MLA_HEADS):
        sl = slice(h * HEAD_PAD, (h + 1) * HEAD_PAD)
        qc_ref[:, sl] = (qraw[:, sl] * cosq + qswp[:, sl] * sinq).astype(bf16)
        kc_ref[:, sl] = (knope[:, sl] + krope).astype(bf16)


def _proj_call(x2, pos2, lw, seq):
    n = x2.shape[0]
    tm = PROJ_ROWS
    row = lambda w: pl.BlockSpec((tm, w), lambda i: (i, 0))
    full = lambda a: pl.BlockSpec(a.shape, lambda i: (0,) * a.ndim)
    bf16 = jnp.bfloat16
    outs = [("aq", 256, bf16), ("ak", 128, bf16), ("av", 128, bf16), ("yb", 256, jnp.float32),
            ("qc", MLA_HEADS * HEAD_PAD, bf16), ("kc", MLA_HEADS * HEAD_PAD, bf16),
            ("vc", 256, bf16), ("dq", 256, bf16), ("dk", 256, bf16), ("dv", 256, bf16),
            ("gate", D_MIX, jnp.float32)]
    ins = [x2, pos2, lw["gpre"], lw["win"], lw["convw"], lw["convb"], lw["gcq"], lw["wuq"],
           lw["wuqs"], lw["gckv"], lw["wuk"], lw["wuv"], lw["freq"]]
    in_specs = [row(D_MODEL), row(1)] + [full(a) for a in ins[2:]]
    res = pl.pallas_call(
        functools.partial(_proj_kernel, tiles_per_seq=seq // tm),
        grid=(n // tm,),
        in_specs=in_specs,
        out_specs=[row(w) for _, w, _ in outs],
        out_shape=[jax.ShapeDtypeStruct((n, w), dt) for _, w, dt in outs],
        scratch_shapes=[pltpu.VMEM((tm + 8, CONV_WIDTH), jnp.float32)],
        compiler_params=pltpu.CompilerParams(dimension_semantics=("arbitrary",),
                                             vmem_limit_bytes=VMEM_LIMIT),
        name="proj",
    )(*ins)
    return {name: r for (name, _, _), r in zip(outs, res)}


def _swa_kernel(sink_ref, q_ref, kc_ref, kp_ref, vc_ref, vp_ref, o_ref, *, blocks_per_seq):
    tq = q_ref.shape[0]
    bf16 = jnp.bfloat16
    not_first = (pl.program_id(0) % blocks_per_seq) != 0
    row = lax.broadcasted_iota(jnp.int32, (tq, BLOCK), 0)
    col = lax.broadcasted_iota(jnp.int32, (tq, BLOCK), 1)
    mask_cur = col <= row
    mask_prev = (col > row) & not_first
    upper = col >= SWA_HEAD_DIM
    kc = kc_ref[...]
    kp = kp_ref[...]
    vc = vc_ref[...]
    vp = vp_ref[...]
    for lb in range(2):
        q2 = q_ref[:, lb * LANES:(lb + 1) * LANES]
        halves = []
        for hh in range(2):
            sink = sink_ref[2 * hh + lb]
            qm = jnp.where(upper == (hh == 1), q2, jnp.zeros_like(q2))
            sc = jnp.where(mask_cur, _dot_nt(qm, kc), NEG_BIG)
            sp = jnp.where(mask_prev, _dot_nt(qm, kp), NEG_BIG)
            m = jnp.maximum(jnp.maximum(jnp.max(sc, axis=1, keepdims=True),
                                        jnp.max(sp, axis=1, keepdims=True)), sink)
            pc = jnp.exp(sc - m)
            pp = jnp.exp(sp - m)
            den = (jnp.sum(pc, axis=1, keepdims=True) + jnp.sum(pp, axis=1, keepdims=True)
                   + jnp.exp(sink - m))
            o = _dot(pc.astype(bf16), vc) + _dot(pp.astype(bf16), vp)
            halves.append(o / den)
        o_ref[:, lb * LANES:(lb + 1) * LANES] = jnp.where(upper, halves[1], halves[0])


def _swa_call(sinks, aq, ak, av, seq):
    n = aq.shape[0]
    nb = seq // BLOCK
    cur = lambda w: pl.BlockSpec((BLOCK, w), lambda i: (i, 0))
    prev = lambda w: pl.BlockSpec((BLOCK, w), lambda i: (jnp.maximum(i - 1, 0), 0))
    return pl.pallas_call(
        functools.partial(_swa_kernel, blocks_per_seq=nb),
        grid=(n // BLOCK,),
        in_specs=[pl.BlockSpec(memory_space=pltpu.SMEM), cur(256), cur(128), prev(128), cur(128),
                  prev(128)],
        out_specs=cur(256),
        out_shape=jax.ShapeDtypeStruct((n, 256), jnp.float32),
        compiler_params=pltpu.CompilerParams(dimension_semantics=("arbitrary",)),
        name="swa",
    )(sinks, aq, ak, ak, av, av)


def _mla_kernel(q_ref, k_ref, v_ref, o_ref):
    tq, tk = MLA_TQ, MLA_TK
    bf16 = jnp.bfloat16
    i = pl.program_id(2)
    qs = [q_ref[:, h * HEAD_PAD:(h + 1) * HEAD_PAD] for h in range(2)]
    row = lax.broadcasted_iota(jnp.int32, (tq, tk), 0)
    col = lax.broadcasted_iota(jnp.int32, (tq, tk), 1)
    causal = col <= row

    def step(c, carry, masked):
        off = pl.multiple_of(c * tk, tk)
        k2 = k_ref[pl.ds(off, tk), :]
        v2 = v_ref[pl.ds(off, tk), :]
        new = []
        for h in range(2):
            m, l, acc = carry[h]
            s = _dot_nt(qs[h], k2[:, h * HEAD_PAD:(h + 1) * HEAD_PAD])
            if masked:
                s = jnp.where(causal, s, NEG_BIG)
            m_new = jnp.maximum(m, jnp.max(s, axis=1, keepdims=True))
            alpha = jnp.exp(m - m_new)
            p = jnp.exp(s - m_new)
            l = alpha * l + jnp.sum(p, axis=1, keepdims=True)
            acc = alpha * acc + _dot(p.astype(bf16), v2)
            new.append((m_new, l, acc))
        return tuple(new)

    init = tuple((jnp.full((tq, 1), NEG_BIG, jnp.float32), jnp.zeros((tq, 1), jnp.float32),
                  jnp.zeros((tq, LANES), jnp.float32)) for _ in range(2))
    carry = lax.fori_loop(0, i, lambda c, cr: step(c, cr, False), init)
    (_, l0, a0), (_, l1, a1) = step(i, carry, True)
    upper = lax.broadcasted_iota(jnp.int32, (tq, LANES), 1) >= MLA_V_DIM
    o_ref[...] = jnp.where(upper, a1 / l1, a0 / l0)


def _mla_call(qc, kc, vc, batch, seq):
    tq = MLA_TQ
    nq = seq // tq
    q3 = qc.reshape(batch, seq, MLA_HEADS * HEAD_PAD)
    k3 = kc.reshape(batch, seq, MLA_HEADS * HEAD_PAD)
    v3 = vc.reshape(batch, seq, MLA_HEADS * MLA_V_DIM)
    out = pl.pallas_call(
        _mla_kernel,
        grid=(batch, 2, nq),
        in_specs=[pl.BlockSpec((None, tq, 2 * HEAD_PAD), lambda b, p, i: (b, i, p)),
                  pl.BlockSpec((None, seq, 2 * HEAD_PAD), lambda b, p, i: (b, 0, p)),
                  pl.BlockSpec((None, seq, LANES), lambda b, p, i: (b, 0, p))],
        out_specs=pl.BlockSpec((None, tq, LANES), lambda b, p, i: (b, i, p)),
        out_shape=jax.ShapeDtypeStruct((batch, seq, 256), jnp.float32),
        compiler_params=pltpu.CompilerParams(
            dimension_semantics=("arbitrary", "arbitrary", "arbitrary"),
            vmem_limit_bytes=VMEM_LIMIT),
        name="mla",
    )(q3, k3, v3)
    return out.reshape(batch * seq, 256)


def _sb_kernel(q_ref, k_ref, v_ref, tri_ref, o_ref):
    tq, tk = SB_TQ, SB_TK
    bf16 = jnp.bfloat16
    i = pl.program_id(2)
    q2 = q_ref[...]
    upper_q = lax.broadcasted_iota(jnp.int32, (tq, LANES), 1) >= SB_HEAD_DIM
    qs = [jnp.where(upper_q == (h == 1), q2, jnp.zeros_like(q2)) for h in range(2)]
    row = lax.broadcasted_iota(jnp.int32, (tq, tk), 0)
    col = lax.broadcasted_iota(jnp.int32, (tq, tk), 1)
    strict = col < row
    tri = tri_ref[...]

    def step(c, carry, masked):
        off = pl.multiple_of(c * tk, tk)
        k2 = k_ref[pl.ds(off, tk), :]
        v2 = v_ref[pl.ds(off, tk), :]
        new = []
        for h in range(2):
            csum, acc = carry[h]
            z = _dot_nt(qs[h], k2)
            softplus_tail = jnp.log(1.0 + jnp.exp(-jnp.abs(z)))
            log_beta = jnp.minimum(z, 0.0) - softplus_tail
            log_keep = log_beta - z
            if masked:
                log_keep = jnp.where(strict, log_keep, 0.0)
            hi = log_keep.astype(bf16)
            lo = (log_keep - hi.astype(jnp.float32)).astype(bf16)
            within = _dot(hi, tri) + _dot(lo, tri)
            a = jnp.exp(log_beta + (csum + within))
            if masked:
                a = jnp.where(strict, a, 0.0)
            acc = acc + _dot(a.astype(bf16), v2)
            csum = csum + jnp.sum(log_keep, axis=1, keepdims=True)
            new.append((csum, acc))
        return tuple(new)

    init = tuple((jnp.zeros((tq, 1), jnp.float32), jnp.zeros((tq, LANES), jnp.float32))
                 for _ in range(2))
    carry = step(i, init, True)
    carry = lax.fori_loop(0, i, lambda t, cr: step(i - 1 - t, cr, False), carry)
    (_, a0), (_, a1) = carry
    o_ref[...] = jnp.where(upper_q, a1, a0)


def _sb_call(dq, dk, dv, tri, batch, seq):
    tq = SB_TQ
    nq = seq // tq
    q3 = dq.reshape(batch, seq, 256)
    k3 = dk.reshape(batch, seq, 256)
    v3 = dv.reshape(batch, seq, 256)
    out = pl.pallas_call(
        _sb_kernel,
        grid=(batch, 2, nq),
        in_specs=[pl.BlockSpec((None, tq, LANES), lambda b, p, i: (b, i, p)),
                  pl.BlockSpec((None, seq, LANES), lambda b, p, i: (b, 0, p)),
                  pl.BlockSpec((None, seq, LANES), lambda b, p, i: (b, 0, p)),
                  pl.BlockSpec((SB_TK, SB_TK), lambda b, p, i: (0, 0))],
        out_specs=pl.BlockSpec((None, tq, LANES), lambda b, p, i: (b, i, p)),
        out_shape=jax.ShapeDtypeStruct((batch, seq, 256), jnp.float32),
        compiler_params=pltpu.CompilerParams(
            dimension_semantics=("arbitrary", "arbitrary", "arbitrary"),
            vmem_limit_bytes=VMEM_LIMIT),
        name="sb",
    )(q3, k3, v3, tri)
    return out.reshape(batch * seq, 256)


def _out_kernel(x_ref, ya_ref, yb_ref, yc_ref, yd_ref, gate_ref, ggrp_ref, wout_ref, gpost_ref,
                o_ref):
    bf16 = jnp.bfloat16
    d = None
    for g, y_ref in enumerate((ya_ref, yb_ref, yc_ref, yd_ref)):
        sl = slice(g * GROUP_WIDTH, (g + 1) * GROUP_WIDTH)
        gate = gate_ref[:, sl]
        y = _rms(y_ref[...], ggrp_ref[:, sl]) * (gate * jax.nn.sigmoid(gate))
        part = _dot(y.astype(bf16), wout_ref[sl, :])
        d = part if d is None else d + part
    o_ref[...] = x_ref[...] + _rms(d, gpost_ref[...])


def _out_call(x2, ya, yb, yc, yd, gate, lw):
    n = x2.shape[0]
    tm = OUT_ROWS
    row = lambda w: pl.BlockSpec((tm, w), lambda i: (i, 0))
    full = lambda a: pl.BlockSpec(a.shape, lambda i: (0,) * a.ndim)
    return pl.pallas_call(
        _out_kernel,
        grid=(n // tm,),
        in_specs=[row(D_MODEL), row(256), row(256), row(256), row(256), row(D_MIX),
                  full(lw["ggrp"]), full(lw["wout"]), full(lw["gpost"])],
        out_specs=row(D_MODEL),
        out_shape=jax.ShapeDtypeStruct((n, D_MODEL), jnp.float32),
        compiler_params=pltpu.CompilerParams(dimension_semantics=("arbitrary",),
                                             vmem_limit_bytes=VMEM_LIMIT),
        name="out",
    )(x2, ya, yb, yc, yd, gate, lw["ggrp"], lw["wout"], lw["gpost"])


def _rope_swap(w):
    half = MLA_ROPE_DIM // 2
    return jnp.concatenate([-w[..., half:], w[..., :half]], axis=-1)


def _prep_layer(l, norm_pre, w_in, attn_sinks, conv_w, conv_b, mla_q_norm, mla_w_uq, mla_kv_norm,
                mla_w_ukv, group_norm, w_out, norm_post):
    f32, bf16 = jnp.float32, jnp.bfloat16
    w = w_in[l]
    zeros = lambda c: jnp.zeros((D_MODEL, c), f32)
    wkr = w[:, _O_CKR:_O_DQ]
    kr_pad = lambda m: jnp.concatenate([zeros(MLA_NOPE_DIM), m, zeros(HEAD_PAD - 96)], axis=1)
    win = jnp.concatenate([
        w[:, _O_AQ:_O_AK][:, _SWA_PERM] * (SWA_HEAD_DIM ** -0.5),
        w[:, _O_AK:_O_BB],
        w[:, _O_BB:_O_CQ],
        w[:, _O_CQ:_O_CKR],
        kr_pad(wkr), kr_pad(_rope_swap(wkr)),
        w[:, _O_DQ:_O_DK] * (SB_HEAD_DIM ** -0.5),
        w[:, _O_DK:_O_GATE],
        w[:, _O_GATE:_O_GATE + GROUP_WIDTH][:, _SWA_PERM],
        w[:, _O_GATE + GROUP_WIDTH:_O_END],
    ], axis=1).astype(bf16)

    dqk = MLA_NOPE_DIM + MLA_ROPE_DIM
    uq = mla_w_uq[l].reshape(MLA_Q_RANK, MLA_HEADS, dqk)
    zq = lambda c: jnp.zeros((MLA_Q_RANK, MLA_HEADS, c), f32)
    wuq = jnp.concatenate([uq, zq(HEAD_PAD - dqk)], axis=2).reshape(MLA_Q_RANK, -1).astype(bf16)
    wuqs = jnp.concatenate([zq(MLA_NOPE_DIM), _rope_swap(uq[:, :, MLA_NOPE_DIM:]),
                            zq(HEAD_PAD - dqk)], axis=2).reshape(MLA_Q_RANK, -1).astype(bf16)
    ukv = mla_w_ukv[l].reshape(MLA_KV_RANK, MLA_HEADS, MLA_NOPE_DIM + MLA_V_DIM)
    wuk = jnp.concatenate([ukv[:, :, :MLA_NOPE_DIM],
                           jnp.zeros((MLA_KV_RANK, MLA_HEADS, HEAD_PAD - MLA_NOPE_DIM), f32)],
                          axis=2).reshape(MLA_KV_RANK, -1).astype(bf16)
    wuv = ukv[:, :, MLA_NOPE_DIM:].reshape(MLA_KV_RANK, -1).astype(bf16)

    half = MLA_ROPE_DIM // 2
    freqs = (ROPE_THETA ** (-np.arange(half, dtype=np.float32) / half)).astype(np.float32)
    freq = np.zeros((1, HEAD_PAD), np.float32)
    freq[0, MLA_NOPE_DIM:MLA_NOPE_DIM + half] = freqs
    freq[0, MLA_NOPE_DIM + half:dqk] = freqs

    ggrp = group_norm[l]
    ggrp = jnp.concatenate([ggrp[:GROUP_WIDTH][_SWA_PERM], ggrp[GROUP_WIDTH:]]).reshape(1, D_MIX)
    wo = w_out[l]
    wout = jnp.concatenate([wo[:GROUP_WIDTH][_SWA_PERM], wo[GROUP_WIDTH:]], axis=0).astype(bf16)
    return dict(
        gpre=norm_pre[l].reshape(1, D_MODEL), win=win,
        sinks=attn_sinks[l], convw=conv_w[l], convb=conv_b[l].reshape(1, CONV_WIDTH),
        gcq=mla_q_norm[l].reshape(1, MLA_Q_RANK), wuq=wuq, wuqs=wuqs,
        gckv=mla_kv_norm[l].reshape(1, MLA_KV_RANK), wuk=wuk, wuv=wuv, freq=jnp.asarray(freq),
        ggrp=ggrp, wout=wout, gpost=norm_post[l].reshape(1, D_MODEL))


def kernel(x, positions, norm_pre, w_in, attn_sinks, conv_w, conv_b, mla_q_norm, mla_w_uq,
           mla_kv_norm, mla_w_ukv, group_norm, w_out, norm_post):
    batch, seq, _ = x.shape
    depth = w_in.shape[0]
    assert seq % max(PROJ_ROWS, MLA_TQ, SB_TQ, OUT_ROWS) == 0
    n = batch * seq
    x2 = x.reshape(n, D_MODEL)
    pos2 = positions.astype(jnp.float32).reshape(n, 1)
    tri = jnp.asarray(np.tril(np.ones((SB_TK, SB_TK), np.float32), -1), jnp.bfloat16)
    for l in range(depth):
        lw = _prep_layer(l, norm_pre, w_in, attn_sinks, conv_w, conv_b, mla_q_norm, mla_w_uq,
                         mla_kv_norm, mla_w_ukv, group_norm, w_out, norm_post)
        p = _proj_call(x2, pos2, lw, seq)
        ya = _swa_call(lw["sinks"], p["aq"], p["ak"], p["av"], seq)
        yc = _mla_call(p["qc"], p["kc"], p["vc"], batch, seq)
        yd = _sb_call(p["dq"], p["dk"], p["dv"], tri, batch, seq)
        x2 = _out_call(x2, ya, p["yb"], yc, yd, p["gate"], lw)
    return x2.reshape(batch, seq, D_MODEL)
```

```python
import functools

import numpy as np
import jax
import jax.numpy as jnp
from jax import lax
from jax.experimental import pallas as pl
from jax.experimental.pallas import tpu as pltpu

D_MODEL = 1024
BLOCK = 128
NORM_EPS = 1e-6
SWA_HEADS = 4
SWA_HEAD_DIM = 64
CONV_WIDTH = 256
CONV_K = 3
MLA_HEADS = 4
MLA_Q_RANK = 256
MLA_KV_RANK = 128
MLA_NOPE_DIM = 64
MLA_ROPE_DIM = 32
MLA_V_DIM = 64
ROPE_THETA = 10000.0
SB_HEADS = 4
SB_HEAD_DIM = 64
GROUP_WIDTH = 256
N_GROUPS = 4
D_MIX = GROUP_WIDTH * N_GROUPS

LANES = 128
HEAD_PAD = 128
NEG_BIG = -1e30
SB_DEAD = -120.0

_IN_SIZES = (256, 128, 128, 256, 256, 256, 256, 128, 32, 256, 256, 256, 1024)
_IN_OFF = np.concatenate([[0], np.cumsum(_IN_SIZES)]).astype(int)
(_O_AQ, _O_AK, _O_AV, _O_BB, _O_BC, _O_BX, _O_CQ, _O_CKV, _O_CKR, _O_DQ, _O_DK, _O_DV, _O_GATE,
 _O_END) = [int(v) for v in _IN_OFF]

P_AQ, P_AK, P_AV = 0, 256, 384
P_BB, P_BC, P_BX = 512, 768, 1024
P_CQ, P_CKV, P_KR, P_KRS = 1280, 1536, 1664, 1792
P_DQ, P_DK, P_DV = 1920, 2176, 2432
P_GATE = 2688
P_END = 3712

_SWA_PERM = np.concatenate([np.arange(0, 64), np.arange(128, 192), np.arange(64, 128),
                            np.arange(192, 256)])

PROJ_ROWS = 256
SWA_ROWS = BLOCK
MLA_TQ = 512
MLA_TK = 512
SB_TQ = 256
SB_TK = 256
OUT_ROWS = 256
VMEM_LIMIT = 56 * 1024 * 1024


def _rms(v, g):
    return v * lax.rsqrt(jnp.mean(v * v, axis=-1, keepdims=True) + NORM_EPS) * g


def _dot(a, b):
    return jnp.dot(a, b, preferred_element_type=jnp.float32)


def _dot_nt(a, b):
    return lax.dot_general(a, b, (((1,), (1,)), ((), ())), preferred_element_type=jnp.float32)


def _proj_kernel(x_ref, pos_ref, gpre_ref, win_ref, convw_ref, convb_ref, gcq_ref, wuq_ref,
                 wuqs_ref, gckv_ref, wuk_ref, wuv_ref, freq_ref,
                 aq_ref, ak_ref, av_ref, yb_ref, qc_ref, kc_ref, vc_ref, dq_ref, dk_ref, dv_ref,
                 gate_ref, ubuf_ref, *, tiles_per_seq):
    tm = x_ref.shape[0]
    bf16 = jnp.bfloat16
    xb = _rms(x_ref[...], gpre_ref[...]).astype(bf16)

    def proj(lo, hi):
        return _dot(xb, win_ref[:, lo:hi])

    aq_ref[...] = proj(P_AQ, P_AK).astype(bf16)
    ak_ref[...] = proj(P_AK, P_AV).astype(bf16)
    av_ref[...] = proj(P_AV, P_BB).astype(bf16)

    dq_ref[...] = proj(P_DQ, P_DK).astype(bf16)
    dk_ref[...] = proj(P_DK, P_DV).astype(bf16)
    dv_ref[...] = proj(P_DV, P_GATE).astype(bf16)
    gate_ref[...] = proj(P_GATE, P_END)

    u = proj(P_BC, P_BX) * proj(P_BX, P_CQ)

    @pl.when(pl.program_id(0) % tiles_per_seq == 0)
    def _():
        ubuf_ref[0:8, :] = jnp.zeros((8, CONV_WIDTH), jnp.float32)

    ubuf_ref[8:tm + 8, :] = u
    u1 = ubuf_ref[7:tm + 7, :]
    u2 = ubuf_ref[6:tm + 6, :]
    conv = (convw_ref[0:1, :] * u2 + convw_ref[1:2, :] * u1 + convw_ref[2:3, :] * u
            + convb_ref[...])
    yb_ref[...] = proj(P_BB, P_BC) * conv
    ubuf_ref[0:8, :] = ubuf_ref[tm:tm + 8, :]

    ang = pos_ref[...] * freq_ref[...]
    cosv = jnp.cos(ang)
    sinv = jnp.sin(ang)
    qscale = (MLA_NOPE_DIM + MLA_ROPE_DIM) ** -0.5 * np.log2(np.e)
    cosq = cosv * qscale
    sinq = sinv * qscale

    cq = _rms(proj(P_CQ, P_CKV), gcq_ref[...]).astype(bf16)
    qraw = _dot(cq, wuq_ref[...])
    qswp = _dot(cq, wuqs_ref[...])
    ckv = _rms(proj(P_CKV, P_KR), gckv_ref[...]).astype(bf16)
    knope = _dot(ckv, wuk_ref[...])
    vc_ref[...] = _dot(ckv, wuv_ref[...]).astype(bf16)
    krope = proj(P_KR, P_KRS) * cosv + proj(P_KRS, P_DQ) * sinv
    for h in range(MLA_HEADS):
        sl = slice(h * HEAD_PAD, (h + 1) * HEAD_PAD)
        qc_ref[:, sl] = (qraw[:, sl] * cosq + qswp[:, sl] * sinq).astype(bf16)
        kc_ref[:, sl] = (knope[:, sl] + krope).astype(bf16)


def _proj_call(x2, pos2, lw, seq):
    n = x2.shape[0]
    tm = PROJ_ROWS
    row = lambda w: pl.BlockSpec((tm, w), lambda i: (i, 0))
    full = lambda a: pl.BlockSpec(a.shape, lambda i: (0,) * a.ndim)
    bf16 = jnp.bfloat16
    outs = [("aq", 256, bf16), ("ak", 128, bf16), ("av", 128, bf16), ("yb", 256, jnp.float32),
            ("qc", MLA_HEADS * HEAD_PAD, bf16), ("kc", MLA_HEADS * HEAD_PAD, bf16),
            ("vc", 256, bf16), ("dq", 256, bf16), ("dk", 256, bf16), ("dv", 256, bf16),
            ("gate", D_MIX, jnp.float32)]
    ins = [x2, pos2, lw["gpre"], lw["win"], lw["convw"], lw["convb"], lw["gcq"], lw["wuq"],
           lw["wuqs"], lw["gckv"], lw["wuk"], lw["wuv"], lw["freq"]]
    in_specs = [row(D_MODEL), row(1)] + [full(a) for a in ins[2:]]
    res = pl.pallas_call(
        functools.partial(_proj_kernel, tiles_per_seq=seq // tm),
        grid=(n // tm,),
        in_specs=in_specs,
        out_specs=[row(w) for _, w, _ in outs],
        out_shape=[jax.ShapeDtypeStruct((n, w), dt) for _, w, dt in outs],
        scratch_shapes=[pltpu.VMEM((tm + 8, CONV_WIDTH), jnp.float32)],
        compiler_params=pltpu.CompilerParams(dimension_semantics=("arbitrary",),
                                             vmem_limit_bytes=VMEM_LIMIT),
        name="proj",
    )(*ins)
    return {name: r for (name, _, _), r in zip(outs, res)}


def _swa_kernel(sink_ref, q_ref, kc_ref, kp_ref, vc_ref, vp_ref, o_ref, *, blocks_per_seq):
    tq = q_ref.shape[0]
    bf16 = jnp.bfloat16
    not_first = (pl.program_id(0) % blocks_per_seq) != 0
    row = lax.broadcasted_iota(jnp.int32, (tq, BLOCK), 0)
    col = lax.broadcasted_iota(jnp.int32, (tq, BLOCK), 1)
    mask_cur = col <= row
    mask_prev = (col > row) & not_first
    upper = col >= SWA_HEAD_DIM
    kc = kc_ref[...]
    kp = kp_ref[...]
    vc = vc_ref[...]
    vp = vp_ref[...]
    for lb in range(2):
        q2 = q_ref[:, lb * LANES:(lb + 1) * LANES]
        halves = []
        for hh in range(2):
            sink = sink_ref[2 * hh + lb]
            qm = jnp.where(upper == (hh == 1), q2, jnp.zeros_like(q2))
            sc = jnp.where(mask_cur, _dot_nt(qm, kc), NEG_BIG)
            sp = jnp.where(mask_prev, _dot_nt(qm, kp), NEG_BIG)
            m = jnp.maximum(jnp.maximum(jnp.max(sc, axis=1, keepdims=True),
                                        jnp.max(sp, axis=1, keepdims=True)), sink)
            pc = jnp.exp(sc - m)
            pp = jnp.exp(sp - m)
            den = (jnp.sum(pc, axis=1, keepdims=True) + jnp.sum(pp, axis=1, keepdims=True)
                   + jnp.exp(sink - m))
            o = _dot(pc.astype(bf16), vc) + _dot(pp.astype(bf16), vp)
            halves.append(o / den)
        o_ref[:, lb * LANES:(lb + 1) * LANES] = jnp.where(upper, halves[1], halves[0])


def _swa_call(sinks, aq, ak, av, seq):
    n = aq.shape[0]
    nb = seq // BLOCK
    cur = lambda w: pl.BlockSpec((BLOCK, w), lambda i: (i, 0))
    prev = lambda w: pl.BlockSpec((BLOCK, w), lambda i: (jnp.maximum(i - 1, 0), 0))
    return pl.pallas_call(
        functools.partial(_swa_kernel, blocks_per_seq=nb),
        grid=(n // BLOCK,),
        in_specs=[pl.BlockSpec(memory_space=pltpu.SMEM), cur(256), cur(128), prev(128), cur(128),
                  prev(128)],
        out_specs=cur(256),
        out_shape=jax.ShapeDtypeStruct((n, 256), jnp.float32),
        compiler_params=pltpu.CompilerParams(dimension_semantics=("arbitrary",)),
        name="swa",
    )(sinks, aq, ak, ak, av, av)


def _mla_kernel(q_ref, k_ref, v_ref, o_ref):
    tq, tk = MLA_TQ, MLA_TK
    bf16 = jnp.bfloat16
    i = pl.program_id(1)
    row = lax.broadcasted_iota(jnp.int32, (tq, tk), 0)
    col = lax.broadcasted_iota(jnp.int32, (tq, tk), 1)
    causal = col <= row

    def step(c, carry, masked):
        off = pl.multiple_of(c * tk, tk)
        new = []
        for h in range(MLA_HEADS):
            m, l, acc = carry[h]
            hs = slice(h * HEAD_PAD, (h + 1) * HEAD_PAD)
            vs = slice((h // 2) * LANES, (h // 2 + 1) * LANES)
            s = _dot_nt(q_ref[:, hs], k_ref[pl.ds(off, tk), hs])
            if masked:
                s = jnp.where(causal, s, NEG_BIG)
            m_new = jnp.maximum(m, jnp.max(s, axis=1, keepdims=True))
            alpha = jnp.exp2(m - m_new)
            p = jnp.exp2(s - m_new)
            l = alpha * l + jnp.sum(p, axis=1, keepdims=True)
            acc = alpha * acc + _dot(p.astype(bf16), v_ref[pl.ds(off, tk), vs])
            new.append((m_new, l, acc))
        return tuple(new)

    init = tuple((jnp.full((tq, 1), NEG_BIG, jnp.float32), jnp.zeros((tq, 1), jnp.float32),
                  jnp.zeros((tq, LANES), jnp.float32)) for _ in range(MLA_HEADS))
    carry = lax.fori_loop(0, i, lambda c, cr: step(c, cr, False), init)
    res = [acc / l for _, l, acc in step(i, carry, True)]
    upper = lax.broadcasted_iota(jnp.int32, (tq, LANES), 1) >= MLA_V_DIM
    for lb in range(2):
        o_ref[:, lb * LANES:(lb + 1) * LANES] = jnp.where(upper, res[2 * lb + 1], res[2 * lb])


def _mla_call(qc, kc, vc, batch, seq):
    tq = MLA_TQ
    nq = seq // tq
    q3 = qc.reshape(batch, seq, MLA_HEADS * HEAD_PAD)
    k3 = kc.reshape(batch, seq, MLA_HEADS * HEAD_PAD)
    v3 = vc.reshape(batch, seq, MLA_HEADS * MLA_V_DIM)
    out = pl.pallas_call(
        _mla_kernel,
        grid=(batch, nq),
        in_specs=[pl.BlockSpec((None, tq, MLA_HEADS * HEAD_PAD), lambda b, i: (b, i, 0)),
                  pl.BlockSpec((None, seq, MLA_HEADS * HEAD_PAD), lambda b, i: (b, 0, 0)),
                  pl.BlockSpec((None, seq, 256), lambda b, i: (b, 0, 0))],
        out_specs=pl.BlockSpec((None, tq, 256), lambda b, i: (b, i, 0)),
        out_shape=jax.ShapeDtypeStruct((batch, seq, 256), jnp.float32),
        compiler_params=pltpu.CompilerParams(
            dimension_semantics=("arbitrary", "arbitrary"),
            vmem_limit_bytes=VMEM_LIMIT),
        name="mla",
    )(q3, k3, v3)
    return out.reshape(batch * seq, 256)


def _sb_kernel(q_ref, k_ref, v_ref, tri_ref, o_ref):
    tq, tk = SB_TQ, SB_TK
    bf16 = jnp.bfloat16
    i = pl.program_id(1)
    upper_q = lax.broadcasted_iota(jnp.int32, (tq, LANES), 1) >= SB_HEAD_DIM
    heads = [(lb, hh) for lb in range(2) for hh in range(2)]
    qs = []
    for lb, hh in heads:
        q2 = q_ref[:, lb * LANES:(lb + 1) * LANES]
        qs.append(jnp.where(upper_q == (hh == 1), q2, jnp.zeros_like(q2)))
    row = lax.broadcasted_iota(jnp.int32, (tq, tk), 0)
    col = lax.broadcasted_iota(jnp.int32, (tq, tk), 1)
    strict = col < row
    tri = tri_ref[...]

    def step(c, state, masked):
        off = pl.multiple_of(c * tk, tk)
        new = []
        for n, (lb, hh) in enumerate(heads):
            csum, acc = state[n]
            k2 = k_ref[pl.ds(off, tk), lb * LANES:(lb + 1) * LANES]
            v2 = v_ref[pl.ds(off, tk), lb * LANES:(lb + 1) * LANES]
            z = _dot_nt(qs[n], k2)
            softplus_tail = jnp.log(1.0 + jnp.exp(-jnp.abs(z)))
            log_beta = jnp.minimum(z, 0.0) - softplus_tail
            log_keep = log_beta - z
            if masked:
                log_keep = jnp.where(strict, log_keep, 0.0)
            hi = log_keep.astype(bf16)
            lo = (log_keep - hi.astype(jnp.float32)).astype(bf16)
            within = _dot(hi, tri) + _dot(lo, tri)
            a = jnp.exp(log_beta + (csum + within))
            if masked:
                a = jnp.where(strict, a, 0.0)
            acc = acc + _dot(a.astype(bf16), v2)
            csum = csum + jnp.sum(log_keep, axis=1, keepdims=True)
            new.append((csum, acc))
        return tuple(new)

    def least_decayed(state):
        worst = state[0][0]
        for csum, _ in state[1:]:
            worst = jnp.maximum(worst, csum)
        return jnp.max(worst)

    init = tuple((jnp.zeros((tq, 1), jnp.float32), jnp.zeros((tq, LANES), jnp.float32))
                 for _ in heads)
    state = step(i, init, True)

    def cond(carry):
        t, worst, _ = carry
        return jnp.logical_and(t < i, worst > SB_DEAD)

    def body(carry):
        t, _, st = carry
        st = step(i - 1 - t, st, False)
        return t + 1, least_decayed(st), st

    _, _, state = lax.while_loop(cond, body, (jnp.int32(0), least_decayed(state), state))
    for lb in range(2):
        o_ref[:, lb * LANES:(lb + 1) * LANES] = jnp.where(upper_q, state[2 * lb + 1][1],
                                                          state[2 * lb][1])


def _sb_call(dq, dk, dv, tri, batch, seq):
    tq = SB_TQ
    nq = seq // tq
    q3 = dq.reshape(batch, seq, 256)
    k3 = dk.reshape(batch, seq, 256)
    v3 = dv.reshape(batch, seq, 256)
    out = pl.pallas_call(
        _sb_kernel,
        grid=(batch, nq),
        in_specs=[pl.BlockSpec((None, tq, 256), lambda b, i: (b, i, 0)),
                  pl.BlockSpec((None, seq, 256), lambda b, i: (b, 0, 0)),
                  pl.BlockSpec((None, seq, 256), lambda b, i: (b, 0, 0)),
                  pl.BlockSpec((SB_TK, SB_TK), lambda b, i: (0, 0))],
        out_specs=pl.BlockSpec((None, tq, 256), lambda b, i: (b, i, 0)),
        out_shape=jax.ShapeDtypeStruct((batch, seq, 256), jnp.float32),
        compiler_params=pltpu.CompilerParams(
            dimension_semantics=("arbitrary", "arbitrary"),
            vmem_limit_bytes=VMEM_LIMIT),
        name="sb",
    )(q3, k3, v3, tri)
    return out.reshape(batch * seq, 256)


def _out_kernel(x_ref, ya_ref, yb_ref, yc_ref, yd_ref, gate_ref, ggrp_ref, wout_ref, gpost_ref,
                o_ref):
    bf16 = jnp.bfloat16
    d = None
    for g, y_ref in enumerate((ya_ref, yb_ref, yc_ref, yd_ref)):
        sl = slice(g * GROUP_WIDTH, (g + 1) * GROUP_WIDTH)
        gate = gate_ref[:, sl]
        y = _rms(y_ref[...], ggrp_ref[:, sl]) * (gate * jax.nn.sigmoid(gate))
        part = _dot(y.astype(bf16), wout_ref[sl, :])
        d = part if d is None else d + part
    o_ref[...] = x_ref[...] + _rms(d, gpost_ref[...])


def _out_call(x2, ya, yb, yc, yd, gate, lw):
    n = x2.shape[0]
    tm = OUT_ROWS
    row = lambda w: pl.BlockSpec((tm, w), lambda i: (i, 0))
    full = lambda a: pl.BlockSpec(a.shape, lambda i: (0,) * a.ndim)
    return pl.pallas_call(
        _out_kernel,
        grid=(n // tm,),
        in_specs=[row(D_MODEL), row(256), row(256), row(256), row(256), row(D_MIX),
                  full(lw["ggrp"]), full(lw["wout"]), full(lw["gpost"])],
        out_specs=row(D_MODEL),
        out_shape=jax.ShapeDtypeStruct((n, D_MODEL), jnp.float32),
        compiler_params=pltpu.CompilerParams(dimension_semantics=("arbitrary",),
                                             vmem_limit_bytes=VMEM_LIMIT),
        name="out",
    )(x2, ya, yb, yc, yd, gate, lw["ggrp"], lw["wout"], lw["gpost"])


def _rope_swap(w):
    half = MLA_ROPE_DIM // 2
    return jnp.concatenate([-w[..., half:], w[..., :half]], axis=-1)


def _prep_layer(l, norm_pre, w_in, attn_sinks, conv_w, conv_b, mla_q_norm, mla_w_uq, mla_kv_norm,
                mla_w_ukv, group_norm, w_out, norm_post):
    f32, bf16 = jnp.float32, jnp.bfloat16
    w = w_in[l]
    zeros = lambda c: jnp.zeros((D_MODEL, c), f32)
    wkr = w[:, _O_CKR:_O_DQ]
    kr_pad = lambda m: jnp.concatenate([zeros(MLA_NOPE_DIM), m, zeros(HEAD_PAD - 96)], axis=1)
    win = jnp.concatenate([
        w[:, _O_AQ:_O_AK][:, _SWA_PERM] * (SWA_HEAD_DIM ** -0.5),
        w[:, _O_AK:_O_BB],
        w[:, _O_BB:_O_CQ],
        w[:, _O_CQ:_O_CKR],
        kr_pad(wkr), kr_pad(_rope_swap(wkr)),
        w[:, _O_DQ:_O_DK] * (SB_HEAD_DIM ** -0.5),
        w[:, _O_DK:_O_GATE],
        w[:, _O_GATE:_O_GATE + GROUP_WIDTH][:, _SWA_PERM],
        w[:, _O_GATE + GROUP_WIDTH:_O_END],
    ], axis=1).astype(bf16)

    dqk = MLA_NOPE_DIM + MLA_ROPE_DIM
    uq = mla_w_uq[l].reshape(MLA_Q_RANK, MLA_HEADS, dqk)
    zq = lambda c: jnp.zeros((MLA_Q_RANK, MLA_HEADS, c), f32)
    wuq = jnp.concatenate([uq, zq(HEAD_PAD - dqk)], axis=2).reshape(MLA_Q_RANK, -1).astype(bf16)
    wuqs = jnp.concatenate([zq(MLA_NOPE_DIM), _rope_swap(uq[:, :, MLA_NOPE_DIM:]),
                            zq(HEAD_PAD - dqk)], axis=2).reshape(MLA_Q_RANK, -1).astype(bf16)
    ukv = mla_w_ukv[l].reshape(MLA_KV_RANK, MLA_HEADS, MLA_NOPE_DIM + MLA_V_DIM)
    wuk = jnp.concatenate([ukv[:, :, :MLA_NOPE_DIM],
                           jnp.zeros((MLA_KV_RANK, MLA_HEADS, HEAD_PAD - MLA_NOPE_DIM), f32)],
                          axis=2).reshape(MLA_KV_RANK, -1).astype(bf16)
    wuv = ukv[:, :, MLA_NOPE_DIM:].reshape(MLA_KV_RANK, -1).astype(bf16)

    half = MLA_ROPE_DIM // 2
    freqs = (ROPE_THETA ** (-np.arange(half, dtype=np.float32) / half)).astype(np.float32)
    freq = np.zeros((1, HEAD_PAD), np.float32)
    freq[0, MLA_NOPE_DIM:MLA_NOPE_DIM + half] = freqs
    freq[0, MLA_NOPE_DIM + half:dqk] = freqs

    ggrp = group_norm[l]
    ggrp = jnp.concatenate([ggrp[:GROUP_WIDTH][_SWA_PERM], ggrp[GROUP_WIDTH:]]).reshape(1, D_MIX)
    wo = w_out[l]
    wout = jnp.concatenate([wo[:GROUP_WIDTH][_SWA_PERM], wo[GROUP_WIDTH:]], axis=0).astype(bf16)
    return dict(
        gpre=norm_pre[l].reshape(1, D_MODEL), win=win,
        sinks=attn_sinks[l], convw=conv_w[l], convb=conv_b[l].reshape(1, CONV_WIDTH),
        gcq=mla_q_norm[l].reshape(1, MLA_Q_RANK), wuq=wuq, wuqs=wuqs,
        gckv=mla_kv_norm[l].reshape(1, MLA_KV_RANK), wuk=wuk, wuv=wuv, freq=jnp.asarray(freq),
        ggrp=ggrp, wout=wout, gpost=norm_post[l].reshape(1, D_MODEL))


def kernel(x, positions, norm_pre, w_in, attn_sinks, conv_w, conv_b, mla_q_norm, mla_w_uq,
           mla_kv_norm, mla_w_ukv, group_norm, w_out, norm_post):
    batch, seq, _ = x.shape
    depth = w_in.shape[0]
    assert seq % max(PROJ_ROWS, MLA_TQ, SB_TQ, OUT_ROWS) == 0
    n = batch * seq
    x2 = x.reshape(n, D_MODEL)
    pos2 = positions.astype(jnp.float32).reshape(n, 1)
    tri = jnp.asarray(np.tril(np.ones((SB_TK, SB_TK), np.float32), -1), jnp.bfloat16)
    for l in range(depth):
        lw = _prep_layer(l, norm_pre, w_in, attn_sinks, conv_w, conv_b, mla_q_norm, mla_w_uq,
                         mla_kv_norm, mla_w_ukv, group_norm, w_out, norm_post)
        p = _proj_call(x2, pos2, lw, seq)
        ya = _swa_call(lw["sinks"], p["aq"], p["ak"], p["av"], seq)
        yc = _mla_call(p["qc"], p["kc"], p["vc"], batch, seq)
        yd = _sb_call(p["dq"], p["dk"], p["dv"], tri, batch, seq)
        x2 = _out_call(x2, ya, p["yb"], yc, yd, p["gate"], lw)
    return x2.reshape(batch, seq, D_MODEL)
```

```python
import functools

import numpy as np
import jax
import jax.numpy as jnp
from jax import lax
from jax.experimental import pallas as pl
from jax.experimental.pallas import tpu as pltpu

D_MODEL = 1024
BLOCK = 128
NORM_EPS = 1e-6
SWA_HEADS = 4
SWA_HEAD_DIM = 64
CONV_WIDTH = 256
CONV_K = 3
MLA_HEADS = 4
MLA_Q_RANK = 256
MLA_KV_RANK = 128
MLA_NOPE_DIM = 64
MLA_ROPE_DIM = 32
MLA_V_DIM = 64
ROPE_THETA = 10000.0
SB_HEADS = 4
SB_HEAD_DIM = 64
GROUP_WIDTH = 256
N_GROUPS = 4
D_MIX = GROUP_WIDTH * N_GROUPS

LANES = 128
LAT_WIDTH = 256
ONES_LANE = 160
MLA_JUMP = 32.0
NEG_BIG = -1e30
SB_DEAD = -120.0

_IN_SIZES = (256, 128, 128, 256, 256, 256, 256, 128, 32, 256, 256, 256, 1024)
_IN_OFF = np.concatenate([[0], np.cumsum(_IN_SIZES)]).astype(int)
(_O_AQ, _O_AK, _O_AV, _O_BB, _O_BC, _O_BX, _O_CQ, _O_CKV, _O_CKR, _O_DQ, _O_DK, _O_DV, _O_GATE,
 _O_END) = [int(v) for v in _IN_OFF]

P_AQ, P_AK, P_AV = 0, 256, 384
P_BB, P_BC, P_BX = 512, 768, 1024
P_CQ, P_CKV, P_KR, P_KRS = 1280, 1536, 1664, 1792
P_DQ, P_DK, P_DV = 1920, 2176, 2432
P_END = 2688

_SWA_PERM = np.concatenate([np.arange(0, 64), np.arange(128, 192), np.arange(64, 128),
                            np.arange(192, 256)])

PROJ_ROWS = 512
SWA_ROWS = 512
MLA_TQ = 512
MLA_TK = 512
SB_TQ = 256
SB_TK = 256
OUT_ROWS = 512
VMEM_LIMIT = 56 * 1024 * 1024


def _rms(v, g):
    return v * lax.rsqrt(jnp.mean(v * v, axis=-1, keepdims=True) + NORM_EPS) * g


def _dot(a, b):
    return jnp.dot(a, b, preferred_element_type=jnp.float32)


def _dot_nt(a, b):
    return lax.dot_general(a, b, (((1,), (1,)), ((), ())), preferred_element_type=jnp.float32)


def _layer_spec(a, l):
    zeros = (0,) * (a.ndim - 1)
    return pl.BlockSpec((None,) + a.shape[1:], lambda *_: (l,) + zeros)


def _proj_kernel(x_ref, pos_ref, freq_ref, gpre_ref, win_ref, convw_ref, convb_ref, gcq_ref,
                 wuqn_ref, wabs_ref, wuqr_ref, wuqrs_ref, gckv_ref,
                 aq_ref, ak_ref, av_ref, yb_ref, qa_ref, kv_ref, dq_ref, dk_ref, dv_ref,
                 ubuf_ref, *, tiles_per_seq):
    tm = x_ref.shape[0]
    bf16 = jnp.bfloat16
    xb = _rms(x_ref[...], gpre_ref[...]).astype(bf16)

    def proj(lo, hi):
        return _dot(xb, win_ref[:, lo:hi])

    aq_ref[...] = proj(P_AQ, P_AK).astype(bf16)
    ak_ref[...] = proj(P_AK, P_AV).astype(bf16)
    av_ref[...] = proj(P_AV, P_BB).astype(bf16)

    dq_ref[...] = proj(P_DQ, P_DK).astype(bf16)
    dk_ref[...] = proj(P_DK, P_DV).astype(bf16)
    dv_ref[...] = proj(P_DV, P_END).astype(bf16)

    u = proj(P_BC, P_BX) * proj(P_BX, P_CQ)

    @pl.when(pl.program_id(0) % tiles_per_seq == 0)
    def _():
        ubuf_ref[0:8, :] = jnp.zeros((8, CONV_WIDTH), jnp.float32)

    ubuf_ref[8:tm + 8, :] = u
    u1 = ubuf_ref[7:tm + 7, :]
    u2 = ubuf_ref[6:tm + 6, :]
    conv = (convw_ref[0:1, :] * u2 + convw_ref[1:2, :] * u1 + convw_ref[2:3, :] * u
            + convb_ref[...])
    yb_ref[...] = proj(P_BB, P_BC) * conv
    ubuf_ref[0:8, :] = ubuf_ref[tm:tm + 8, :]

    ang = pos_ref[...] * freq_ref[...]
    cosv = jnp.cos(ang)
    sinv = jnp.sin(ang)
    qscale = (MLA_NOPE_DIM + MLA_ROPE_DIM) ** -0.5 * np.log2(np.e)
    cosq = cosv * qscale
    sinq = sinv * qscale

    cq = _rms(proj(P_CQ, P_CKV), gcq_ref[...]).astype(bf16)
    qnope = _dot(cq, wuqn_ref[...]).astype(bf16)
    qlat = _dot(qnope, wabs_ref[...])
    qrope = _dot(cq, wuqr_ref[...])
    qrswp = _dot(cq, wuqrs_ref[...])
    for h in range(MLA_HEADS):
        rows = slice(h * tm, (h + 1) * tm)
        sl = slice(h * LANES, (h + 1) * LANES)
        qa_ref[rows, 0:LANES] = (qlat[:, sl] * qscale).astype(bf16)
        qa_ref[rows, LANES:2 * LANES] = (qrope[:, sl] * cosq + qrswp[:, sl] * sinq).astype(bf16)
    kv_ref[:, 0:LANES] = _rms(proj(P_CKV, P_KR), gckv_ref[...]).astype(bf16)
    krope = proj(P_KR, P_KRS) * cosv + proj(P_KRS, P_DQ) * sinv
    ones_lane = lax.broadcasted_iota(jnp.int32, (tm, LANES), 1) == ONES_LANE - LANES
    kv_ref[:, LANES:2 * LANES] = jnp.where(ones_lane, 1.0, krope).astype(bf16)


def _proj_call(x2, pos2, pw, l, seq):
    n = x2.shape[0]
    tm = PROJ_ROWS
    row = lambda w: pl.BlockSpec((tm, w), lambda i: (i, 0))
    bf16 = jnp.bfloat16
    outs = [("aq", 1, 256, bf16), ("ak", 1, 128, bf16), ("av", 1, 128, bf16),
            ("yb", 1, 256, jnp.float32), ("qa", MLA_HEADS, LAT_WIDTH, bf16),
            ("kv", 1, LAT_WIDTH, bf16), ("dq", 1, 256, bf16), ("dk", 1, 256, bf16),
            ("dv", 1, 256, bf16)]
    params = [pw[k] for k in ("gpre", "win", "convw", "convb", "gcq", "wuqn", "wabs", "wuqr",
                              "wuqrs", "gckv")]
    res = pl.pallas_call(
        functools.partial(_proj_kernel, tiles_per_seq=seq // tm),
        grid=(n // tm,),
        in_specs=[row(D_MODEL), row(1), pl.BlockSpec(pw["freq"].shape, lambda i: (0, 0))]
        + [_layer_spec(a, l) for a in params],
        out_specs=[pl.BlockSpec((r * tm, w), lambda i: (i, 0)) for _, r, w, _ in outs],
        out_shape=[jax.ShapeDtypeStruct((r * n, w), dt) for _, r, w, dt in outs],
        scratch_shapes=[pltpu.VMEM((tm + 8, CONV_WIDTH), jnp.float32)],
        compiler_params=pltpu.CompilerParams(dimension_semantics=("arbitrary",),
                                             vmem_limit_bytes=VMEM_LIMIT),
        name="proj",
    )(x2, pos2, pw["freq"], *params)
    return {name: r for (name, _, _, _), r in zip(outs, res)}


def _swa_kernel(sink_ref, q_ref, k_ref, kp_ref, v_ref, vp_ref, o_ref, *, layer, tiles_per_seq):
    bf16 = jnp.bfloat16
    nsub = q_ref.shape[0] // BLOCK
    first_tile = (pl.program_id(0) % tiles_per_seq) == 0
    row = lax.broadcasted_iota(jnp.int32, (BLOCK, 2 * BLOCK), 0)
    col = lax.broadcasted_iota(jnp.int32, (BLOCK, 2 * BLOCK), 1)
    band = (col > row) & (col <= row + BLOCK)
    band_first = band & ((col >= BLOCK) | jnp.logical_not(first_tile))
    upper = lax.broadcasted_iota(jnp.int32, (BLOCK, LANES), 1) >= SWA_HEAD_DIM
    for j in range(nsub):
        rows = slice(j * BLOCK, (j + 1) * BLOCK)
        if j == 0:
            kk = jnp.concatenate([kp_ref[...], k_ref[0:BLOCK, :]], axis=0)
            vv = jnp.concatenate([vp_ref[...], v_ref[0:BLOCK, :]], axis=0)
            mask = band_first
        else:
            kk = k_ref[(j - 1) * BLOCK:(j + 1) * BLOCK, :]
            vv = v_ref[(j - 1) * BLOCK:(j + 1) * BLOCK, :]
            mask = band
        for lb in range(2):
            q2 = q_ref[rows, lb * LANES:(lb + 1) * LANES]
            halves = []
            for hh in range(2):
                sink = sink_ref[layer, 2 * hh + lb]
                qm = jnp.where(upper == (hh == 1), q2, jnp.zeros_like(q2))
                s = jnp.where(mask, _dot_nt(qm, kk), NEG_BIG)
                m = jnp.maximum(jnp.max(s, axis=1, keepdims=True), sink)
                p = jnp.exp(s - m)
                den = jnp.sum(p, axis=1, keepdims=True) + jnp.exp(sink - m)
                halves.append(_dot(p.astype(bf16), vv) / den)
            o_ref[rows, lb * LANES:(lb + 1) * LANES] = jnp.where(upper, halves[1], halves[0])


def _swa_call(sinks, aq, ak, av, l, seq):
    n = aq.shape[0]
    tm = SWA_ROWS
    per = tm // BLOCK
    cur = lambda w: pl.BlockSpec((tm, w), lambda i: (i, 0))
    prev = lambda w: pl.BlockSpec((BLOCK, w), lambda i: (jnp.maximum(i * per - 1, 0), 0))
    return pl.pallas_call(
        functools.partial(_swa_kernel, layer=l, tiles_per_seq=seq // tm),
        grid=(n // tm,),
        in_specs=[pl.BlockSpec(memory_space=pltpu.SMEM), cur(256), cur(128), prev(128), cur(128),
                  prev(128)],
        out_specs=cur(256),
        out_shape=jax.ShapeDtypeStruct((n, 256), jnp.float32),
        compiler_params=pltpu.CompilerParams(dimension_semantics=("arbitrary",)),
        name="swa",
    )(sinks, aq, ak, ak, av, av)


def _mla_kernel(q_ref, kv_ref, wuv_ref, o_ref, m_ref, acc_ref):
    tq, tk = MLA_TQ, MLA_TK
    bf16 = jnp.bfloat16
    rows = MLA_HEADS * tq
    i = pl.program_id(1)
    q = q_ref[...]

    def exact_update(off, masked):
        kv = kv_ref[pl.ds(off, tk), :]
        s = _dot_nt(q, kv)
        if masked:
            row = lax.broadcasted_iota(jnp.int32, (rows, tk), 0) & (tq - 1)
            col = lax.broadcasted_iota(jnp.int32, (rows, tk), 1)
            s = jnp.where(col <= row, s, NEG_BIG)
        m = m_ref[:, 0:1]
        m_new = jnp.maximum(m, jnp.max(s, axis=1, keepdims=True))
        p = jnp.exp2(s - m_new)
        acc_ref[...] = jnp.exp2(m - m_new) * acc_ref[...] + _dot(p.astype(bf16), kv)
        m_ref[...] = jnp.broadcast_to(m_new, (rows, LANES))

    m_ref[...] = jnp.full((rows, LANES), NEG_BIG, jnp.float32)
    acc_ref[...] = jnp.zeros((rows, LAT_WIDTH), jnp.float32)
    exact_update(pl.multiple_of(i * tk, tk), True)

    def body(c, carry):
        off = pl.multiple_of(c * tk, tk)
        kv = kv_ref[pl.ds(off, tk), :]
        shift = m_ref[...]
        d = _dot_nt(q, kv) - jnp.concatenate([shift] * (tk // LANES), axis=1)
        acc_new = acc_ref[...] + _dot(jnp.exp2(d).astype(bf16), kv)
        safe = jnp.max(d) <= MLA_JUMP

        @pl.when(safe)
        def _():
            acc_ref[...] = acc_new

        @pl.when(jnp.logical_not(safe))
        def _():
            exact_update(off, False)

        return carry

    lax.fori_loop(0, i, body, 0)

    acc = acc_ref[...]
    lat = (acc[:, 0:LANES] / acc[:, ONES_LANE:ONES_LANE + 1]).astype(bf16)
    y = None
    for h in range(MLA_HEADS):
        part = _dot(lat[h * tq:(h + 1) * tq, :], wuv_ref[h])
        y = part if y is None else y + part
    o_ref[...] = y


def _mla_call(qa, kv, pw, l, batch, seq):
    tq = MLA_TQ
    assert tq & (tq - 1) == 0 and PROJ_ROWS == tq
    nq = seq // tq
    rows = MLA_HEADS * tq
    kv3 = kv.reshape(batch, seq, LAT_WIDTH)
    out = pl.pallas_call(
        _mla_kernel,
        grid=(batch, nq),
        in_specs=[pl.BlockSpec((rows, LAT_WIDTH), lambda b, i: (b * nq + i, 0)),
                  pl.BlockSpec((None, seq, LAT_WIDTH), lambda b, i: (b, 0, 0)),
                  _layer_spec(pw["wuv"], l)],
        out_specs=pl.BlockSpec((None, tq, 256), lambda b, i: (b, i, 0)),
        out_shape=jax.ShapeDtypeStruct((batch, seq, 256), jnp.float32),
        scratch_shapes=[pltpu.VMEM((rows, LANES), jnp.float32),
                        pltpu.VMEM((rows, LAT_WIDTH), jnp.float32)],
        compiler_params=pltpu.CompilerParams(
            dimension_semantics=("arbitrary", "arbitrary"),
            vmem_limit_bytes=VMEM_LIMIT),
        name="mla",
    )(qa, kv3, pw["wuv"])
    return out.reshape(batch * seq, 256)


def _sb_kernel(q_ref, k_ref, v_ref, tri_ref, o_ref):
    tq, tk = SB_TQ, SB_TK
    bf16 = jnp.bfloat16
    i = pl.program_id(1)
    upper_q = lax.broadcasted_iota(jnp.int32, (tq, LANES), 1) >= SB_HEAD_DIM
    heads = [(lb, hh) for lb in range(2) for hh in range(2)]
    qs = []
    for lb, hh in heads:
        q2 = q_ref[:, lb * LANES:(lb + 1) * LANES]
        qs.append(jnp.where(upper_q == (hh == 1), q2, jnp.zeros_like(q2)))
    row = lax.broadcasted_iota(jnp.int32, (tq, tk), 0)
    col = lax.broadcasted_iota(jnp.int32, (tq, tk), 1)
    strict = col < row
    tri = tri_ref[...]

    def step(c, state, valid):
        off = pl.multiple_of(c * tk, tk)
        new = []
        for n, (lb, hh) in enumerate(heads):
            csum, acc = state[n]
            k2 = k_ref[pl.ds(off, tk), lb * LANES:(lb + 1) * LANES]
            v2 = v_ref[pl.ds(off, tk), lb * LANES:(lb + 1) * LANES]
            z = _dot_nt(qs[n], k2)
            softplus_tail = jnp.log(1.0 + jnp.exp(-jnp.abs(z)))
            log_beta = jnp.minimum(z, 0.0) - softplus_tail
            log_keep = log_beta - z
            if valid is not None:
                log_keep = jnp.where(valid, log_keep, 0.0)
            hi = log_keep.astype(bf16)
            lo = (log_keep - hi.astype(jnp.float32)).astype(bf16)
            within = _dot(hi, tri) + _dot(lo, tri)
            a = jnp.exp(log_beta + (csum + within))
            if valid is not None:
                a = jnp.where(valid, a, 0.0)
            acc = acc + _dot(a.astype(bf16), v2)
            csum = csum + jnp.sum(log_keep, axis=1, keepdims=True)
            new.append((csum, acc))
        return tuple(new)

    def least_decayed(state):
        worst = state[0][0]
        for csum, _ in state[1:]:
            worst = jnp.maximum(worst, csum)
        return jnp.max(worst)

    init = tuple((jnp.zeros((tq, 1), jnp.float32), jnp.zeros((tq, LANES), jnp.float32))
                 for _ in heads)
    state = step(i, init, strict)
    state = step(jnp.maximum(i - 1, 0), state, jnp.broadcast_to(i > 0, (tq, tk)))

    def cond(carry):
        t, worst, _ = carry
        return jnp.logical_and(t < i - 1, worst > SB_DEAD)

    def body(carry):
        t, _, st = carry
        st = step(i - 2 - t, st, None)
        return t + 1, least_decayed(st), st

    _, _, state = lax.while_loop(cond, body, (jnp.int32(0), least_decayed(state), state))
    for lb in range(2):
        o_ref[:, lb * LANES:(lb + 1) * LANES] = jnp.where(upper_q, state[2 * lb + 1][1],
                                                          state[2 * lb][1])


def _sb_call(dq, dk, dv, tri, batch, seq):
    tq = SB_TQ
    nq = seq // tq
    q3 = dq.reshape(batch, seq, 256)
    k3 = dk.reshape(batch, seq, 256)
    v3 = dv.reshape(batch, seq, 256)
    out = pl.pallas_call(
        _sb_kernel,
        grid=(batch, nq),
        in_specs=[pl.BlockSpec((None, tq, 256), lambda b, i: (b, i, 0)),
                  pl.BlockSpec((None, seq, 256), lambda b, i: (b, 0, 0)),
                  pl.BlockSpec((None, seq, 256), lambda b, i: (b, 0, 0)),
                  pl.BlockSpec((SB_TK, SB_TK), lambda b, i: (0, 0))],
        out_specs=pl.BlockSpec((None, tq, 256), lambda b, i: (b, i, 0)),
        out_shape=jax.ShapeDtypeStruct((batch, seq, 256), jnp.float32),
        compiler_params=pltpu.CompilerParams(
            dimension_semantics=("arbitrary", "arbitrary"),
            vmem_limit_bytes=VMEM_LIMIT),
        name="sb",
    )(q3, k3, v3, tri)
    return out.reshape(batch * seq, 256)


def _out_kernel(x_ref, ya_ref, yb_ref, yc_ref, yd_ref, gpre_ref, wgate_ref, ggrp_ref, wout_ref,
                gpost_ref, o_ref):
    bf16 = jnp.bfloat16
    x = x_ref[...]
    xb = _rms(x, gpre_ref[...]).astype(bf16)
    d = None
    for g, y_ref in enumerate((ya_ref, yb_ref, yc_ref, yd_ref)):
        sl = slice(g * GROUP_WIDTH, (g + 1) * GROUP_WIDTH)
        gate = _dot(xb, wgate_ref[:, sl])
        y = _rms(y_ref[...], ggrp_ref[:, sl]) * (gate * jax.nn.sigmoid(gate))
        part = _dot(y.astype(bf16), wout_ref[sl, :])
        d = part if d is None else d + part
    o_ref[...] = x + _rms(d, gpost_ref[...])


def _out_call(x2, ya, yb, yc, yd, pw, l):
    n = x2.shape[0]
    tm = OUT_ROWS
    row = lambda w: pl.BlockSpec((tm, w), lambda i: (i, 0))
    params = [pw[k] for k in ("gpre", "wgate", "ggrp", "wout", "gpost")]
    return pl.pallas_call(
        _out_kernel,
        grid=(n // tm,),
        in_specs=[row(D_MODEL), row(256), row(256), row(256), row(256)]
        + [_layer_spec(a, l) for a in params],
        out_specs=row(D_MODEL),
        out_shape=jax.ShapeDtypeStruct((n, D_MODEL), jnp.float32),
        compiler_params=pltpu.CompilerParams(dimension_semantics=("arbitrary",),
                                             vmem_limit_bytes=VMEM_LIMIT),
        name="out",
    )(x2, ya, yb, yc, yd, *params)


def _rope_swap(w):
    half = MLA_ROPE_DIM // 2
    return jnp.concatenate([-w[..., half:], w[..., :half]], axis=-1)


def _prep_weights(norm_pre, w_in, conv_w, conv_b, mla_q_norm, mla_w_uq, mla_kv_norm, mla_w_ukv,
                  group_norm, w_out, norm_post):
    f32, bf16 = jnp.float32, jnp.bfloat16
    depth = w_in.shape[0]
    w = w_in
    zeros = lambda c: jnp.zeros((depth, D_MODEL, c), f32)
    wkr = w[..., _O_CKR:_O_DQ]
    dqk = MLA_NOPE_DIM + MLA_ROPE_DIM
    kr_pad = lambda m: [m, zeros(LANES - MLA_ROPE_DIM)]
    win = jnp.concatenate(
        [w[..., _O_AQ:_O_AK][..., _SWA_PERM] * (SWA_HEAD_DIM ** -0.5),
         w[..., _O_AK:_O_CKR]]
        + kr_pad(wkr) + kr_pad(_rope_swap(wkr))
        + [w[..., _O_DQ:_O_DK] * (SB_HEAD_DIM ** -0.5),
           w[..., _O_DK:_O_GATE]], axis=-1).astype(bf16)
    wgate = jnp.concatenate([w[..., _O_GATE:_O_GATE + GROUP_WIDTH][..., _SWA_PERM],
                             w[..., _O_GATE + GROUP_WIDTH:_O_END]], axis=-1).astype(bf16)

    uq = mla_w_uq.reshape(depth, MLA_Q_RANK, MLA_HEADS, dqk)
    zq = jnp.zeros((depth, MLA_Q_RANK, MLA_HEADS, LANES - MLA_ROPE_DIM), f32)
    rope_cols = lambda m: jnp.concatenate([m, zq], axis=-1).reshape(depth, MLA_Q_RANK, -1)
    wuqn = uq[..., :MLA_NOPE_DIM].reshape(depth, MLA_Q_RANK, -1).astype(bf16)
    wuqr = rope_cols(uq[..., MLA_NOPE_DIM:]).astype(bf16)
    wuqrs = rope_cols(_rope_swap(uq[..., MLA_NOPE_DIM:])).astype(bf16)
    ukv = mla_w_ukv.reshape(depth, MLA_KV_RANK, MLA_HEADS, MLA_NOPE_DIM + MLA_V_DIM)
    own_head = jnp.eye(MLA_HEADS, dtype=f32)[None, :, None, :, None]
    uk_t = jnp.transpose(ukv[..., :MLA_NOPE_DIM], (0, 2, 3, 1))
    wabs = (uk_t[:, :, :, None, :] * own_head).reshape(
        depth, MLA_HEADS * MLA_NOPE_DIM, MLA_HEADS * MLA_KV_RANK).astype(bf16)
    uv = jnp.transpose(ukv[..., MLA_NOPE_DIM:], (0, 2, 1, 3))
    wuv = (uv[:, :, :, None, :] * own_head).reshape(
        depth, MLA_HEADS, MLA_KV_RANK, MLA_HEADS * MLA_V_DIM).astype(bf16)

    half = MLA_ROPE_DIM // 2
    freqs = (ROPE_THETA ** (-np.arange(half, dtype=np.float32) / half)).astype(np.float32)
    freq = np.zeros((1, LANES), np.float32)
    freq[0, 0:half] = freqs
    freq[0, half:MLA_ROPE_DIM] = freqs

    ggrp = jnp.concatenate([group_norm[:, :GROUP_WIDTH][:, _SWA_PERM],
                            group_norm[:, GROUP_WIDTH:]], axis=1).reshape(depth, 1, D_MIX)
    wout = jnp.concatenate([w_out[:, :GROUP_WIDTH][:, _SWA_PERM], w_out[:, GROUP_WIDTH:]],
                           axis=1).astype(bf16)
    return dict(
        gpre=norm_pre.reshape(depth, 1, D_MODEL), win=win, wgate=wgate,
        convw=conv_w, convb=conv_b.reshape(depth, 1, CONV_WIDTH),
        gcq=mla_q_norm.reshape(depth, 1, MLA_Q_RANK), wuqn=wuqn, wabs=wabs, wuqr=wuqr,
        wuqrs=wuqrs, gckv=mla_kv_norm.reshape(depth, 1, MLA_KV_RANK), wuv=wuv,
        freq=jnp.asarray(freq), ggrp=ggrp, wout=wout,
        gpost=norm_post.reshape(depth, 1, D_MODEL))


def kernel(x, positions, norm_pre, w_in, attn_sinks, conv_w, conv_b, mla_q_norm, mla_w_uq,
           mla_kv_norm, mla_w_ukv, group_norm, w_out, norm_post):
    batch, seq, _ = x.shape
    depth = w_in.shape[0]
    assert seq % max(PROJ_ROWS, SWA_ROWS, MLA_TQ, SB_TQ, OUT_ROWS) == 0
    assert MLA_TQ == MLA_TK and SB_TQ == SB_TK
    n = batch * seq
    x2 = x.reshape(n, D_MODEL)
    pos2 = positions.astype(jnp.float32).reshape(n, 1)
    tri = jnp.asarray(np.tril(np.ones((SB_TK, SB_TK), np.float32), -1), jnp.bfloat16)
    pw = _prep_weights(norm_pre, w_in, conv_w, conv_b, mla_q_norm, mla_w_uq, mla_kv_norm,
                       mla_w_ukv, group_norm, w_out, norm_post)
    for l in range(depth):
        p = _proj_call(x2, pos2, pw, l, seq)
        ya = _swa_call(attn_sinks, p["aq"], p["ak"], p["av"], l, seq)
        yc = _mla_call(p["qa"], p["kv"], pw, l, batch, seq)
        yd = _sb_call(p["dq"], p["dk"], p["dv"], tri, batch, seq)
        x2 = _out_call(x2, ya, p["yb"], yc, yd, pw, l)
    return x2.reshape(batch, seq, D_MODEL)
```

```python
import functools

import numpy as np
import jax
import jax.numpy as jnp
from jax import lax
from jax.experimental import pallas as pl
from jax.experimental.pallas import tpu as pltpu

D_MODEL = 1024
BLOCK = 128
NORM_EPS = 1e-6
SWA_HEADS = 4
SWA_HEAD_DIM = 64
CONV_WIDTH = 256
CONV_K = 3
MLA_HEADS = 4
MLA_Q_RANK = 256
MLA_KV_RANK = 128
MLA_NOPE_DIM = 64
MLA_ROPE_DIM = 32
MLA_V_DIM = 64
ROPE_THETA = 10000.0
SB_HEADS = 4
SB_HEAD_DIM = 64
GROUP_WIDTH = 256
N_GROUPS = 4
D_MIX = GROUP_WIDTH * N_GROUPS

LANES = 128
LAT_WIDTH = 256
ONES_LANE = 160
MLA_JUMP = 32.0
MLA_MIN_SUM = 2.0 ** -64
NEG_BIG = -1e30
SB_DEAD = -160.0

_IN_SIZES = (256, 128, 128, 256, 256, 256, 256, 128, 32, 256, 256, 256, 1024)
_IN_OFF = np.concatenate([[0], np.cumsum(_IN_SIZES)]).astype(int)
(_O_AQ, _O_AK, _O_AV, _O_BB, _O_BC, _O_BX, _O_CQ, _O_CKV, _O_CKR, _O_DQ, _O_DK, _O_DV, _O_GATE,
 _O_END) = [int(v) for v in _IN_OFF]

D_IN = _O_END
P_KR_END = _O_CKR + LANES
P_DQ, P_DK, P_DV, P_END = P_KR_END, P_KR_END + 256, P_KR_END + 512, P_KR_END + 768

ROPE_ROWS = 1024
PROJ_ROWS = 512
SWA_ROWS = 512
MLA_TQ = 512
MLA_TK = 512
SB_TQ = 256
SB_TK = 256
OUT_ROWS = 512
VMEM_LIMIT = 56 * 1024 * 1024


def _rms(v, g):
    return v * lax.rsqrt(jnp.mean(v * v, axis=-1, keepdims=True) + NORM_EPS) * g


def _dot(a, b):
    return jnp.dot(a, b, preferred_element_type=jnp.float32)


def _dot_nt(a, b):
    return lax.dot_general(a, b, (((1,), (1,)), ((), ())), preferred_element_type=jnp.float32)


def _layer_spec(a, l, single=False):
    zeros = (0,) * (a.ndim - 1)
    mode = dict(pipeline_mode=pl.Buffered(1)) if single else {}
    return pl.BlockSpec((None,) + a.shape[1:], lambda *_: (l,) + zeros, **mode)


def _rope_kernel(pos_ref, freq_ref, cos_ref, sin_ref):
    ang = pos_ref[...] * freq_ref[...]
    cos_ref[...] = jnp.cos(ang)
    sin_ref[...] = jnp.sin(ang)


def _rope_call(pos2, freq):
    n = pos2.shape[0]
    tm = ROPE_ROWS
    table = jax.ShapeDtypeStruct((n, LANES), jnp.float32)
    return pl.pallas_call(
        _rope_kernel,
        grid=(n // tm,),
        in_specs=[pl.BlockSpec((tm, 1), lambda i: (i, 0)),
                  pl.BlockSpec(freq.shape, lambda i: (0, 0))],
        out_specs=[pl.BlockSpec((tm, LANES), lambda i: (i, 0))] * 2,
        out_shape=[table, table],
        compiler_params=pltpu.CompilerParams(dimension_semantics=("arbitrary",)),
        name="rope",
    )(pos2, freq)


def _proj_kernel(x_ref, cos_ref, sin_ref, gpre_ref, win_ref, convw_ref, convb_ref, gcq_ref,
                 wuqn_ref, wabs_ref, wuqr_ref, gckv_ref,
                 aq_ref, ak_ref, av_ref, yb_ref, qa_ref, kv_ref, dq_ref, dk_ref, dv_ref,
                 wbf_ref, ubuf_ref, *, tiles_per_seq):
    tm = x_ref.shape[0]
    bf16 = jnp.bfloat16

    @pl.when(pl.program_id(0) == 0)
    def _():
        wbf_ref[:, _O_AQ:_O_AK] = (win_ref[:, _O_AQ:_O_AK] * SWA_HEAD_DIM ** -0.5).astype(bf16)
        wbf_ref[:, _O_AK:P_KR_END] = win_ref[:, _O_AK:P_KR_END].astype(bf16)
        wbf_ref[:, P_DQ:P_DK] = (win_ref[:, _O_DQ:_O_DK]
                                 * (SB_HEAD_DIM ** -0.5 * np.log2(np.e))).astype(bf16)
        wbf_ref[:, P_DK:P_END] = win_ref[:, _O_DK:_O_GATE].astype(bf16)

    xb = _rms(x_ref[...], gpre_ref[...]).astype(bf16)

    def proj(lo, hi):
        return _dot(xb, wbf_ref[:, lo:hi])

    aq_ref[...] = proj(_O_AQ, _O_AK).astype(bf16)
    akv = proj(_O_AK, _O_BB)
    for ref, blk in ((ak_ref, akv[:, 0:LANES]), (av_ref, akv[:, LANES:2 * LANES])):
        ref[:, 0:LANES] = blk.astype(bf16)
        ref[:, LANES:2 * LANES] = pltpu.roll(blk, SWA_HEAD_DIM, axis=1).astype(bf16)

    dq_ref[...] = proj(P_DQ, P_DK).astype(bf16)
    dk_ref[...] = proj(P_DK, P_DV).astype(bf16)
    dv_ref[...] = proj(P_DV, P_END).astype(bf16)

    ckr = proj(_O_CKV, P_KR_END)

    u = proj(_O_BC, _O_BX) * proj(_O_BX, _O_CQ)

    @pl.when(pl.program_id(0) % tiles_per_seq == 0)
    def _():
        ubuf_ref[0:8, :] = jnp.zeros((8, CONV_WIDTH), jnp.float32)

    ubuf_ref[8:tm + 8, :] = u
    u1 = ubuf_ref[7:tm + 7, :]
    u2 = ubuf_ref[6:tm + 6, :]
    conv = (convw_ref[0:1, :] * u2 + convw_ref[1:2, :] * u1 + convw_ref[2:3, :] * u
            + convb_ref[...])
    yb_ref[...] = proj(_O_BB, _O_BC) * conv
    ubuf_ref[0:8, :] = ubuf_ref[tm:tm + 8, :]

    cosv = cos_ref[...]
    sinv = sin_ref[...]
    lane = lax.broadcasted_iota(jnp.int32, (tm, LANES), 1)
    in_rope = lane < MLA_ROPE_DIM
    qscale = (MLA_NOPE_DIM + MLA_ROPE_DIM) ** -0.5 * np.log2(np.e)

    cq = _rms(proj(_O_CQ, _O_CKV), gcq_ref[...]).astype(bf16)
    qnope = _dot(cq, wuqn_ref[...]).astype(bf16)
    qlat = _dot(qnope, wabs_ref[...])
    qr = _dot(cq, wuqr_ref[...])
    qroped = (qr[:, 0:LANES] * cosv + qr[:, LANES:2 * LANES] * sinv) * qscale
    for h in range(MLA_HEADS):
        rows = slice(h * tm, (h + 1) * tm)
        qa_ref[rows, 0:LANES] = (qlat[:, h * LANES:(h + 1) * LANES] * qscale).astype(bf16)
        mine = qroped if h == 0 else pltpu.roll(qroped, LANES - h * MLA_ROPE_DIM, axis=1)
        qa_ref[rows, LANES:2 * LANES] = jnp.where(in_rope, mine, 0.0).astype(bf16)
    kv_ref[:, 0:LANES] = _rms(ckr[:, 0:LANES], gckv_ref[...]).astype(bf16)
    kr = ckr[:, LANES:2 * LANES]
    half = MLA_ROPE_DIM // 2
    partner = jnp.where(lane < half, -pltpu.roll(kr, LANES - half, axis=1),
                        pltpu.roll(kr, half, axis=1))
    krope = jnp.where(in_rope, kr * cosv + partner * sinv, 0.0)
    kv_ref[:, LANES:2 * LANES] = jnp.where(lane == ONES_LANE - LANES, 1.0, krope).astype(bf16)


def _proj_call(x2, rope, pw, l, seq):
    n = x2.shape[0]
    tm = PROJ_ROWS
    row = lambda w: pl.BlockSpec((tm, w), lambda i: (i, 0))
    bf16 = jnp.bfloat16
    outs = [("aq", 1, 256, bf16), ("ak", 1, 256, bf16), ("av", 1, 256, bf16),
            ("yb", 1, 256, jnp.float32), ("qa", MLA_HEADS, LAT_WIDTH, bf16),
            ("kv", 1, LAT_WIDTH, bf16), ("dq", 1, 256, bf16), ("dk", 1, 256, bf16),
            ("dv", 1, 256, bf16)]
    params = [pw[k] for k in ("gpre", "win", "convw", "convb", "gcq", "wuqn", "wabs", "wuqr",
                              "gckv")]
    res = pl.pallas_call(
        functools.partial(_proj_kernel, tiles_per_seq=seq // tm),
        grid=(n // tm,),
        in_specs=[row(D_MODEL), row(LANES), row(LANES)]
        + [_layer_spec(a, l, single=(k == "win")) for k, a in zip(
            ("gpre", "win", "convw", "convb", "gcq", "wuqn", "wabs", "wuqr", "gckv"), params)],
        out_specs=[pl.BlockSpec((r * tm, w), lambda i: (i, 0)) for _, r, w, _ in outs],
        out_shape=[jax.ShapeDtypeStruct((r * n, w), dt) for _, r, w, dt in outs],
        scratch_shapes=[pltpu.VMEM((D_MODEL, P_END), bf16),
                        pltpu.VMEM((tm + 8, CONV_WIDTH), jnp.float32)],
        compiler_params=pltpu.CompilerParams(dimension_semantics=("arbitrary",),
                                             vmem_limit_bytes=VMEM_LIMIT),
        name="proj",
    )(x2, *rope, *params)
    return {name: r for (name, _, _, _), r in zip(outs, res)}


def _swa_kernel(sink_ref, q_ref, k_ref, kp_ref, v_ref, vp_ref, o_ref, *, layer, tiles_per_seq):
    bf16 = jnp.bfloat16
    nsub = q_ref.shape[0] // BLOCK
    first_tile = (pl.program_id(0) % tiles_per_seq) == 0
    row = lax.broadcasted_iota(jnp.int32, (BLOCK, 2 * BLOCK), 0)
    col = lax.broadcasted_iota(jnp.int32, (BLOCK, 2 * BLOCK), 1)
    band = (col > row) & (col <= row + BLOCK)
    band_first = band & ((col >= BLOCK) | jnp.logical_not(first_tile))
    upper = lax.broadcasted_iota(jnp.int32, (BLOCK, LANES), 1) >= SWA_HEAD_DIM
    for j in range(nsub):
        rows = slice(j * BLOCK, (j + 1) * BLOCK)
        if j == 0:
            kk = jnp.concatenate([kp_ref[...], k_ref[0:BLOCK, :]], axis=0)
            vv = jnp.concatenate([vp_ref[...], v_ref[0:BLOCK, :]], axis=0)
            mask = band_first
        else:
            kk = k_ref[(j - 1) * BLOCK:(j + 1) * BLOCK, :]
            vv = v_ref[(j - 1) * BLOCK:(j + 1) * BLOCK, :]
            mask = band
        for lb in range(2):
            q2 = q_ref[rows, lb * LANES:(lb + 1) * LANES]
            halves = []
            for hh in range(2):
                sink = sink_ref[layer, 2 * lb + hh]
                qm = jnp.where(upper == (hh == 1), q2, jnp.zeros_like(q2))
                sel = slice(0, LANES) if lb == hh else slice(LANES, 2 * LANES)
                s = jnp.where(mask, _dot_nt(qm, kk[:, sel]), NEG_BIG)
                m = jnp.maximum(jnp.max(s, axis=1, keepdims=True), sink)
                p = jnp.exp(s - m)
                den = jnp.sum(p, axis=1, keepdims=True) + jnp.exp(sink - m)
                halves.append(_dot(p.astype(bf16), vv[:, sel]) / den)
            o_ref[rows, lb * LANES:(lb + 1) * LANES] = jnp.where(upper, halves[1], halves[0])


def _swa_call(sinks, aq, ak, av, l, seq):
    n = aq.shape[0]
    tm = SWA_ROWS
    per = tm // BLOCK
    cur = lambda w: pl.BlockSpec((tm, w), lambda i: (i, 0))
    prev = lambda w: pl.BlockSpec((BLOCK, w), lambda i: (jnp.maximum(i * per - 1, 0), 0))
    return pl.pallas_call(
        functools.partial(_swa_kernel, layer=l, tiles_per_seq=seq // tm),
        grid=(n // tm,),
        in_specs=[pl.BlockSpec(memory_space=pltpu.SMEM), cur(256), cur(256), prev(256), cur(256),
                  prev(256)],
        out_specs=cur(256),
        out_shape=jax.ShapeDtypeStruct((n, 256), jnp.float32),
        compiler_params=pltpu.CompilerParams(dimension_semantics=("arbitrary",)),
        name="swa",
    )(sinks, aq, ak, ak, av, av)


def _mla_kernel(q_ref, kv_ref, wuv_ref, o_ref, m_ref, acc_ref):
    tq, tk = MLA_TQ, MLA_TK
    bf16 = jnp.bfloat16
    rows = MLA_HEADS * tq
    i = pl.program_id(1)
    q = q_ref[...]

    def exact_update(off, masked):
        kv = kv_ref[pl.ds(off, tk), :]
        s = _dot_nt(q, kv)
        if masked:
            row = lax.broadcasted_iota(jnp.int32, (rows, tk), 0) & (tq - 1)
            col = lax.broadcasted_iota(jnp.int32, (rows, tk), 1)
            s = jnp.where(col <= row, s, NEG_BIG)
        m = m_ref[:, 0:1]
        m_new = jnp.maximum(m, jnp.max(s, axis=1, keepdims=True))
        p = jnp.exp2(s - m_new)
        acc_ref[...] = jnp.exp2(m - m_new) * acc_ref[...] + _dot(p.astype(bf16), kv)
        m_ref[...] = jnp.broadcast_to(m_new, (rows, LANES))

    def lazy_update(off, width):
        kv = kv_ref[pl.ds(off, width), :]
        d = _dot_nt(q, kv) - jnp.concatenate([m_ref[...]] * (width // LANES), axis=1)
        acc_new = acc_ref[...] + _dot(jnp.exp2(d).astype(bf16), kv)
        safe = jnp.max(d) <= MLA_JUMP

        @pl.when(safe)
        def _():
            acc_ref[...] = acc_new

        @pl.when(jnp.logical_not(safe))
        def _():
            def redo(j, carry):
                exact_update(pl.multiple_of(off + j * tk, tk), False)
                return carry
            lax.fori_loop(0, width // tk, redo, 0)

    off_d = pl.multiple_of(i * tk, tk)
    kv_d = kv_ref[pl.ds(off_d, tk), :]
    row = lax.broadcasted_iota(jnp.int32, (rows, tk), 0) & (tq - 1)
    col = lax.broadcasted_iota(jnp.int32, (rows, tk), 1)
    d0 = jnp.where(col <= row, _dot_nt(q, kv_d), NEG_BIG)
    acc0 = _dot(jnp.exp2(d0).astype(bf16), kv_d)
    sums = acc0[:, LANES:2 * LANES]
    ones_lane = lax.broadcasted_iota(jnp.int32, (rows, LANES), 1) == ONES_LANE - LANES
    smallest = jnp.min(jnp.where(ones_lane, sums, 1.0))
    fine = jnp.logical_and(jnp.max(d0) <= MLA_JUMP, smallest >= MLA_MIN_SUM)

    @pl.when(fine)
    def _():
        m_ref[...] = jnp.zeros((rows, LANES), jnp.float32)
        acc_ref[...] = acc0

    @pl.when(jnp.logical_not(fine))
    def _():
        m_ref[...] = jnp.full((rows, LANES), NEG_BIG, jnp.float32)
        acc_ref[...] = jnp.zeros((rows, LAT_WIDTH), jnp.float32)
        exact_update(off_d, True)

    def pair(c, carry):
        lazy_update(pl.multiple_of(c * 2 * tk, 2 * tk), 2 * tk)
        return carry

    lax.fori_loop(0, i // 2, pair, 0)

    @pl.when(i % 2 == 1)
    def _():
        lazy_update(pl.multiple_of((i - 1) * tk, tk), tk)

    acc = acc_ref[...]
    lat = (acc[:, 0:LANES] / acc[:, ONES_LANE:ONES_LANE + 1]).astype(bf16)
    y = None
    for h in range(MLA_HEADS):
        part = _dot(lat[h * tq:(h + 1) * tq, :], wuv_ref[h])
        y = part if y is None else y + part
    o_ref[...] = y


def _mla_call(qa, kv, pw, l, batch, seq):
    tq = MLA_TQ
    assert tq & (tq - 1) == 0 and PROJ_ROWS == tq
    nq = seq // tq
    rows = MLA_HEADS * tq
    kv3 = kv.reshape(batch, seq, LAT_WIDTH)
    out = pl.pallas_call(
        _mla_kernel,
        grid=(batch, nq),
        in_specs=[pl.BlockSpec((rows, LAT_WIDTH), lambda b, i: (b * nq + i, 0)),
                  pl.BlockSpec((None, seq, LAT_WIDTH), lambda b, i: (b, 0, 0)),
                  _layer_spec(pw["wuv"], l)],
        out_specs=pl.BlockSpec((None, tq, 256), lambda b, i: (b, i, 0)),
        out_shape=jax.ShapeDtypeStruct((batch, seq, 256), jnp.float32),
        scratch_shapes=[pltpu.VMEM((rows, LANES), jnp.float32),
                        pltpu.VMEM((rows, LAT_WIDTH), jnp.float32)],
        compiler_params=pltpu.CompilerParams(
            dimension_semantics=("arbitrary", "arbitrary"),
            vmem_limit_bytes=VMEM_LIMIT),
        name="mla",
    )(qa, kv3, pw["wuv"])
    return out.reshape(batch * seq, 256)


def _sb_kernel(q_ref, k_ref, v_ref, tri_ref, o_ref):
    tq, tk = SB_TQ, SB_TK
    bf16 = jnp.bfloat16
    i = pl.program_id(1)
    upper_q = lax.broadcasted_iota(jnp.int32, (tq, LANES), 1) >= SB_HEAD_DIM
    qs = []
    for lb in range(2):
        q2 = q_ref[:, lb * LANES:(lb + 1) * LANES]
        zero = jnp.zeros_like(q2)
        qs.append(jnp.concatenate([jnp.where(upper_q, zero, q2), jnp.where(upper_q, q2, zero)],
                                  axis=0))
    row = lax.broadcasted_iota(jnp.int32, (2 * tq, tk), 0) & (tq - 1)
    col = lax.broadcasted_iota(jnp.int32, (2 * tq, tk), 1)
    strict = col < row
    tri = tri_ref[...]

    def step(c, state, masked):
        off = pl.multiple_of(c * tk, tk)
        new = []
        for lb in range(2):
            csum, acc = state[lb]
            k2 = k_ref[pl.ds(off, tk), lb * LANES:(lb + 1) * LANES]
            v2 = v_ref[pl.ds(off, tk), lb * LANES:(lb + 1) * LANES]
            z = _dot_nt(qs[lb], k2)
            if masked:
                z = jnp.where(strict, z, NEG_BIG)
            log_beta = jnp.minimum(z, 0.0) - jnp.log2(1.0 + jnp.exp2(-jnp.abs(z)))
            log_keep = log_beta - z
            hi = log_keep.astype(bf16)
            lo = (log_keep - hi.astype(jnp.float32)).astype(bf16)
            both = _dot(jnp.concatenate([hi, lo], axis=0), tri)
            within = both[0:2 * tq] + both[2 * tq:4 * tq]
            a = jnp.exp2(log_beta + (csum + within))
            acc = acc + _dot(a.astype(bf16), v2)
            csum = csum + jnp.sum(log_keep, axis=1, keepdims=True)
            new.append((csum, acc))
        return tuple(new)

    def least_decayed(state):
        return jnp.max(jnp.maximum(state[0][0], state[1][0]))

    init = tuple((jnp.zeros((2 * tq, 1), jnp.float32), jnp.zeros((2 * tq, LANES), jnp.float32))
                 for _ in range(2))
    state = step(i, init, True)
    gone = jnp.where(i > 0, 0.0, NEG_BIG)
    state = step(jnp.maximum(i - 1, 0), tuple((csum + gone, acc) for csum, acc in state), False)

    def cond(carry):
        t, worst, _ = carry
        return jnp.logical_and(t < i - 1, worst > SB_DEAD)

    def body(carry):
        t, _, st = carry
        st = step(i - 2 - t, st, False)
        return t + 1, least_decayed(st), st

    _, _, state = lax.while_loop(cond, body, (jnp.int32(0), least_decayed(state), state))
    for lb in range(2):
        acc = state[lb][1]
        o_ref[:, lb * LANES:(lb + 1) * LANES] = jnp.where(upper_q, acc[tq:2 * tq], acc[0:tq])


def _sb_call(dq, dk, dv, tri, batch, seq):
    tq = SB_TQ
    nq = seq // tq
    q3 = dq.reshape(batch, seq, 256)
    k3 = dk.reshape(batch, seq, 256)
    v3 = dv.reshape(batch, seq, 256)
    out = pl.pallas_call(
        _sb_kernel,
        grid=(batch, nq),
        in_specs=[pl.BlockSpec((None, tq, 256), lambda b, i: (b, i, 0)),
                  pl.BlockSpec((None, seq, 256), lambda b, i: (b, 0, 0)),
                  pl.BlockSpec((None, seq, 256), lambda b, i: (b, 0, 0)),
                  pl.BlockSpec((SB_TK, SB_TK), lambda b, i: (0, 0))],
        out_specs=pl.BlockSpec((None, tq, 256), lambda b, i: (b, i, 0)),
        out_shape=jax.ShapeDtypeStruct((batch, seq, 256), jnp.float32),
        compiler_params=pltpu.CompilerParams(
            dimension_semantics=("arbitrary", "arbitrary"),
            vmem_limit_bytes=VMEM_LIMIT),
        name="sb",
    )(q3, k3, v3, tri)
    return out.reshape(batch * seq, 256)


def _out_kernel(x_ref, ya_ref, yb_ref, yc_ref, yd_ref, gpre_ref, win_ref, ggrp_ref, wout_ref,
                gpost_ref, o_ref, wgate_ref):
    bf16 = jnp.bfloat16

    @pl.when(pl.program_id(0) == 0)
    def _():
        wgate_ref[...] = win_ref[:, _O_GATE:_O_END].astype(bf16)

    x = x_ref[...]
    xb = _rms(x, gpre_ref[...]).astype(bf16)
    d = None
    for g, y_ref in enumerate((ya_ref, yb_ref, yc_ref, yd_ref)):
        sl = slice(g * GROUP_WIDTH, (g + 1) * GROUP_WIDTH)
        gate = _dot(xb, wgate_ref[:, sl])
        y = _rms(y_ref[...], ggrp_ref[:, sl]) * (gate * jax.nn.sigmoid(gate))
        part = _dot(y.astype(bf16), wout_ref[sl, :])
        d = part if d is None else d + part
    o_ref[...] = x + _rms(d, gpost_ref[...])


def _out_call(x2, ya, yb, yc, yd, pw, l):
    n = x2.shape[0]
    tm = OUT_ROWS
    row = lambda w: pl.BlockSpec((tm, w), lambda i: (i, 0))
    names = ("gpre", "win", "ggrp", "wout", "gpost")
    params = [pw[k] for k in names]
    return pl.pallas_call(
        _out_kernel,
        grid=(n // tm,),
        in_specs=[row(D_MODEL), row(256), row(256), row(256), row(256)]
        + [_layer_spec(a, l, single=(k == "win")) for k, a in zip(names, params)],
        out_specs=row(D_MODEL),
        out_shape=jax.ShapeDtypeStruct((n, D_MODEL), jnp.float32),
        scratch_shapes=[pltpu.VMEM((D_MODEL, D_MIX), jnp.bfloat16)],
        compiler_params=pltpu.CompilerParams(dimension_semantics=("arbitrary",),
                                             vmem_limit_bytes=VMEM_LIMIT),
        name="out",
    )(x2, ya, yb, yc, yd, *params)


def _rope_swap(w):
    half = MLA_ROPE_DIM // 2
    return jnp.concatenate([-w[..., half:], w[..., :half]], axis=-1)


def _prep_weights(norm_pre, w_in, conv_w, conv_b, mla_q_norm, mla_w_uq, mla_kv_norm, mla_w_ukv,
                  group_norm, w_out, norm_post):
    f32, bf16 = jnp.float32, jnp.bfloat16
    depth = w_in.shape[0]
    dqk = MLA_NOPE_DIM + MLA_ROPE_DIM
    uq = mla_w_uq.reshape(depth, MLA_Q_RANK, MLA_HEADS, dqk)
    wuqn = uq[..., :MLA_NOPE_DIM].reshape(depth, MLA_Q_RANK, -1).astype(bf16)
    wuqr = jnp.concatenate(
        [uq[..., MLA_NOPE_DIM:].reshape(depth, MLA_Q_RANK, -1),
         _rope_swap(uq[..., MLA_NOPE_DIM:]).reshape(depth, MLA_Q_RANK, -1)], axis=-1).astype(bf16)
    ukv = mla_w_ukv.reshape(depth, MLA_KV_RANK, MLA_HEADS, MLA_NOPE_DIM + MLA_V_DIM)
    own_head = jnp.eye(MLA_HEADS, dtype=f32)[None, :, None, :, None]
    uk_t = jnp.transpose(ukv[..., :MLA_NOPE_DIM], (0, 2, 3, 1))
    wabs = (uk_t[:, :, :, None, :] * own_head).reshape(
        depth, MLA_HEADS * MLA_NOPE_DIM, MLA_HEADS * MLA_KV_RANK).astype(bf16)
    uv = jnp.transpose(ukv[..., MLA_NOPE_DIM:], (0, 2, 1, 3))
    wuv = (uv[:, :, :, None, :] * own_head).reshape(
        depth, MLA_HEADS, MLA_KV_RANK, MLA_HEADS * MLA_V_DIM).astype(bf16)

    half = MLA_ROPE_DIM // 2
    freqs = (ROPE_THETA ** (-np.arange(half, dtype=np.float32) / half)).astype(np.float32)
    freq = np.tile(np.concatenate([freqs, freqs]), LANES // MLA_ROPE_DIM).reshape(1, LANES)

    return dict(
        gpre=norm_pre.reshape(depth, 1, D_MODEL), win=w_in,
        convw=conv_w, convb=conv_b.reshape(depth, 1, CONV_WIDTH),
        gcq=mla_q_norm.reshape(depth, 1, MLA_Q_RANK), wuqn=wuqn, wabs=wabs, wuqr=wuqr,
        gckv=mla_kv_norm.reshape(depth, 1, MLA_KV_RANK), wuv=wuv,
        freq=jnp.asarray(freq), ggrp=group_norm.reshape(depth, 1, D_MIX),
        wout=w_out.astype(bf16),
        gpost=norm_post.reshape(depth, 1, D_MODEL))


def kernel(x, positions, norm_pre, w_in, attn_sinks, conv_w, conv_b, mla_q_norm, mla_w_uq,
           mla_kv_norm, mla_w_ukv, group_norm, w_out, norm_post):
    batch, seq, _ = x.shape
    depth = w_in.shape[0]
    assert seq % max(PROJ_ROWS, SWA_ROWS, MLA_TQ, SB_TQ, OUT_ROWS) == 0
    assert MLA_TQ == MLA_TK and SB_TQ == SB_TK and SB_TQ & (SB_TQ - 1) == 0
    n = batch * seq
    x2 = x.reshape(n, D_MODEL)
    pos2 = positions.astype(jnp.float32).reshape(n, 1)
    tri = jnp.asarray(np.tril(np.ones((SB_TK, SB_TK), np.float32), -1), jnp.bfloat16)
    pw = _prep_weights(norm_pre, w_in, conv_w, conv_b, mla_q_norm, mla_w_uq, mla_kv_norm,
                       mla_w_ukv, group_norm, w_out, norm_post)
    rope = _rope_call(pos2, pw["freq"])
    for l in range(depth):
        p = _proj_call(x2, rope, pw, l, seq)
        ya = _swa_call(attn_sinks, p["aq"], p["ak"], p["av"], l, seq)
        yc = _mla_call(p["qa"], p["kv"], pw, l, batch, seq)
        yd = _sb_call(p["dq"], p["dk"], p["dv"], tri, batch, seq)
        x2 = _out_call(x2, ya, p["yb"], yc, yd, pw, l)
    return x2.reshape(batch, seq, D_MODEL)
```

```python
import functools

import numpy as np
import jax
import jax.numpy as jnp
from jax import lax
from jax.experimental import pallas as pl
from jax.experimental.pallas import tpu as pltpu

D_MODEL = 1024
BLOCK = 128
NORM_EPS = 1e-6
SWA_HEADS = 4
SWA_HEAD_DIM = 64
CONV_WIDTH = 256
CONV_K = 3
MLA_HEADS = 4
MLA_Q_RANK = 256
MLA_KV_RANK = 128
MLA_NOPE_DIM = 64
MLA_ROPE_DIM = 32
MLA_V_DIM = 64
ROPE_THETA = 10000.0
SB_HEADS = 4
SB_HEAD_DIM = 64
GROUP_WIDTH = 256
N_GROUPS = 4
D_MIX = GROUP_WIDTH * N_GROUPS

LANES = 128
LAT_WIDTH = 256
ONES_LANE = 160
MLA_JUMP = 32.0
MLA_MIN_SUM = 2.0 ** -64
NEG_BIG = -1e30
SB_DEAD = -160.0

_IN_SIZES = (256, 128, 128, 256, 256, 256, 256, 128, 32, 256, 256, 256, 1024)
_IN_OFF = np.concatenate([[0], np.cumsum(_IN_SIZES)]).astype(int)
(_O_AQ, _O_AK, _O_AV, _O_BB, _O_BC, _O_BX, _O_CQ, _O_CKV, _O_CKR, _O_DQ, _O_DK, _O_DV, _O_GATE,
 _O_END) = [int(v) for v in _IN_OFF]

D_IN = _O_END
P_KR_END = _O_CKR + LANES
P_DQ, P_DK, P_DV, P_END = P_KR_END, P_KR_END + 256, P_KR_END + 512, P_KR_END + 768

ROPE_ROWS = 1024
PROJ_ROWS = 512
SWA_ROWS = 512
MLA_TQ = 512
MLA_TK = 512
SB_TQ = 256
SB_TK = 256
OUT_ROWS = 512
VMEM_LIMIT = 56 * 1024 * 1024


def _rms(v, g):
    return v * lax.rsqrt(jnp.mean(v * v, axis=-1, keepdims=True) + NORM_EPS) * g


def _dot(a, b):
    return jnp.dot(a, b, preferred_element_type=jnp.float32)


def _dot_nt(a, b):
    return lax.dot_general(a, b, (((1,), (1,)), ((), ())), preferred_element_type=jnp.float32)


def _layer_spec(a, l, single=False):
    zeros = (0,) * (a.ndim - 1)
    mode = dict(pipeline_mode=pl.Buffered(1)) if single else {}
    return pl.BlockSpec((None,) + a.shape[1:], lambda *_: (l,) + zeros, **mode)


def _rope_kernel(pos_ref, freq_ref, cos_ref, sin_ref):
    ang = pos_ref[...] * freq_ref[...]
    cos_ref[...] = jnp.cos(ang)
    sin_ref[...] = jnp.sin(ang)


def _rope_call(pos2, freq):
    n = pos2.shape[0]
    tm = ROPE_ROWS
    table = jax.ShapeDtypeStruct((n, LANES), jnp.float32)
    return pl.pallas_call(
        _rope_kernel,
        grid=(n // tm,),
        in_specs=[pl.BlockSpec((tm, 1), lambda i: (i, 0)),
                  pl.BlockSpec(freq.shape, lambda i: (0, 0))],
        out_specs=[pl.BlockSpec((tm, LANES), lambda i: (i, 0))] * 2,
        out_shape=[table, table],
        compiler_params=pltpu.CompilerParams(dimension_semantics=("arbitrary",)),
        name="rope",
    )(pos2, freq)


def _proj_kernel(x_ref, cos_ref, sin_ref, gpre_ref, win_ref, convw_ref, convb_ref, gcq_ref,
                 wuqn_ref, wabs_ref, wuqr_ref, gckv_ref,
                 aq_ref, ak_ref, av_ref, yb_ref, qa_ref, kv_ref, dq_ref, dk_ref, dv_ref,
                 wbf_ref, ubuf_ref, *, tiles_per_seq):
    tm = x_ref.shape[0]
    bf16 = jnp.bfloat16

    @pl.when(pl.program_id(0) == 0)
    def _():
        def put(dst, src, scale=None):
            blk = win_ref[src:src + 256, :].T
            blk = blk if scale is None else blk * scale
            wbf_ref[:, dst:dst + 256] = blk.astype(bf16)

        put(_O_AQ, _O_AQ, SWA_HEAD_DIM ** -0.5)
        for c in range(_O_AK, P_KR_END, 256):
            put(c, c)
        put(P_DQ, _O_DQ, SB_HEAD_DIM ** -0.5 * np.log2(np.e))
        put(P_DK, _O_DK)
        put(P_DV, _O_DV)

    xb = _rms(x_ref[...], gpre_ref[...]).astype(bf16)

    def proj(lo, hi):
        return _dot(xb, wbf_ref[:, lo:hi])

    aq_ref[...] = proj(_O_AQ, _O_AK).astype(bf16)
    akv = proj(_O_AK, _O_BB)
    for ref, blk in ((ak_ref, akv[:, 0:LANES]), (av_ref, akv[:, LANES:2 * LANES])):
        ref[:, 0:LANES] = blk.astype(bf16)
        ref[:, LANES:2 * LANES] = pltpu.roll(blk, SWA_HEAD_DIM, axis=1).astype(bf16)

    dq_ref[...] = proj(P_DQ, P_DK).astype(bf16)
    dk_ref[...] = proj(P_DK, P_DV).astype(bf16)
    dv_ref[...] = proj(P_DV, P_END).astype(bf16)

    ckr = proj(_O_CKV, P_KR_END)

    u = proj(_O_BC, _O_BX) * proj(_O_BX, _O_CQ)

    @pl.when(pl.program_id(0) % tiles_per_seq == 0)
    def _():
        ubuf_ref[0:8, :] = jnp.zeros((8, CONV_WIDTH), jnp.float32)

    ubuf_ref[8:tm + 8, :] = u
    u1 = ubuf_ref[7:tm + 7, :]
    u2 = ubuf_ref[6:tm + 6, :]
    conv = (convw_ref[0:1, :] * u2 + convw_ref[1:2, :] * u1 + convw_ref[2:3, :] * u
            + convb_ref[...])
    yb_ref[...] = proj(_O_BB, _O_BC) * conv
    ubuf_ref[0:8, :] = ubuf_ref[tm:tm + 8, :]

    cosv = cos_ref[...]
    sinv = sin_ref[...]
    lane = lax.broadcasted_iota(jnp.int32, (tm, LANES), 1)
    in_rope = lane < MLA_ROPE_DIM
    qscale = (MLA_NOPE_DIM + MLA_ROPE_DIM) ** -0.5 * np.log2(np.e)

    cq = _rms(proj(_O_CQ, _O_CKV), gcq_ref[...]).astype(bf16)
    qnope = _dot(cq, wuqn_ref[...]).astype(bf16)
    qlat = _dot(qnope, wabs_ref[...])
    qr = _dot(cq, wuqr_ref[...])
    qroped = (qr[:, 0:LANES] * cosv + qr[:, LANES:2 * LANES] * sinv) * qscale
    for h in range(MLA_HEADS):
        rows = slice(h * tm, (h + 1) * tm)
        qa_ref[rows, 0:LANES] = (qlat[:, h * LANES:(h + 1) * LANES] * qscale).astype(bf16)
        mine = qroped if h == 0 else pltpu.roll(qroped, LANES - h * MLA_ROPE_DIM, axis=1)
        qa_ref[rows, LANES:2 * LANES] = jnp.where(in_rope, mine, 0.0).astype(bf16)
    kv_ref[:, 0:LANES] = _rms(ckr[:, 0:LANES], gckv_ref[...]).astype(bf16)
    kr = ckr[:, LANES:2 * LANES]
    half = MLA_ROPE_DIM // 2
    partner = jnp.where(lane < half, -pltpu.roll(kr, LANES - half, axis=1),
                        pltpu.roll(kr, half, axis=1))
    krope = jnp.where(in_rope, kr * cosv + partner * sinv, 0.0)
    kv_ref[:, LANES:2 * LANES] = jnp.where(lane == ONES_LANE - LANES, 1.0, krope).astype(bf16)


def _proj_call(x2, rope, pw, l, seq):
    n = x2.shape[0]
    tm = PROJ_ROWS
    row = lambda w: pl.BlockSpec((tm, w), lambda i: (i, 0))
    bf16 = jnp.bfloat16
    outs = [("aq", 1, 256, bf16), ("ak", 1, 256, bf16), ("av", 1, 256, bf16),
            ("yb", 1, 256, jnp.float32), ("qa", MLA_HEADS, LAT_WIDTH, bf16),
            ("kv", 1, LAT_WIDTH, bf16), ("dq", 1, 256, bf16), ("dk", 1, 256, bf16),
            ("dv", 1, 256, bf16)]
    params = [pw[k] for k in ("gpre", "win", "convw", "convb", "gcq", "wuqn", "wabs", "wuqr",
                              "gckv")]
    res = pl.pallas_call(
        functools.partial(_proj_kernel, tiles_per_seq=seq // tm),
        grid=(n // tm,),
        in_specs=[row(D_MODEL), row(LANES), row(LANES)]
        + [_layer_spec(a, l, single=(k == "win")) for k, a in zip(
            ("gpre", "win", "convw", "convb", "gcq", "wuqn", "wabs", "wuqr", "gckv"), params)],
        out_specs=[pl.BlockSpec((r * tm, w), lambda i: (i, 0)) for _, r, w, _ in outs],
        out_shape=[jax.ShapeDtypeStruct((r * n, w), dt) for _, r, w, dt in outs],
        scratch_shapes=[pltpu.VMEM((D_MODEL, P_END), bf16),
                        pltpu.VMEM((tm + 8, CONV_WIDTH), jnp.float32)],
        compiler_params=pltpu.CompilerParams(dimension_semantics=("arbitrary",),
                                             vmem_limit_bytes=VMEM_LIMIT),
        name="proj",
    )(x2, *rope, *params)
    return {name: r for (name, _, _, _), r in zip(outs, res)}


def _swa_kernel(sink_ref, q_ref, k_ref, kp_ref, v_ref, vp_ref, o_ref, *, layer, tiles_per_seq):
    bf16 = jnp.bfloat16
    nsub = q_ref.shape[0] // BLOCK
    first_tile = (pl.program_id(0) % tiles_per_seq) == 0
    row = lax.broadcasted_iota(jnp.int32, (BLOCK, 2 * BLOCK), 0)
    col = lax.broadcasted_iota(jnp.int32, (BLOCK, 2 * BLOCK), 1)
    band = (col > row) & (col <= row + BLOCK)
    band_first = band & ((col >= BLOCK) | jnp.logical_not(first_tile))
    upper = lax.broadcasted_iota(jnp.int32, (BLOCK, LANES), 1) >= SWA_HEAD_DIM
    for j in range(nsub):
        rows = slice(j * BLOCK, (j + 1) * BLOCK)
        if j == 0:
            kk = jnp.concatenate([kp_ref[...], k_ref[0:BLOCK, :]], axis=0)
            vv = jnp.concatenate([vp_ref[...], v_ref[0:BLOCK, :]], axis=0)
            mask = band_first
        else:
            kk = k_ref[(j - 1) * BLOCK:(j + 1) * BLOCK, :]
            vv = v_ref[(j - 1) * BLOCK:(j + 1) * BLOCK, :]
            mask = band
        for lb in range(2):
            q2 = q_ref[rows, lb * LANES:(lb + 1) * LANES]
            halves = []
            for hh in range(2):
                sink = sink_ref[layer, 2 * lb + hh]
                qm = jnp.where(upper == (hh == 1), q2, jnp.zeros_like(q2))
                sel = slice(0, LANES) if lb == hh else slice(LANES, 2 * LANES)
                s = jnp.where(mask, _dot_nt(qm, kk[:, sel]), NEG_BIG)
                m = jnp.maximum(jnp.max(s, axis=1, keepdims=True), sink)
                p = jnp.exp(s - m)
                den = jnp.sum(p, axis=1, keepdims=True) + jnp.exp(sink - m)
                halves.append(_dot(p.astype(bf16), vv[:, sel]) / den)
            o_ref[rows, lb * LANES:(lb + 1) * LANES] = jnp.where(upper, halves[1], halves[0])


def _swa_call(sinks, aq, ak, av, l, seq):
    n = aq.shape[0]
    tm = SWA_ROWS
    per = tm // BLOCK
    cur = lambda w: pl.BlockSpec((tm, w), lambda i: (i, 0))
    prev = lambda w: pl.BlockSpec((BLOCK, w), lambda i: (jnp.maximum(i * per - 1, 0), 0))
    return pl.pallas_call(
        functools.partial(_swa_kernel, layer=l, tiles_per_seq=seq // tm),
        grid=(n // tm,),
        in_specs=[pl.BlockSpec(memory_space=pltpu.SMEM), cur(256), cur(256), prev(256), cur(256),
                  prev(256)],
        out_specs=cur(256),
        out_shape=jax.ShapeDtypeStruct((n, 256), jnp.float32),
        compiler_params=pltpu.CompilerParams(dimension_semantics=("arbitrary",)),
        name="swa",
    )(sinks, aq, ak, ak, av, av)


def _attn_kernel(q_ref, kv_ref, wuv_ref, dq_ref, dk_ref, dv_ref, tri_ref, o_ref, od_ref, m_ref,
                 acc_ref):
    tq, tk = MLA_TQ, MLA_TK
    bf16 = jnp.bfloat16
    rows = MLA_HEADS * tq
    i = pl.program_id(1)
    q = q_ref[...]

    def exact_update(off, masked):
        kv = kv_ref[pl.ds(off, tk), :]
        s = _dot_nt(q, kv)
        if masked:
            row = lax.broadcasted_iota(jnp.int32, (rows, tk), 0) & (tq - 1)
            col = lax.broadcasted_iota(jnp.int32, (rows, tk), 1)
            s = jnp.where(col <= row, s, NEG_BIG)
        m = m_ref[:, 0:1]
        m_new = jnp.maximum(m, jnp.max(s, axis=1, keepdims=True))
        p = jnp.exp2(s - m_new)
        acc_ref[...] = jnp.exp2(m - m_new) * acc_ref[...] + _dot(p.astype(bf16), kv)
        m_ref[...] = jnp.broadcast_to(m_new, (rows, LANES))

    def lazy_update(off, width):
        kv = kv_ref[pl.ds(off, width), :]
        d = _dot_nt(q, kv) - jnp.concatenate([m_ref[...]] * (width // LANES), axis=1)
        acc_new = acc_ref[...] + _dot(jnp.exp2(d).astype(bf16), kv)
        safe = jnp.max(d) <= MLA_JUMP

        @pl.when(safe)
        def _():
            acc_ref[...] = acc_new

        @pl.when(jnp.logical_not(safe))
        def _():
            def redo(j, carry):
                exact_update(pl.multiple_of(off + j * tk, tk), False)
                return carry
            lax.fori_loop(0, width // tk, redo, 0)

    off_d = pl.multiple_of(i * tk, tk)
    kv_d = kv_ref[pl.ds(off_d, tk), :]
    row = lax.broadcasted_iota(jnp.int32, (rows, tk), 0) & (tq - 1)
    col = lax.broadcasted_iota(jnp.int32, (rows, tk), 1)
    d0 = jnp.where(col <= row, _dot_nt(q, kv_d), NEG_BIG)
    acc0 = _dot(jnp.exp2(d0).astype(bf16), kv_d)
    sums = acc0[:, LANES:2 * LANES]
    ones_lane = lax.broadcasted_iota(jnp.int32, (rows, LANES), 1) == ONES_LANE - LANES
    smallest = jnp.min(jnp.where(ones_lane, sums, 1.0))
    fine = jnp.logical_and(jnp.max(d0) <= MLA_JUMP, smallest >= MLA_MIN_SUM)

    sb_tiles = [_sb_open(dq_ref, j, dk_ref, dv_ref, tri_ref, (tq // SB_TQ) * i + j)
                for j in range(tq // SB_TQ)]

    @pl.when(fine)
    def _():
        m_ref[...] = jnp.zeros((rows, LANES), jnp.float32)
        acc_ref[...] = acc0

    @pl.when(jnp.logical_not(fine))
    def _():
        m_ref[...] = jnp.full((rows, LANES), NEG_BIG, jnp.float32)
        acc_ref[...] = jnp.zeros((rows, LAT_WIDTH), jnp.float32)
        exact_update(off_d, True)

    for j, opened in enumerate(sb_tiles):
        _sb_close(opened, (tq // SB_TQ) * i + j, od_ref, j)

    def pair(c, carry):
        lazy_update(pl.multiple_of(c * 2 * tk, 2 * tk), 2 * tk)
        return carry

    lax.fori_loop(0, i // 2, pair, 0)

    @pl.when(i % 2 == 1)
    def _():
        lazy_update(pl.multiple_of((i - 1) * tk, tk), tk)

    acc = acc_ref[...]
    lat = (acc[:, 0:LANES] / acc[:, ONES_LANE:ONES_LANE + 1]).astype(bf16)
    y = None
    for h in range(MLA_HEADS):
        part = _dot(lat[h * tq:(h + 1) * tq, :], wuv_ref[h])
        y = part if y is None else y + part
    o_ref[...] = y


def _sb_open(q_ref, j, k_ref, v_ref, tri_ref, i):
    tq, tk = SB_TQ, SB_TK
    bf16 = jnp.bfloat16
    upper_q = lax.broadcasted_iota(jnp.int32, (tq, LANES), 1) >= SB_HEAD_DIM
    qs = []
    for lb in range(2):
        q2 = q_ref[j * tq:(j + 1) * tq, lb * LANES:(lb + 1) * LANES]
        zero = jnp.zeros_like(q2)
        qs.append(jnp.concatenate([jnp.where(upper_q, zero, q2), jnp.where(upper_q, q2, zero)],
                                  axis=0))
    row = lax.broadcasted_iota(jnp.int32, (2 * tq, tk), 0) & (tq - 1)
    col = lax.broadcasted_iota(jnp.int32, (2 * tq, tk), 1)
    strict = col < row
    tri = tri_ref[...]

    def step(c, state, masked):
        off = pl.multiple_of(c * tk, tk)
        new = []
        for lb in range(2):
            csum, acc = state[lb]
            k2 = k_ref[pl.ds(off, tk), lb * LANES:(lb + 1) * LANES]
            v2 = v_ref[pl.ds(off, tk), lb * LANES:(lb + 1) * LANES]
            z = _dot_nt(qs[lb], k2)
            if masked:
                z = jnp.where(strict, z, NEG_BIG)
            log_beta = jnp.minimum(z, 0.0) - jnp.log2(1.0 + jnp.exp2(-jnp.abs(z)))
            log_keep = log_beta - z
            hi = log_keep.astype(bf16)
            lo = (log_keep - hi.astype(jnp.float32)).astype(bf16)
            both = _dot(jnp.concatenate([hi, lo], axis=0), tri)
            within = both[0:2 * tq] + both[2 * tq:4 * tq]
            a = jnp.exp2(log_beta + (csum + within))
            acc = acc + _dot(a.astype(bf16), v2)
            csum = csum + jnp.sum(log_keep, axis=1, keepdims=True)
            new.append((csum, acc))
        return tuple(new)

    init = tuple((jnp.zeros((2 * tq, 1), jnp.float32), jnp.zeros((2 * tq, LANES), jnp.float32))
                 for _ in range(2))
    state = step(i, init, True)
    gone = jnp.where(i > 0, 0.0, NEG_BIG)
    state = step(jnp.maximum(i - 1, 0), tuple((csum + gone, acc) for csum, acc in state), False)
    return step, state


def _sb_close(opened, i, o_ref, j):
    tq = SB_TQ
    step, state = opened

    def least_decayed(st):
        return jnp.max(jnp.maximum(st[0][0], st[1][0]))

    def cond(carry):
        t, worst, _ = carry
        return jnp.logical_and(t < i - 1, worst > SB_DEAD)

    def body(carry):
        t, _, st = carry
        st = step(i - 2 - t, st, False)
        return t + 1, least_decayed(st), st

    _, _, state = lax.while_loop(cond, body, (jnp.int32(0), least_decayed(state), state))
    upper_q = lax.broadcasted_iota(jnp.int32, (tq, LANES), 1) >= SB_HEAD_DIM
    for lb in range(2):
        acc = state[lb][1]
        o_ref[j * tq:(j + 1) * tq, lb * LANES:(lb + 1) * LANES] = jnp.where(
            upper_q, acc[tq:2 * tq], acc[0:tq])


def _attn_call(qa, kv, dq, dk, dv, tri, pw, l, batch, seq):
    tq = MLA_TQ
    assert tq & (tq - 1) == 0 and PROJ_ROWS == tq and tq % SB_TQ == 0
    nq = seq // tq
    rows = MLA_HEADS * tq
    per_seq = lambda a, w: a.reshape(batch, seq, w)
    whole = lambda w: pl.BlockSpec((None, seq, w), lambda b, i: (b, 0, 0))
    tile = lambda w: pl.BlockSpec((None, tq, w), lambda b, i: (b, i, 0))
    out_mla, out_sb = pl.pallas_call(
        _attn_kernel,
        grid=(batch, nq),
        in_specs=[pl.BlockSpec((rows, LAT_WIDTH), lambda b, i: (b * nq + i, 0)),
                  whole(LAT_WIDTH), _layer_spec(pw["wuv"], l),
                  tile(256), whole(256), whole(256),
                  pl.BlockSpec((SB_TK, SB_TK), lambda b, i: (0, 0))],
        out_specs=[tile(256), tile(256)],
        out_shape=[jax.ShapeDtypeStruct((batch, seq, 256), jnp.float32)] * 2,
        scratch_shapes=[pltpu.VMEM((rows, LANES), jnp.float32),
                        pltpu.VMEM((rows, LAT_WIDTH), jnp.float32)],
        compiler_params=pltpu.CompilerParams(
            dimension_semantics=("arbitrary", "arbitrary"),
            vmem_limit_bytes=VMEM_LIMIT),
        name="attn",
    )(qa, per_seq(kv, LAT_WIDTH), pw["wuv"], per_seq(dq, 256), per_seq(dk, 256),
      per_seq(dv, 256), tri)
    return out_mla.reshape(batch * seq, 256), out_sb.reshape(batch * seq, 256)


def _out_kernel(x_ref, ya_ref, yb_ref, yc_ref, yd_ref, gpre_ref, win_ref, ggrp_ref, wout_ref,
                gpost_ref, o_ref, wgate_ref):
    bf16 = jnp.bfloat16

    @pl.when(pl.program_id(0) == 0)
    def _():
        for c in range(0, D_MIX, 256):
            wgate_ref[:, c:c + 256] = win_ref[_O_GATE + c:_O_GATE + c + 256, :].T.astype(bf16)

    x = x_ref[...]
    xb = _rms(x, gpre_ref[...]).astype(bf16)
    d = None
    for g, y_ref in enumerate((ya_ref, yb_ref, yc_ref, yd_ref)):
        sl = slice(g * GROUP_WIDTH, (g + 1) * GROUP_WIDTH)
        gate = _dot(xb, wgate_ref[:, sl])
        y = _rms(y_ref[...], ggrp_ref[:, sl]) * (gate * jax.nn.sigmoid(gate))
        part = _dot(y.astype(bf16), wout_ref[sl, :])
        d = part if d is None else d + part
    o_ref[...] = x + _rms(d, gpost_ref[...])


def _out_call(x2, ya, yb, yc, yd, pw, l):
    n = x2.shape[0]
    tm = OUT_ROWS
    row = lambda w: pl.BlockSpec((tm, w), lambda i: (i, 0))
    names = ("gpre", "win", "ggrp", "wout", "gpost")
    params = [pw[k] for k in names]
    return pl.pallas_call(
        _out_kernel,
        grid=(n // tm,),
        in_specs=[row(D_MODEL), row(256), row(256), row(256), row(256)]
        + [_layer_spec(a, l, single=(k == "win")) for k, a in zip(names, params)],
        out_specs=row(D_MODEL),
        out_shape=jax.ShapeDtypeStruct((n, D_MODEL), jnp.float32),
        scratch_shapes=[pltpu.VMEM((D_MODEL, D_MIX), jnp.bfloat16)],
        compiler_params=pltpu.CompilerParams(dimension_semantics=("arbitrary",),
                                             vmem_limit_bytes=VMEM_LIMIT),
        name="out",
    )(x2, ya, yb, yc, yd, *params)


def _rope_swap(w):
    half = MLA_ROPE_DIM // 2
    return jnp.concatenate([-w[..., half:], w[..., :half]], axis=-1)


def _prep_weights(norm_pre, w_in, conv_w, conv_b, mla_q_norm, mla_w_uq, mla_kv_norm, mla_w_ukv,
                  group_norm, w_out, norm_post):
    f32, bf16 = jnp.float32, jnp.bfloat16
    depth = w_in.shape[0]
    dqk = MLA_NOPE_DIM + MLA_ROPE_DIM
    uq = mla_w_uq.reshape(depth, MLA_Q_RANK, MLA_HEADS, dqk)
    wuqn = uq[..., :MLA_NOPE_DIM].reshape(depth, MLA_Q_RANK, -1).astype(bf16)
    wuqr = jnp.concatenate(
        [uq[..., MLA_NOPE_DIM:].reshape(depth, MLA_Q_RANK, -1),
         _rope_swap(uq[..., MLA_NOPE_DIM:]).reshape(depth, MLA_Q_RANK, -1)], axis=-1).astype(bf16)
    ukv = mla_w_ukv.reshape(depth, MLA_KV_RANK, MLA_HEADS, MLA_NOPE_DIM + MLA_V_DIM)
    own_head = jnp.eye(MLA_HEADS, dtype=f32)[None, :, None, :, None]
    uk_t = jnp.transpose(ukv[..., :MLA_NOPE_DIM], (0, 2, 3, 1))
    wabs = (uk_t[:, :, :, None, :] * own_head).reshape(
        depth, MLA_HEADS * MLA_NOPE_DIM, MLA_HEADS * MLA_KV_RANK).astype(bf16)
    uv = jnp.transpose(ukv[..., MLA_NOPE_DIM:], (0, 2, 1, 3))
    wuv = (uv[:, :, :, None, :] * own_head).reshape(
        depth, MLA_HEADS, MLA_KV_RANK, MLA_HEADS * MLA_V_DIM).astype(bf16)

    half = MLA_ROPE_DIM // 2
    freqs = (ROPE_THETA ** (-np.arange(half, dtype=np.float32) / half)).astype(np.float32)
    freq = np.tile(np.concatenate([freqs, freqs]), LANES // MLA_ROPE_DIM).reshape(1, LANES)

    return dict(
        gpre=norm_pre.reshape(depth, 1, D_MODEL), win=jnp.swapaxes(w_in, 1, 2),
        convw=conv_w, convb=conv_b.reshape(depth, 1, CONV_WIDTH),
        gcq=mla_q_norm.reshape(depth, 1, MLA_Q_RANK), wuqn=wuqn, wabs=wabs, wuqr=wuqr,
        gckv=mla_kv_norm.reshape(depth, 1, MLA_KV_RANK), wuv=wuv,
        freq=jnp.asarray(freq), ggrp=group_norm.reshape(depth, 1, D_MIX),
        wout=w_out.astype(bf16),
        gpost=norm_post.reshape(depth, 1, D_MODEL))


def kernel(x, positions, norm_pre, w_in, attn_sinks, conv_w, conv_b, mla_q_norm, mla_w_uq,
           mla_kv_norm, mla_w_ukv, group_norm, w_out, norm_post):
    batch, seq, _ = x.shape
    depth = w_in.shape[0]
    assert seq % max(PROJ_ROWS, SWA_ROWS, MLA_TQ, SB_TQ, OUT_ROWS) == 0
    assert MLA_TQ == MLA_TK and SB_TQ == SB_TK and SB_TQ & (SB_TQ - 1) == 0
    n = batch * seq
    x2 = x.reshape(n, D_MODEL)
    pos2 = positions.astype(jnp.float32).reshape(n, 1)
    tri = jnp.asarray(np.tril(np.ones((SB_TK, SB_TK), np.float32), -1), jnp.bfloat16)
    pw = _prep_weights(norm_pre, w_in, conv_w, conv_b, mla_q_norm, mla_w_uq, mla_kv_norm,
                       mla_w_ukv, group_norm, w_out, norm_post)
    rope = _rope_call(pos2, pw["freq"])
    for l in range(depth):
        p = _proj_call(x2, rope, pw, l, seq)
        ya = _swa_call(attn_sinks, p["aq"], p["ak"], p["av"], l, seq)
        yc, yd = _attn_call(p["qa"], p["kv"], p["dq"], p["dk"], p["dv"], tri, pw, l, batch, seq)
        x2 = _out_call(x2, ya, p["yb"], yc, yd, pw, l)
    return x2.reshape(batch, seq, D_MODEL)
```

```python
import functools

import numpy as np
import jax
import jax.numpy as jnp
from jax import lax
from jax.experimental import pallas as pl
from jax.experimental.pallas import tpu as pltpu

D_MODEL = 1024
BLOCK = 128
NORM_EPS = 1e-6
SWA_HEADS = 4
SWA_HEAD_DIM = 64
CONV_WIDTH = 256
CONV_K = 3
MLA_HEADS = 4
MLA_Q_RANK = 256
MLA_KV_RANK = 128
MLA_NOPE_DIM = 64
MLA_ROPE_DIM = 32
MLA_V_DIM = 64
ROPE_THETA = 10000.0
SB_HEADS = 4
SB_HEAD_DIM = 64
GROUP_WIDTH = 256
N_GROUPS = 4
D_MIX = GROUP_WIDTH * N_GROUPS

LANES = 128
LAT_WIDTH = 256
ONES_LANE = 160
MLA_JUMP = 32.0
MLA_MIN_SUM = 2.0 ** -64
NEG_BIG = -1e30
SB_DEAD = -160.0

_IN_SIZES = (256, 128, 128, 256, 256, 256, 256, 128, 32, 256, 256, 256, 1024)
_IN_OFF = np.concatenate([[0], np.cumsum(_IN_SIZES)]).astype(int)
(_O_AQ, _O_AK, _O_AV, _O_BB, _O_BC, _O_BX, _O_CQ, _O_CKV, _O_CKR, _O_DQ, _O_DK, _O_DV, _O_GATE,
 _O_END) = [int(v) for v in _IN_OFF]

D_IN = _O_END
P_KR_END = _O_CKR + LANES
P_DQ, P_DK, P_DV, P_END = P_KR_END, P_KR_END + 256, P_KR_END + 512, P_KR_END + 768

ROPE_ROWS = 1024
PROJ_ROWS = 512
MLA_TQ = 512
MLA_TK = 512
SB_TQ = 256
SB_TK = 256
OUT_ROWS = 512
VMEM_LIMIT = 56 * 1024 * 1024


def _rms(v, g):
    return v * lax.rsqrt(jnp.mean(v * v, axis=-1, keepdims=True) + NORM_EPS) * g


def _dot(a, b):
    return jnp.dot(a, b, preferred_element_type=jnp.float32)


def _dot_nt(a, b):
    return lax.dot_general(a, b, (((1,), (1,)), ((), ())), preferred_element_type=jnp.float32)


def _layer_spec(a, l, single=False):
    zeros = (0,) * (a.ndim - 1)
    mode = dict(pipeline_mode=pl.Buffered(1)) if single else {}
    return pl.BlockSpec((None,) + a.shape[1:], lambda *_: (l,) + zeros, **mode)


def _rope_kernel(pos_ref, freq_ref, cos_ref, sin_ref):
    ang = pos_ref[...] * freq_ref[...]
    cos_ref[...] = jnp.cos(ang)
    sin_ref[...] = jnp.sin(ang)


def _rope_call(pos2, freq):
    n = pos2.shape[0]
    tm = ROPE_ROWS
    table = jax.ShapeDtypeStruct((n, LANES), jnp.float32)
    return pl.pallas_call(
        _rope_kernel,
        grid=(n // tm,),
        in_specs=[pl.BlockSpec((tm, 1), lambda i: (i, 0)),
                  pl.BlockSpec(freq.shape, lambda i: (0, 0))],
        out_specs=[pl.BlockSpec((tm, LANES), lambda i: (i, 0))] * 2,
        out_shape=[table, table],
        compiler_params=pltpu.CompilerParams(dimension_semantics=("arbitrary",)),
        name="rope",
    )(pos2, freq)


def _proj_kernel(x_ref, cos_ref, sin_ref, gpre_ref, win_ref, convw_ref, convb_ref, gcq_ref,
                 wuqn_ref, wabs_ref, wuqr_ref, gckv_ref,
                 aq_ref, ak_ref, av_ref, yb_ref, qa_ref, kv_ref, dq_ref, dk_ref, dv_ref,
                 wbf_ref, ubuf_ref, *, tiles_per_seq):
    tm = x_ref.shape[0]
    bf16 = jnp.bfloat16

    @pl.when(pl.program_id(0) == 0)
    def _():
        def put(dst, src, scale=None):
            blk = win_ref[src:src + 256, :].T
            blk = blk if scale is None else blk * scale
            wbf_ref[:, dst:dst + 256] = blk.astype(bf16)

        put(_O_AQ, _O_AQ, SWA_HEAD_DIM ** -0.5)
        for c in range(_O_AK, P_KR_END, 256):
            put(c, c)
        put(P_DQ, _O_DQ, SB_HEAD_DIM ** -0.5 * np.log2(np.e))
        put(P_DK, _O_DK)
        put(P_DV, _O_DV)

    xb = _rms(x_ref[...], gpre_ref[...]).astype(bf16)

    def proj(lo, hi):
        return _dot(xb, wbf_ref[:, lo:hi])

    aq_ref[...] = proj(_O_AQ, _O_AK).astype(bf16)
    akv = proj(_O_AK, _O_BB)
    for ref, blk in ((ak_ref, akv[:, 0:LANES]), (av_ref, akv[:, LANES:2 * LANES])):
        ref[:, 0:LANES] = blk.astype(bf16)
        ref[:, LANES:2 * LANES] = pltpu.roll(blk, SWA_HEAD_DIM, axis=1).astype(bf16)

    dq_ref[...] = proj(P_DQ, P_DK).astype(bf16)
    dk_ref[...] = proj(P_DK, P_DV).astype(bf16)
    dv_ref[...] = proj(P_DV, P_END).astype(bf16)

    ckr = proj(_O_CKV, P_KR_END)

    u = proj(_O_BC, _O_BX) * proj(_O_BX, _O_CQ)

    @pl.when(pl.program_id(0) % tiles_per_seq == 0)
    def _():
        ubuf_ref[0:8, :] = jnp.zeros((8, CONV_WIDTH), jnp.float32)

    ubuf_ref[8:tm + 8, :] = u
    u1 = ubuf_ref[7:tm + 7, :]
    u2 = ubuf_ref[6:tm + 6, :]
    conv = (convw_ref[0:1, :] * u2 + convw_ref[1:2, :] * u1 + convw_ref[2:3, :] * u
            + convb_ref[...])
    yb_ref[...] = proj(_O_BB, _O_BC) * conv
    ubuf_ref[0:8, :] = ubuf_ref[tm:tm + 8, :]

    cosv = cos_ref[...]
    sinv = sin_ref[...]
    lane = lax.broadcasted_iota(jnp.int32, (tm, LANES), 1)
    in_rope = lane < MLA_ROPE_DIM
    qscale = (MLA_NOPE_DIM + MLA_ROPE_DIM) ** -0.5 * np.log2(np.e)

    cq = _rms(proj(_O_CQ, _O_CKV), gcq_ref[...]).astype(bf16)
    qnope = _dot(cq, wuqn_ref[...]).astype(bf16)
    qlat = _dot(qnope, wabs_ref[...])
    qr = _dot(cq, wuqr_ref[...])
    qroped = (qr[:, 0:LANES] * cosv + qr[:, LANES:2 * LANES] * sinv) * qscale
    for h in range(MLA_HEADS):
        rows = slice(h * tm, (h + 1) * tm)
        qa_ref[rows, 0:LANES] = (qlat[:, h * LANES:(h + 1) * LANES] * qscale).astype(bf16)
        mine = qroped if h == 0 else pltpu.roll(qroped, LANES - h * MLA_ROPE_DIM, axis=1)
        qa_ref[rows, LANES:2 * LANES] = jnp.where(in_rope, mine, 0.0).astype(bf16)
    kv_ref[:, 0:LANES] = _rms(ckr[:, 0:LANES], gckv_ref[...]).astype(bf16)
    kr = ckr[:, LANES:2 * LANES]
    half = MLA_ROPE_DIM // 2
    partner = jnp.where(lane < half, -pltpu.roll(kr, LANES - half, axis=1),
                        pltpu.roll(kr, half, axis=1))
    krope = jnp.where(in_rope, kr * cosv + partner * sinv, 0.0)
    kv_ref[:, LANES:2 * LANES] = jnp.where(lane == ONES_LANE - LANES, 1.0, krope).astype(bf16)


def _proj_call(x2, rope, pw, l, seq):
    n = x2.shape[0]
    tm = PROJ_ROWS
    row = lambda w: pl.BlockSpec((tm, w), lambda i: (i, 0))
    bf16 = jnp.bfloat16
    outs = [("aq", 1, 256, bf16), ("ak", 1, 256, bf16), ("av", 1, 256, bf16),
            ("yb", 1, 256, jnp.float32), ("qa", MLA_HEADS, LAT_WIDTH, bf16),
            ("kv", 1, LAT_WIDTH, bf16), ("dq", 1, 256, bf16), ("dk", 1, 256, bf16),
            ("dv", 1, 256, bf16)]
    params = [pw[k] for k in ("gpre", "win", "convw", "convb", "gcq", "wuqn", "wabs", "wuqr",
                              "gckv")]
    res = pl.pallas_call(
        functools.partial(_proj_kernel, tiles_per_seq=seq // tm),
        grid=(n // tm,),
        in_specs=[row(D_MODEL), row(LANES), row(LANES)]
        + [_layer_spec(a, l, single=(k == "win")) for k, a in zip(
            ("gpre", "win", "convw", "convb", "gcq", "wuqn", "wabs", "wuqr", "gckv"), params)],
        out_specs=[pl.BlockSpec((r * tm, w), lambda i: (i, 0)) for _, r, w, _ in outs],
        out_shape=[jax.ShapeDtypeStruct((r * n, w), dt) for _, r, w, dt in outs],
        scratch_shapes=[pltpu.VMEM((D_MODEL, P_END), bf16),
                        pltpu.VMEM((tm + 8, CONV_WIDTH), jnp.float32)],
        compiler_params=pltpu.CompilerParams(dimension_semantics=("arbitrary",),
                                             vmem_limit_bytes=VMEM_LIMIT),
        name="proj",
    )(x2, *rope, *params)
    return {name: r for (name, _, _, _), r in zip(outs, res)}


def _swa_tile(sink_ref, q_ref, k_ref, kp_ref, v_ref, vp_ref, layer, first_tile):
    bf16 = jnp.bfloat16
    nsub = q_ref.shape[0] // BLOCK
    out = []
    row = lax.broadcasted_iota(jnp.int32, (BLOCK, 2 * BLOCK), 0)
    col = lax.broadcasted_iota(jnp.int32, (BLOCK, 2 * BLOCK), 1)
    band = (col > row) & (col <= row + BLOCK)
    band_first = band & ((col >= BLOCK) | jnp.logical_not(first_tile))
    upper = lax.broadcasted_iota(jnp.int32, (BLOCK, LANES), 1) >= SWA_HEAD_DIM
    for j in range(nsub):
        rows = slice(j * BLOCK, (j + 1) * BLOCK)
        if j == 0:
            kk = jnp.concatenate([kp_ref[...], k_ref[0:BLOCK, :]], axis=0)
            vv = jnp.concatenate([vp_ref[...], v_ref[0:BLOCK, :]], axis=0)
            mask = band_first
        else:
            kk = k_ref[(j - 1) * BLOCK:(j + 1) * BLOCK, :]
            vv = v_ref[(j - 1) * BLOCK:(j + 1) * BLOCK, :]
            mask = band
        blocks = []
        for lb in range(2):
            q2 = q_ref[rows, lb * LANES:(lb + 1) * LANES]
            halves = []
            for hh in range(2):
                sink = sink_ref[layer, 2 * lb + hh]
                qm = jnp.where(upper == (hh == 1), q2, jnp.zeros_like(q2))
                sel = slice(0, LANES) if lb == hh else slice(LANES, 2 * LANES)
                s = jnp.where(mask, _dot_nt(qm, kk[:, sel]), NEG_BIG)
                m = jnp.maximum(jnp.max(s, axis=1, keepdims=True), sink)
                p = jnp.exp(s - m)
                den = jnp.sum(p, axis=1, keepdims=True) + jnp.exp(sink - m)
                halves.append(_dot(p.astype(bf16), vv[:, sel]) / den)
            blocks.append(jnp.where(upper, halves[1], halves[0]))
        out.append(jnp.concatenate(blocks, axis=1))
    return jnp.concatenate(out, axis=0)


def _attn_kernel(q_ref, kv_ref, wuv_ref, dq_ref, dk_ref, dv_ref, tri_ref, o_ref, od_ref, m_ref,
                 acc_ref):
    tq, tk = MLA_TQ, MLA_TK
    bf16 = jnp.bfloat16
    rows = MLA_HEADS * tq
    i = pl.program_id(1)
    q = q_ref[...]

    def exact_update(off, masked):
        kv = kv_ref[pl.ds(off, tk), :]
        s = _dot_nt(q, kv)
        if masked:
            row = lax.broadcasted_iota(jnp.int32, (rows, tk), 0) & (tq - 1)
            col = lax.broadcasted_iota(jnp.int32, (rows, tk), 1)
            s = jnp.where(col <= row, s, NEG_BIG)
        m = m_ref[:, 0:1]
        m_new = jnp.maximum(m, jnp.max(s, axis=1, keepdims=True))
        p = jnp.exp2(s - m_new)
        acc_ref[...] = jnp.exp2(m - m_new) * acc_ref[...] + _dot(p.astype(bf16), kv)
        m_ref[...] = jnp.broadcast_to(m_new, (rows, LANES))

    def lazy_update(off, width):
        kv = kv_ref[pl.ds(off, width), :]
        d = _dot_nt(q, kv) - jnp.concatenate([m_ref[...]] * (width // LANES), axis=1)
        acc_new = acc_ref[...] + _dot(jnp.exp2(d).astype(bf16), kv)
        safe = jnp.max(d) <= MLA_JUMP

        @pl.when(safe)
        def _():
            acc_ref[...] = acc_new

        @pl.when(jnp.logical_not(safe))
        def _():
            def redo(j, carry):
                exact_update(pl.multiple_of(off + j * tk, tk), False)
                return carry
            lax.fori_loop(0, width // tk, redo, 0)

    off_d = pl.multiple_of(i * tk, tk)
    kv_d = kv_ref[pl.ds(off_d, tk), :]
    row = lax.broadcasted_iota(jnp.int32, (rows, tk), 0) & (tq - 1)
    col = lax.broadcasted_iota(jnp.int32, (rows, tk), 1)
    d0 = jnp.where(col <= row, _dot_nt(q, kv_d), NEG_BIG)
    acc0 = _dot(jnp.exp2(d0).astype(bf16), kv_d)
    sums = acc0[:, LANES:2 * LANES]
    ones_lane = lax.broadcasted_iota(jnp.int32, (rows, LANES), 1) == ONES_LANE - LANES
    smallest = jnp.min(jnp.where(ones_lane, sums, 1.0))
    fine = jnp.logical_and(jnp.max(d0) <= MLA_JUMP, smallest >= MLA_MIN_SUM)

    sb_tiles = [_sb_open(dq_ref, j, dk_ref, dv_ref, tri_ref, (tq // SB_TQ) * i + j)
                for j in range(tq // SB_TQ)]

    @pl.when(fine)
    def _():
        m_ref[...] = jnp.zeros((rows, LANES), jnp.float32)
        acc_ref[...] = acc0

    @pl.when(jnp.logical_not(fine))
    def _():
        m_ref[...] = jnp.full((rows, LANES), NEG_BIG, jnp.float32)
        acc_ref[...] = jnp.zeros((rows, LAT_WIDTH), jnp.float32)
        exact_update(off_d, True)

    for j, opened in enumerate(sb_tiles):
        _sb_close(opened, (tq // SB_TQ) * i + j, od_ref, j)

    def pair(c, carry):
        lazy_update(pl.multiple_of(c * 2 * tk, 2 * tk), 2 * tk)
        return carry

    lax.fori_loop(0, i // 2, pair, 0)

    @pl.when(i % 2 == 1)
    def _():
        lazy_update(pl.multiple_of((i - 1) * tk, tk), tk)

    acc = acc_ref[...]
    lat = (acc[:, 0:LANES] / acc[:, ONES_LANE:ONES_LANE + 1]).astype(bf16)
    y = None
    for h in range(MLA_HEADS):
        part = _dot(lat[h * tq:(h + 1) * tq, :], wuv_ref[h])
        y = part if y is None else y + part
    o_ref[...] = y


def _sb_open(q_ref, j, k_ref, v_ref, tri_ref, i):
    tq, tk = SB_TQ, SB_TK
    bf16 = jnp.bfloat16
    upper_q = lax.broadcasted_iota(jnp.int32, (tq, LANES), 1) >= SB_HEAD_DIM
    qs = []
    for lb in range(2):
        q2 = q_ref[j * tq:(j + 1) * tq, lb * LANES:(lb + 1) * LANES]
        zero = jnp.zeros_like(q2)
        qs.append(jnp.concatenate([jnp.where(upper_q, zero, q2), jnp.where(upper_q, q2, zero)],
                                  axis=0))
    row = lax.broadcasted_iota(jnp.int32, (2 * tq, tk), 0) & (tq - 1)
    col = lax.broadcasted_iota(jnp.int32, (2 * tq, tk), 1)
    strict = col < row
    tri = tri_ref[...]

    def step(c, state, masked):
        off = pl.multiple_of(c * tk, tk)
        new = []
        for lb in range(2):
            csum, acc = state[lb]
            k2 = k_ref[pl.ds(off, tk), lb * LANES:(lb + 1) * LANES]
            v2 = v_ref[pl.ds(off, tk), lb * LANES:(lb + 1) * LANES]
            z = _dot_nt(qs[lb], k2)
            if masked:
                z = jnp.where(strict, z, NEG_BIG)
            log_beta = jnp.minimum(z, 0.0) - jnp.log2(1.0 + jnp.exp2(-jnp.abs(z)))
            log_keep = log_beta - z
            hi = log_keep.astype(bf16)
            lo = (log_keep - hi.astype(jnp.float32)).astype(bf16)
            both = _dot(jnp.concatenate([hi, lo], axis=0), tri)
            within = both[0:2 * tq] + both[2 * tq:4 * tq]
            a = jnp.exp2(log_beta + (csum + within))
            acc = acc + _dot(a.astype(bf16), v2)
            csum = csum + jnp.sum(log_keep, axis=1, keepdims=True)
            new.append((csum, acc))
        return tuple(new)

    init = tuple((jnp.zeros((2 * tq, 1), jnp.float32), jnp.zeros((2 * tq, LANES), jnp.float32))
                 for _ in range(2))
    state = step(i, init, True)
    gone = jnp.where(i > 0, 0.0, NEG_BIG)
    state = step(jnp.maximum(i - 1, 0), tuple((csum + gone, acc) for csum, acc in state), False)
    return step, state


def _sb_close(opened, i, o_ref, j):
    tq = SB_TQ
    step, state = opened

    def least_decayed(st):
        return jnp.max(jnp.maximum(st[0][0], st[1][0]))

    def cond(carry):
        t, worst, _ = carry
        return jnp.logical_and(t < i - 1, worst > SB_DEAD)

    def body(carry):
        t, _, st = carry
        st = step(i - 2 - t, st, False)
        return t + 1, least_decayed(st), st

    _, _, state = lax.while_loop(cond, body, (jnp.int32(0), least_decayed(state), state))
    upper_q = lax.broadcasted_iota(jnp.int32, (tq, LANES), 1) >= SB_HEAD_DIM
    for lb in range(2):
        acc = state[lb][1]
        o_ref[j * tq:(j + 1) * tq, lb * LANES:(lb + 1) * LANES] = jnp.where(
            upper_q, acc[tq:2 * tq], acc[0:tq])


def _attn_call(qa, kv, dq, dk, dv, tri, pw, l, batch, seq):
    tq = MLA_TQ
    assert tq & (tq - 1) == 0 and PROJ_ROWS == tq and tq % SB_TQ == 0
    nq = seq // tq
    rows = MLA_HEADS * tq
    per_seq = lambda a, w: a.reshape(batch, seq, w)
    whole = lambda w: pl.BlockSpec((None, seq, w), lambda b, i: (b, 0, 0))
    tile = lambda w: pl.BlockSpec((None, tq, w), lambda b, i: (b, i, 0))
    out_mla, out_sb = pl.pallas_call(
        _attn_kernel,
        grid=(batch, nq),
        in_specs=[pl.BlockSpec((rows, LAT_WIDTH), lambda b, i: (b * nq + i, 0)),
                  whole(LAT_WIDTH), _layer_spec(pw["wuv"], l),
                  tile(256), whole(256), whole(256),
                  pl.BlockSpec((SB_TK, SB_TK), lambda b, i: (0, 0))],
        out_specs=[tile(256), tile(256)],
        out_shape=[jax.ShapeDtypeStruct((batch, seq, 256), jnp.float32)] * 2,
        scratch_shapes=[pltpu.VMEM((rows, LANES), jnp.float32),
                        pltpu.VMEM((rows, LAT_WIDTH), jnp.float32)],
        compiler_params=pltpu.CompilerParams(
            dimension_semantics=("arbitrary", "arbitrary"),
            vmem_limit_bytes=VMEM_LIMIT),
        name="attn",
    )(qa, per_seq(kv, LAT_WIDTH), pw["wuv"], per_seq(dq, 256), per_seq(dk, 256),
      per_seq(dv, 256), tri)
    return out_mla.reshape(batch * seq, 256), out_sb.reshape(batch * seq, 256)


def _out_kernel(sink_ref, x_ref, aq_ref, ak_ref, akp_ref, av_ref, avp_ref, yb_ref, yc_ref, yd_ref,
                gpre_ref, win_ref, ggrp_ref, wout_ref, gpost_ref, o_ref, wgate_ref, *, layer,
                tiles_per_seq):
    bf16 = jnp.bfloat16

    @pl.when(pl.program_id(0) == 0)
    def _():
        for c in range(0, D_MIX, 256):
            wgate_ref[:, c:c + 256] = win_ref[_O_GATE + c:_O_GATE + c + 256, :].T.astype(bf16)

    first_tile = (pl.program_id(0) % tiles_per_seq) == 0
    ya = _swa_tile(sink_ref, aq_ref, ak_ref, akp_ref, av_ref, avp_ref, layer, first_tile)
    x = x_ref[...]
    xb = _rms(x, gpre_ref[...]).astype(bf16)
    d = None
    for g, y_in in enumerate((ya, yb_ref, yc_ref, yd_ref)):
        sl = slice(g * GROUP_WIDTH, (g + 1) * GROUP_WIDTH)
        gate = _dot(xb, wgate_ref[:, sl])
        y = _rms(y_in[...], ggrp_ref[:, sl]) * (gate * jax.nn.sigmoid(gate))
        part = _dot(y.astype(bf16), wout_ref[sl, :])
        d = part if d is None else d + part
    o_ref[...] = x + _rms(d, gpost_ref[...])


def _out_call(x2, sinks, aq, ak, av, yb, yc, yd, pw, l, seq):
    n = x2.shape[0]
    tm = OUT_ROWS
    per = tm // BLOCK
    row = lambda w: pl.BlockSpec((tm, w), lambda i: (i, 0))
    prev = lambda w: pl.BlockSpec((BLOCK, w), lambda i: (jnp.maximum(i * per - 1, 0), 0))
    names = ("gpre", "win", "ggrp", "wout", "gpost")
    params = [pw[k] for k in names]
    return pl.pallas_call(
        functools.partial(_out_kernel, layer=l, tiles_per_seq=seq // tm),
        grid=(n // tm,),
        in_specs=[pl.BlockSpec(memory_space=pltpu.SMEM), row(D_MODEL), row(256), row(256),
                  prev(256), row(256), prev(256), row(256), row(256), row(256)]
        + [_layer_spec(a, l, single=(k == "win")) for k, a in zip(names, params)],
        out_specs=row(D_MODEL),
        out_shape=jax.ShapeDtypeStruct((n, D_MODEL), jnp.float32),
        scratch_shapes=[pltpu.VMEM((D_MODEL, D_MIX), jnp.bfloat16)],
        compiler_params=pltpu.CompilerParams(dimension_semantics=("arbitrary",),
                                             vmem_limit_bytes=VMEM_LIMIT),
        name="out",
    )(sinks, x2, aq, ak, ak, av, av, yb, yc, yd, *params)


def _rope_swap(w):
    half = MLA_ROPE_DIM // 2
    return jnp.concatenate([-w[..., half:], w[..., :half]], axis=-1)


def _prep_weights(norm_pre, w_in, conv_w, conv_b, mla_q_norm, mla_w_uq, mla_kv_norm, mla_w_ukv,
                  group_norm, w_out, norm_post):
    f32, bf16 = jnp.float32, jnp.bfloat16
    depth = w_in.shape[0]
    dqk = MLA_NOPE_DIM + MLA_ROPE_DIM
    uq = mla_w_uq.reshape(depth, MLA_Q_RANK, MLA_HEADS, dqk)
    wuqn = uq[..., :MLA_NOPE_DIM].reshape(depth, MLA_Q_RANK, -1).astype(bf16)
    wuqr = jnp.concatenate(
        [uq[..., MLA_NOPE_DIM:].reshape(depth, MLA_Q_RANK, -1),
         _rope_swap(uq[..., MLA_NOPE_DIM:]).reshape(depth, MLA_Q_RANK, -1)], axis=-1).astype(bf16)
    ukv = mla_w_ukv.reshape(depth, MLA_KV_RANK, MLA_HEADS, MLA_NOPE_DIM + MLA_V_DIM)
    own_head = jnp.eye(MLA_HEADS, dtype=f32)[None, :, None, :, None]
    uk_t = jnp.transpose(ukv[..., :MLA_NOPE_DIM], (0, 2, 3, 1))
    wabs = (uk_t[:, :, :, None, :] * own_head).reshape(
        depth, MLA_HEADS * MLA_NOPE_DIM, MLA_HEADS * MLA_KV_RANK).astype(bf16)
    uv = jnp.transpose(ukv[..., MLA_NOPE_DIM:], (0, 2, 1, 3))
    wuv = (uv[:, :, :, None, :] * own_head).reshape(
        depth, MLA_HEADS, MLA_KV_RANK, MLA_HEADS * MLA_V_DIM).astype(bf16)

    half = MLA_ROPE_DIM // 2
    freqs = (ROPE_THETA ** (-np.arange(half, dtype=np.float32) / half)).astype(np.float32)
    freq = np.tile(np.concatenate([freqs, freqs]), LANES // MLA_ROPE_DIM).reshape(1, LANES)

    return dict(
        gpre=norm_pre.reshape(depth, 1, D_MODEL), win=jnp.swapaxes(w_in, 1, 2),
        convw=conv_w, convb=conv_b.reshape(depth, 1, CONV_WIDTH),
        gcq=mla_q_norm.reshape(depth, 1, MLA_Q_RANK), wuqn=wuqn, wabs=wabs, wuqr=wuqr,
        gckv=mla_kv_norm.reshape(depth, 1, MLA_KV_RANK), wuv=wuv,
        freq=jnp.asarray(freq), ggrp=group_norm.reshape(depth, 1, D_MIX),
        wout=w_out.astype(bf16),
        gpost=norm_post.reshape(depth, 1, D_MODEL))


def kernel(x, positions, norm_pre, w_in, attn_sinks, conv_w, conv_b, mla_q_norm, mla_w_uq,
           mla_kv_norm, mla_w_ukv, group_norm, w_out, norm_post):
    batch, seq, _ = x.shape
    depth = w_in.shape[0]
    assert seq % max(PROJ_ROWS, MLA_TQ, SB_TQ, OUT_ROWS) == 0 and OUT_ROWS % BLOCK == 0
    assert MLA_TQ == MLA_TK and SB_TQ == SB_TK and SB_TQ & (SB_TQ - 1) == 0
    n = batch * seq
    x2 = x.reshape(n, D_MODEL)
    pos2 = positions.astype(jnp.float32).reshape(n, 1)
    tri = jnp.asarray(np.tril(np.ones((SB_TK, SB_TK), np.float32), -1), jnp.bfloat16)
    pw = _prep_weights(norm_pre, w_in, conv_w, conv_b, mla_q_norm, mla_w_uq, mla_kv_norm,
                       mla_w_ukv, group_norm, w_out, norm_post)
    rope = _rope_call(pos2, pw["freq"])
    for l in range(depth):
        p = _proj_call(x2, rope, pw, l, seq)
        yc, yd = _attn_call(p["qa"], p["kv"], p["dq"], p["dk"], p["dv"], tri, pw, l, batch, seq)
        x2 = _out_call(x2, attn_sinks, p["aq"], p["ak"], p["av"], p["yb"], yc, yd, pw, l, seq)
    return x2.reshape(batch, seq, D_MODEL)
```

```python
import functools

import numpy as np
import jax
import jax.numpy as jnp
from jax import lax
from jax.experimental import pallas as pl
from jax.experimental.pallas import tpu as pltpu

D_MODEL = 1024
BLOCK = 128
NORM_EPS = 1e-6
SWA_HEADS = 4
SWA_HEAD_DIM = 64
CONV_WIDTH = 256
CONV_K = 3
MLA_HEADS = 4
MLA_Q_RANK = 256
MLA_KV_RANK = 128
MLA_NOPE_DIM = 64
MLA_ROPE_DIM = 32
MLA_V_DIM = 64
ROPE_THETA = 10000.0
SB_HEADS = 4
SB_HEAD_DIM = 64
GROUP_WIDTH = 256
N_GROUPS = 4
D_MIX = GROUP_WIDTH * N_GROUPS

LANES = 128
LAT_WIDTH = 256
ONES_LANE = 160
MLA_JUMP = 32.0
MLA_MIN_SUM = 2.0 ** -64
NEG_BIG = -1e30
SB_DEAD = -160.0

_IN_SIZES = (256, 128, 128, 256, 256, 256, 256, 128, 32, 256, 256, 256, 1024)
_IN_OFF = np.concatenate([[0], np.cumsum(_IN_SIZES)]).astype(int)
(_O_AQ, _O_AK, _O_AV, _O_BB, _O_BC, _O_BX, _O_CQ, _O_CKV, _O_CKR, _O_DQ, _O_DK, _O_DV, _O_GATE,
 _O_END) = [int(v) for v in _IN_OFF]

D_IN = _O_END
P_KR_END = _O_CKR + LANES
P_DQ, P_DK, P_DV, P_END = P_KR_END, P_KR_END + 256, P_KR_END + 512, P_KR_END + 768

ROPE_ROWS = 1024
PROJ_ROWS = 512
MLA_TQ = 512
MLA_TK = 512
SB_TQ = 256
SB_TK = 256
OUT_ROWS = 512
VMEM_LIMIT = 56 * 1024 * 1024


def _rms(v, g):
    return v * lax.rsqrt(jnp.mean(v * v, axis=-1, keepdims=True) + NORM_EPS) * g


def _dot(a, b):
    return jnp.dot(a, b, preferred_element_type=jnp.float32)


def _dot_nt(a, b):
    return lax.dot_general(a, b, (((1,), (1,)), ((), ())), preferred_element_type=jnp.float32)


def _layer_spec(a, l, single=False):
    zeros = (0,) * (a.ndim - 1)
    mode = dict(pipeline_mode=pl.Buffered(1)) if single else {}
    return pl.BlockSpec((None,) + a.shape[1:], lambda *_: (l,) + zeros, **mode)


def _rope_kernel(pos_ref, freq_ref, cos_ref, sin_ref):
    ang = pos_ref[...] * freq_ref[...]
    cos_ref[...] = jnp.cos(ang)
    sin_ref[...] = jnp.sin(ang)


def _rope_call(pos2, freq):
    n = pos2.shape[0]
    tm = ROPE_ROWS
    table = jax.ShapeDtypeStruct((n, LANES), jnp.float32)
    return pl.pallas_call(
        _rope_kernel,
        grid=(n // tm,),
        in_specs=[pl.BlockSpec((tm, 1), lambda i: (i, 0)),
                  pl.BlockSpec(freq.shape, lambda i: (0, 0))],
        out_specs=[pl.BlockSpec((tm, LANES), lambda i: (i, 0))] * 2,
        out_shape=[table, table],
        compiler_params=pltpu.CompilerParams(dimension_semantics=("arbitrary",)),
        name="rope",
    )(pos2, freq)


def _proj_kernel(x_ref, cos_ref, sin_ref, gpre_ref, win_ref, convw_ref, convb_ref, gcq_ref,
                 wuqn_ref, wabs_ref, wuqr_ref, gckv_ref,
                 aq_ref, ak_ref, av_ref, yb_ref, qa_ref, kv_ref, dq_ref, dk_ref, dv_ref,
                 wbf_ref, ubuf_ref, *, tiles_per_seq):
    tm = x_ref.shape[0]
    bf16 = jnp.bfloat16

    @pl.when(pl.program_id(0) == 0)
    def _():
        def put(dst, src, scale=None):
            blk = win_ref[src:src + 256, :].T
            blk = blk if scale is None else blk * scale
            wbf_ref[:, dst:dst + 256] = blk.astype(bf16)

        put(_O_AQ, _O_AQ, SWA_HEAD_DIM ** -0.5)
        for c in range(_O_AK, P_KR_END, 256):
            put(c, c)
        put(P_DQ, _O_DQ, SB_HEAD_DIM ** -0.5 * np.log2(np.e))
        put(P_DK, _O_DK)
        put(P_DV, _O_DV)

    xb = _rms(x_ref[...], gpre_ref[...]).astype(bf16)

    def proj(lo, hi):
        return _dot(xb, wbf_ref[:, lo:hi])

    aq_ref[...] = proj(_O_AQ, _O_AK).astype(bf16)
    akv = proj(_O_AK, _O_BB)
    for ref, blk in ((ak_ref, akv[:, 0:LANES]), (av_ref, akv[:, LANES:2 * LANES])):
        ref[:, 0:LANES] = blk.astype(bf16)
        ref[:, LANES:2 * LANES] = pltpu.roll(blk, SWA_HEAD_DIM, axis=1).astype(bf16)

    dq_ref[...] = proj(P_DQ, P_DK).astype(bf16)
    dk_ref[...] = proj(P_DK, P_DV).astype(bf16)
    dv_ref[...] = proj(P_DV, P_END).astype(bf16)

    ckr = proj(_O_CKV, P_KR_END)

    u = proj(_O_BC, _O_BX) * proj(_O_BX, _O_CQ)

    @pl.when(pl.program_id(0) % tiles_per_seq == 0)
    def _():
        ubuf_ref[0:8, :] = jnp.zeros((8, CONV_WIDTH), jnp.float32)

    ubuf_ref[8:tm + 8, :] = u
    u1 = ubuf_ref[7:tm + 7, :]
    u2 = ubuf_ref[6:tm + 6, :]
    conv = (convw_ref[0:1, :] * u2 + convw_ref[1:2, :] * u1 + convw_ref[2:3, :] * u
            + convb_ref[...])
    yb_ref[...] = proj(_O_BB, _O_BC) * conv
    ubuf_ref[0:8, :] = ubuf_ref[tm:tm + 8, :]

    cosv = cos_ref[...]
    sinv = sin_ref[...]
    lane = lax.broadcasted_iota(jnp.int32, (tm, LANES), 1)
    in_rope = lane < MLA_ROPE_DIM
    qscale = (MLA_NOPE_DIM + MLA_ROPE_DIM) ** -0.5 * np.log2(np.e)

    cq = _rms(proj(_O_CQ, _O_CKV), gcq_ref[...]).astype(bf16)
    qnope = _dot(cq, wuqn_ref[...]).astype(bf16)
    qlat = _dot(qnope, wabs_ref[...])
    qr = _dot(cq, wuqr_ref[...])
    qroped = (qr[:, 0:LANES] * cosv + qr[:, LANES:2 * LANES] * sinv) * qscale
    for h in range(MLA_HEADS):
        rows = slice(h * tm, (h + 1) * tm)
        qa_ref[rows, 0:LANES] = (qlat[:, h * LANES:(h + 1) * LANES] * qscale).astype(bf16)
        mine = qroped if h == 0 else pltpu.roll(qroped, LANES - h * MLA_ROPE_DIM, axis=1)
        qa_ref[rows, LANES:2 * LANES] = jnp.where(in_rope, mine, 0.0).astype(bf16)
    kv_ref[:, 0:LANES] = _rms(ckr[:, 0:LANES], gckv_ref[...]).astype(bf16)
    kr = ckr[:, LANES:2 * LANES]
    half = MLA_ROPE_DIM // 2
    partner = jnp.where(lane < half, -pltpu.roll(kr, LANES - half, axis=1),
                        pltpu.roll(kr, half, axis=1))
    krope = jnp.where(in_rope, kr * cosv + partner * sinv, 0.0)
    kv_ref[:, LANES:2 * LANES] = jnp.where(lane == ONES_LANE - LANES, 1.0, krope).astype(bf16)


def _proj_call(x2, rope, pw, l, seq):
    n = x2.shape[0]
    tm = PROJ_ROWS
    row = lambda w: pl.BlockSpec((tm, w), lambda i: (i, 0))
    bf16 = jnp.bfloat16
    outs = [("aq", 1, 256, bf16), ("ak", 1, 256, bf16), ("av", 1, 256, bf16),
            ("yb", 1, 256, jnp.float32), ("qa", MLA_HEADS, LAT_WIDTH, bf16),
            ("kv", 1, LAT_WIDTH, bf16), ("dq", 1, 256, bf16), ("dk", 1, 256, bf16),
            ("dv", 1, 256, bf16)]
    params = [pw[k] for k in ("gpre", "win", "convw", "convb", "gcq", "wuqn", "wabs", "wuqr",
                              "gckv")]
    res = pl.pallas_call(
        functools.partial(_proj_kernel, tiles_per_seq=seq // tm),
        grid=(n // tm,),
        in_specs=[row(D_MODEL), row(LANES), row(LANES)]
        + [_layer_spec(a, l, single=(k == "win")) for k, a in zip(
            ("gpre", "win", "convw", "convb", "gcq", "wuqn", "wabs", "wuqr", "gckv"), params)],
        out_specs=[pl.BlockSpec((r * tm, w), lambda i: (i, 0)) for _, r, w, _ in outs],
        out_shape=[jax.ShapeDtypeStruct((r * n, w), dt) for _, r, w, dt in outs],
        scratch_shapes=[pltpu.VMEM((D_MODEL, P_END), bf16),
                        pltpu.VMEM((tm + 8, CONV_WIDTH), jnp.float32)],
        compiler_params=pltpu.CompilerParams(dimension_semantics=("arbitrary",),
                                             vmem_limit_bytes=VMEM_LIMIT),
        name="proj",
    )(x2, *rope, *params)
    return {name: r for (name, _, _, _), r in zip(outs, res)}


def _swa_tile(sink_ref, q_ref, k_ref, kp_ref, v_ref, vp_ref, layer, first_tile):
    bf16 = jnp.bfloat16
    nsub = q_ref.shape[0] // BLOCK
    out = []
    row = lax.broadcasted_iota(jnp.int32, (BLOCK, 2 * BLOCK), 0)
    col = lax.broadcasted_iota(jnp.int32, (BLOCK, 2 * BLOCK), 1)
    band = (col > row) & (col <= row + BLOCK)
    band_first = band & ((col >= BLOCK) | jnp.logical_not(first_tile))
    upper = lax.broadcasted_iota(jnp.int32, (BLOCK, LANES), 1) >= SWA_HEAD_DIM
    for j in range(nsub):
        rows = slice(j * BLOCK, (j + 1) * BLOCK)
        if j == 0:
            kk = jnp.concatenate([kp_ref[...], k_ref[0:BLOCK, :]], axis=0)
            vv = jnp.concatenate([vp_ref[...], v_ref[0:BLOCK, :]], axis=0)
            mask = band_first
        else:
            kk = k_ref[(j - 1) * BLOCK:(j + 1) * BLOCK, :]
            vv = v_ref[(j - 1) * BLOCK:(j + 1) * BLOCK, :]
            mask = band
        blocks = []
        for lb in range(2):
            q2 = q_ref[rows, lb * LANES:(lb + 1) * LANES]
            halves = []
            for hh in range(2):
                sink = sink_ref[layer, 2 * lb + hh]
                qm = jnp.where(upper == (hh == 1), q2, jnp.zeros_like(q2))
                sel = slice(0, LANES) if lb == hh else slice(LANES, 2 * LANES)
                s = jnp.where(mask, _dot_nt(qm, kk[:, sel]), NEG_BIG)
                m = jnp.maximum(jnp.max(s, axis=1, keepdims=True), sink)
                p = jnp.exp(s - m)
                den = jnp.sum(p, axis=1, keepdims=True) + jnp.exp(sink - m)
                halves.append(_dot(p.astype(bf16), vv[:, sel]) / den)
            blocks.append(jnp.where(upper, halves[1], halves[0]))
        out.append(jnp.concatenate(blocks, axis=1))
    return jnp.concatenate(out, axis=0)


def _attn_kernel(q_ref, kv_ref, wuv_ref, dq_ref, dk_ref, dv_ref, tri_ref, o_ref, od_ref, m_ref,
                 acc_ref):
    tq, tk = MLA_TQ, MLA_TK
    bf16 = jnp.bfloat16
    rows = MLA_HEADS * tq
    i = pl.program_id(1)
    q = q_ref[...]

    def exact_update(off, masked):
        kv = kv_ref[pl.ds(off, tk), :]
        s = _dot_nt(q, kv)
        if masked:
            row = lax.broadcasted_iota(jnp.int32, (rows, tk), 0) & (tq - 1)
            col = lax.broadcasted_iota(jnp.int32, (rows, tk), 1)
            s = jnp.where(col <= row, s, NEG_BIG)
        m = m_ref[:, 0:1]
        m_new = jnp.maximum(m, jnp.max(s, axis=1, keepdims=True))
        p = jnp.exp2(s - m_new)
        acc_ref[...] = jnp.exp2(m - m_new) * acc_ref[...] + _dot(p.astype(bf16), kv)
        m_ref[...] = jnp.broadcast_to(m_new, (rows, LANES))

    def lazy_update(off, width):
        kv = kv_ref[pl.ds(off, width), :]
        d = _dot_nt(q, kv) - jnp.concatenate([m_ref[...]] * (width // LANES), axis=1)
        acc_new = acc_ref[...] + _dot(jnp.exp2(d).astype(bf16), kv)
        safe = jnp.max(d) <= MLA_JUMP

        @pl.when(safe)
        def _():
            acc_ref[...] = acc_new

        @pl.when(jnp.logical_not(safe))
        def _():
            def redo(j, carry):
                exact_update(pl.multiple_of(off + j * tk, tk), False)
                return carry
            lax.fori_loop(0, width // tk, redo, 0)

    off_d = pl.multiple_of(i * tk, tk)
    kv_d = kv_ref[pl.ds(off_d, tk), :]
    row = lax.broadcasted_iota(jnp.int32, (rows, tk), 0) & (tq - 1)
    col = lax.broadcasted_iota(jnp.int32, (rows, tk), 1)
    d0 = jnp.where(col <= row, _dot_nt(q, kv_d), NEG_BIG)
    acc0 = _dot(jnp.exp2(d0).astype(bf16), kv_d)
    sums = acc0[:, LANES:2 * LANES]
    ones_lane = lax.broadcasted_iota(jnp.int32, (rows, LANES), 1) == ONES_LANE - LANES
    smallest = jnp.min(jnp.where(ones_lane, sums, 1.0))
    fine = jnp.logical_and(jnp.max(d0) <= MLA_JUMP, smallest >= MLA_MIN_SUM)

    sb_tiles = [_sb_open(dq_ref, j, dk_ref, dv_ref, tri_ref, (tq // SB_TQ) * i + j)
                for j in range(tq // SB_TQ)]

    @pl.when(fine)
    def _():
        m_ref[...] = jnp.zeros((rows, LANES), jnp.float32)
        acc_ref[...] = acc0

    @pl.when(jnp.logical_not(fine))
    def _():
        m_ref[...] = jnp.full((rows, LANES), NEG_BIG, jnp.float32)
        acc_ref[...] = jnp.zeros((rows, LAT_WIDTH), jnp.float32)
        exact_update(off_d, True)

    for j, opened in enumerate(sb_tiles):
        _sb_close(opened, (tq // SB_TQ) * i + j, od_ref, j)

    def pair(c, carry):
        lazy_update(pl.multiple_of(c * 2 * tk, 2 * tk), 2 * tk)
        return carry

    lax.fori_loop(0, i // 2, pair, 0)

    @pl.when(i % 2 == 1)
    def _():
        lazy_update(pl.multiple_of((i - 1) * tk, tk), tk)

    acc = acc_ref[...]
    lat = (acc[:, 0:LANES] / acc[:, ONES_LANE:ONES_LANE + 1]).astype(bf16)
    heads_on_lanes = jnp.concatenate([lat[h * tq:(h + 1) * tq, :] for h in range(MLA_HEADS)], axis=1)
    o_ref[...] = _dot(heads_on_lanes, wuv_ref[...].reshape(MLA_HEADS * MLA_KV_RANK, -1))


def _sb_open(q_ref, j, k_ref, v_ref, tri_ref, i):
    tq, tk = SB_TQ, SB_TK
    bf16 = jnp.bfloat16
    upper_q = lax.broadcasted_iota(jnp.int32, (tq, LANES), 1) >= SB_HEAD_DIM
    qs = []
    for lb in range(2):
        q2 = q_ref[j * tq:(j + 1) * tq, lb * LANES:(lb + 1) * LANES]
        zero = jnp.zeros_like(q2)
        qs.append(jnp.concatenate([jnp.where(upper_q, zero, q2), jnp.where(upper_q, q2, zero)],
                                  axis=0))
    row = lax.broadcasted_iota(jnp.int32, (2 * tq, tk), 0) & (tq - 1)
    col = lax.broadcasted_iota(jnp.int32, (2 * tq, tk), 1)
    strict = col < row
    tri = tri_ref[...]

    def step(c, state, masked):
        off = pl.multiple_of(c * tk, tk)
        new = []
        for lb in range(2):
            csum, acc = state[lb]
            k2 = k_ref[pl.ds(off, tk), lb * LANES:(lb + 1) * LANES]
            v2 = v_ref[pl.ds(off, tk), lb * LANES:(lb + 1) * LANES]
            z = _dot_nt(qs[lb], k2)
            if masked:
                z = jnp.where(strict, z, NEG_BIG)
            log_beta = jnp.minimum(z, 0.0) - jnp.log2(1.0 + jnp.exp2(-jnp.abs(z)))
            log_keep = log_beta - z
            hi = log_keep.astype(bf16)
            lo = (log_keep - hi.astype(jnp.float32)).astype(bf16)
            within = _dot(jnp.concatenate([hi, lo], axis=1), tri)
            a = jnp.exp2(log_beta + (csum + within))
            acc = acc + _dot(a.astype(bf16), v2)
            csum = csum + jnp.sum(log_keep, axis=1, keepdims=True)
            new.append((csum, acc))
        return tuple(new)

    init = tuple((jnp.zeros((2 * tq, 1), jnp.float32), jnp.zeros((2 * tq, LANES), jnp.float32))
                 for _ in range(2))
    state = step(i, init, True)
    gone = jnp.where(i > 0, 0.0, NEG_BIG)
    state = step(jnp.maximum(i - 1, 0), tuple((csum + gone, acc) for csum, acc in state), False)
    return step, state


def _sb_close(opened, i, o_ref, j):
    tq = SB_TQ
    step, state = opened

    def least_decayed(st):
        return jnp.max(jnp.maximum(st[0][0], st[1][0]))

    def cond(carry):
        t, worst, _ = carry
        return jnp.logical_and(t < i - 1, worst > SB_DEAD)

    def body(carry):
        t, _, st = carry
        st = step(i - 2 - t, st, False)
        return t + 1, least_decayed(st), st

    _, _, state = lax.while_loop(cond, body, (jnp.int32(0), least_decayed(state), state))
    upper_q = lax.broadcasted_iota(jnp.int32, (tq, LANES), 1) >= SB_HEAD_DIM
    for lb in range(2):
        acc = state[lb][1]
        o_ref[j * tq:(j + 1) * tq, lb * LANES:(lb + 1) * LANES] = jnp.where(
            upper_q, acc[tq:2 * tq], acc[0:tq])


def _attn_call(qa, kv, dq, dk, dv, tri, pw, l, batch, seq):
    tq = MLA_TQ
    assert tq & (tq - 1) == 0 and PROJ_ROWS == tq and tq % SB_TQ == 0
    nq = seq // tq
    rows = MLA_HEADS * tq
    per_seq = lambda a, w: a.reshape(batch, seq, w)
    whole = lambda w: pl.BlockSpec((None, seq, w), lambda b, i: (b, 0, 0))
    tile = lambda w: pl.BlockSpec((None, tq, w), lambda b, i: (b, i, 0))
    out_mla, out_sb = pl.pallas_call(
        _attn_kernel,
        grid=(batch, nq),
        in_specs=[pl.BlockSpec((rows, LAT_WIDTH), lambda b, i: (b * nq + i, 0)),
                  whole(LAT_WIDTH), _layer_spec(pw["wuv"], l),
                  tile(256), whole(256), whole(256),
                  pl.BlockSpec((2 * SB_TK, SB_TK), lambda b, i: (0, 0))],
        out_specs=[tile(256), tile(256)],
        out_shape=[jax.ShapeDtypeStruct((batch, seq, 256), jnp.float32)] * 2,
        scratch_shapes=[pltpu.VMEM((rows, LANES), jnp.float32),
                        pltpu.VMEM((rows, LAT_WIDTH), jnp.float32)],
        compiler_params=pltpu.CompilerParams(
            dimension_semantics=("arbitrary", "arbitrary"),
            vmem_limit_bytes=VMEM_LIMIT),
        name="attn",
    )(qa, per_seq(kv, LAT_WIDTH), pw["wuv"], per_seq(dq, 256), per_seq(dk, 256),
      per_seq(dv, 256), tri)
    return out_mla.reshape(batch * seq, 256), out_sb.reshape(batch * seq, 256)


def _out_kernel(sink_ref, x_ref, aq_ref, ak_ref, akp_ref, av_ref, avp_ref, yb_ref, yc_ref, yd_ref,
                gpre_ref, win_ref, ggrp_ref, wout_ref, gpost_ref, o_ref, wgate_ref, woutb_ref, *,
                layer, tiles_per_seq):
    bf16 = jnp.bfloat16

    @pl.when(pl.program_id(0) == 0)
    def _():
        for c in range(0, D_MIX, 256):
            wgate_ref[:, c:c + 256] = win_ref[_O_GATE + c:_O_GATE + c + 256, :].T.astype(bf16)
        woutb_ref[...] = wout_ref[...].astype(bf16)

    first_tile = (pl.program_id(0) % tiles_per_seq) == 0
    ya = _swa_tile(sink_ref, aq_ref, ak_ref, akp_ref, av_ref, avp_ref, layer, first_tile)
    x = x_ref[...]
    xb = _rms(x, gpre_ref[...]).astype(bf16)
    gates = _dot(xb, wgate_ref[...])
    ys = []
    for g, y_in in enumerate((ya, yb_ref, yc_ref, yd_ref)):
        sl = slice(g * GROUP_WIDTH, (g + 1) * GROUP_WIDTH)
        gate = gates[:, sl]
        y = _rms(y_in[...], ggrp_ref[:, sl]) * (gate * jax.nn.sigmoid(gate))
        ys.append(y.astype(bf16))
    d = _dot(jnp.concatenate(ys, axis=1), woutb_ref[...])
    o_ref[...] = x + _rms(d, gpost_ref[...])


def _out_call(x2, sinks, aq, ak, av, yb, yc, yd, pw, l, seq):
    n = x2.shape[0]
    tm = OUT_ROWS
    per = tm // BLOCK
    row = lambda w: pl.BlockSpec((tm, w), lambda i: (i, 0))
    prev = lambda w: pl.BlockSpec((BLOCK, w), lambda i: (jnp.maximum(i * per - 1, 0), 0))
    names = ("gpre", "win", "ggrp", "wout", "gpost")
    params = [pw[k] for k in names]
    return pl.pallas_call(
        functools.partial(_out_kernel, layer=l, tiles_per_seq=seq // tm),
        grid=(n // tm,),
        in_specs=[pl.BlockSpec(memory_space=pltpu.SMEM), row(D_MODEL), row(256), row(256),
                  prev(256), row(256), prev(256), row(256), row(256), row(256)]
        + [_layer_spec(a, l, single=(k in ("win", "wout"))) for k, a in zip(names, params)],
        out_specs=row(D_MODEL),
        out_shape=jax.ShapeDtypeStruct((n, D_MODEL), jnp.float32),
        scratch_shapes=[pltpu.VMEM((D_MODEL, D_MIX), jnp.bfloat16),
                        pltpu.VMEM((D_MIX, D_MODEL), jnp.bfloat16)],
        compiler_params=pltpu.CompilerParams(dimension_semantics=("arbitrary",),
                                             vmem_limit_bytes=VMEM_LIMIT),
        name="out",
    )(sinks, x2, aq, ak, ak, av, av, yb, yc, yd, *params)


def _rope_swap(w):
    half = MLA_ROPE_DIM // 2
    return jnp.concatenate([-w[..., half:], w[..., :half]], axis=-1)


def _prep_weights(norm_pre, w_in, conv_w, conv_b, mla_q_norm, mla_w_uq, mla_kv_norm, mla_w_ukv,
                  group_norm, w_out, norm_post):
    f32, bf16 = jnp.float32, jnp.bfloat16
    depth = w_in.shape[0]
    dqk = MLA_NOPE_DIM + MLA_ROPE_DIM
    uq = mla_w_uq.reshape(depth, MLA_Q_RANK, MLA_HEADS, dqk)
    wuqn = uq[..., :MLA_NOPE_DIM].reshape(depth, MLA_Q_RANK, -1).astype(bf16)
    wuqr = jnp.concatenate(
        [uq[..., MLA_NOPE_DIM:].reshape(depth, MLA_Q_RANK, -1),
         _rope_swap(uq[..., MLA_NOPE_DIM:]).reshape(depth, MLA_Q_RANK, -1)], axis=-1).astype(bf16)
    ukv = mla_w_ukv.reshape(depth, MLA_KV_RANK, MLA_HEADS, MLA_NOPE_DIM + MLA_V_DIM)
    own_head = jnp.eye(MLA_HEADS, dtype=f32)[None, :, None, :, None]
    uk_t = jnp.transpose(ukv[..., :MLA_NOPE_DIM], (0, 2, 3, 1))
    wabs = (uk_t[:, :, :, None, :] * own_head).reshape(
        depth, MLA_HEADS * MLA_NOPE_DIM, MLA_HEADS * MLA_KV_RANK).astype(bf16)
    uv = jnp.transpose(ukv[..., MLA_NOPE_DIM:], (0, 2, 1, 3))
    wuv = (uv[:, :, :, None, :] * own_head).reshape(
        depth, MLA_HEADS, MLA_KV_RANK, MLA_HEADS * MLA_V_DIM).astype(bf16)

    half = MLA_ROPE_DIM // 2
    freqs = (ROPE_THETA ** (-np.arange(half, dtype=np.float32) / half)).astype(np.float32)
    freq = np.tile(np.concatenate([freqs, freqs]), LANES // MLA_ROPE_DIM).reshape(1, LANES)

    return dict(
        gpre=norm_pre.reshape(depth, 1, D_MODEL), win=jnp.swapaxes(w_in, 1, 2),
        convw=conv_w, convb=conv_b.reshape(depth, 1, CONV_WIDTH),
        gcq=mla_q_norm.reshape(depth, 1, MLA_Q_RANK), wuqn=wuqn, wabs=wabs, wuqr=wuqr,
        gckv=mla_kv_norm.reshape(depth, 1, MLA_KV_RANK), wuv=wuv,
        freq=jnp.asarray(freq), ggrp=group_norm.reshape(depth, 1, D_MIX),
        wout=w_out,
        gpost=norm_post.reshape(depth, 1, D_MODEL))


def kernel(x, positions, norm_pre, w_in, attn_sinks, conv_w, conv_b, mla_q_norm, mla_w_uq,
           mla_kv_norm, mla_w_ukv, group_norm, w_out, norm_post):
    batch, seq, _ = x.shape
    depth = w_in.shape[0]
    assert seq % max(PROJ_ROWS, MLA_TQ, SB_TQ, OUT_ROWS) == 0 and OUT_ROWS % BLOCK == 0
    assert MLA_TQ == MLA_TK and SB_TQ == SB_TK and SB_TQ & (SB_TQ - 1) == 0
    n = batch * seq
    x2 = x.reshape(n, D_MODEL)
    pos2 = positions.astype(jnp.float32).reshape(n, 1)
    tri = jnp.asarray(np.tile(np.tril(np.ones((SB_TK, SB_TK), np.float32), -1), (2, 1)),
                      jnp.bfloat16)
    pw = _prep_weights(norm_pre, w_in, conv_w, conv_b, mla_q_norm, mla_w_uq, mla_kv_norm,
                       mla_w_ukv, group_norm, w_out, norm_post)
    rope = _rope_call(pos2, pw["freq"])
    for l in range(depth):
        p = _proj_call(x2, rope, pw, l, seq)
        yc, yd = _attn_call(p["qa"], p["kv"], p["dq"], p["dk"], p["dv"], tri, pw, l, batch, seq)
        x2 = _out_call(x2, attn_sinks, p["aq"], p["ak"], p["av"], p["yb"], yc, yd, pw, l, seq)
    return x2.reshape(batch, seq, D_MODEL)
```

```python
import functools

import numpy as np
import jax
import jax.numpy as jnp
from jax import lax
from jax.experimental import pallas as pl
from jax.experimental.pallas import tpu as pltpu

D_MODEL = 1024
BLOCK = 128
NORM_EPS = 1e-6
SWA_HEADS = 4
SWA_HEAD_DIM = 64
CONV_WIDTH = 256
CONV_K = 3
MLA_HEADS = 4
MLA_Q_RANK = 256
MLA_KV_RANK = 128
MLA_NOPE_DIM = 64
MLA_ROPE_DIM = 32
MLA_V_DIM = 64
ROPE_THETA = 10000.0
SB_HEADS = 4
SB_HEAD_DIM = 64
GROUP_WIDTH = 256
N_GROUPS = 4
D_MIX = GROUP_WIDTH * N_GROUPS

LANES = 128
LAT_WIDTH = 256
ONES_LANE = 160
MLA_JUMP = 32.0
MLA_MIN_SUM = 2.0 ** -64
NEG_BIG = -1e30
SB_DEAD = -160.0

_IN_SIZES = (256, 128, 128, 256, 256, 256, 256, 128, 32, 256, 256, 256, 1024)
_IN_OFF = np.concatenate([[0], np.cumsum(_IN_SIZES)]).astype(int)
(_O_AQ, _O_AK, _O_AV, _O_BB, _O_BC, _O_BX, _O_CQ, _O_CKV, _O_CKR, _O_DQ, _O_DK, _O_DV, _O_GATE,
 _O_END) = [int(v) for v in _IN_OFF]

D_IN = _O_END
GATE_SLAB = D_IN // 2
assert D_IN % 2 == 0 and GATE_SLAB % 8 == 0 and GATE_SLAB <= _O_GATE
P_KR_END = _O_CKR + LANES
P_DQ, P_DK, P_DV, P_END = P_KR_END, P_KR_END + 256, P_KR_END + 512, P_KR_END + 768

ROPE_ROWS = 1024
PROJ_ROWS = 1024
MLA_TQ = 512
MLA_TK = 512
SB_TQ = 256
SB_TK = 256
OUT_ROWS = 1024
VMEM_LIMIT = 56 * 1024 * 1024


def _rms(v, g):
    return v * lax.rsqrt(jnp.mean(v * v, axis=-1, keepdims=True) + NORM_EPS) * g


def _dot(a, b):
    return jnp.dot(a, b, preferred_element_type=jnp.float32)


def _dot_nt(a, b):
    return lax.dot_general(a, b, (((1,), (1,)), ((), ())), preferred_element_type=jnp.float32)


def _layer_spec(a, l, single=False):
    zeros = (0,) * (a.ndim - 1)
    mode = dict(pipeline_mode=pl.Buffered(1)) if single else {}
    return pl.BlockSpec((None,) + a.shape[1:], lambda *_: (l,) + zeros, **mode)


def _rope_kernel(pos_ref, freq_ref, cos_ref, sin_ref):
    ang = pos_ref[...] * freq_ref[...]
    cos_ref[...] = jnp.cos(ang)
    sin_ref[...] = jnp.sin(ang)


def _rope_call(pos2, freq):
    n = pos2.shape[0]
    tm = ROPE_ROWS
    table = jax.ShapeDtypeStruct((n, LANES), jnp.float32)
    return pl.pallas_call(
        _rope_kernel,
        grid=(n // tm,),
        in_specs=[pl.BlockSpec((tm, 1), lambda i: (i, 0)),
                  pl.BlockSpec(freq.shape, lambda i: (0, 0))],
        out_specs=[pl.BlockSpec((tm, LANES), lambda i: (i, 0))] * 2,
        out_shape=[table, table],
        compiler_params=pltpu.CompilerParams(dimension_semantics=("arbitrary",)),
        name="rope",
    )(pos2, freq)


def _proj_kernel(x_ref, cos_ref, sin_ref, gpre_ref, win_ref, convw_ref, convb_ref, gcq_ref,
                 wuqn_ref, wabs_ref, wuqr_ref, gckv_ref,
                 aq_ref, ak_ref, av_ref, yb_ref, qa_ref, kv_ref, dq_ref, dk_ref, dv_ref,
                 wbf_ref, ubuf_ref, *, tiles_per_seq):
    tm = x_ref.shape[0]
    bf16 = jnp.bfloat16

    @pl.when(pl.program_id(0) == 0)
    def _():
        def put(dst, src, scale=None):
            blk = win_ref[src:src + 256, :].T
            blk = blk if scale is None else blk * scale
            wbf_ref[:, dst:dst + 256] = blk.astype(bf16)

        put(_O_AQ, _O_AQ, SWA_HEAD_DIM ** -0.5)
        for c in range(_O_AK, P_KR_END, 256):
            put(c, c)
        put(P_DQ, _O_DQ, SB_HEAD_DIM ** -0.5 * np.log2(np.e))
        put(P_DK, _O_DK)
        put(P_DV, _O_DV)

    xb = _rms(x_ref[...], gpre_ref[...]).astype(bf16)

    def proj(lo, hi):
        return _dot(xb, wbf_ref[:, lo:hi])

    aq_ref[...] = proj(_O_AQ, _O_AK).astype(bf16)
    akv = proj(_O_AK, _O_BB)
    for ref, blk in ((ak_ref, akv[:, 0:LANES]), (av_ref, akv[:, LANES:2 * LANES])):
        ref[:, 0:LANES] = blk.astype(bf16)
        ref[:, LANES:2 * LANES] = pltpu.roll(blk, SWA_HEAD_DIM, axis=1).astype(bf16)

    dq_ref[...] = proj(P_DQ, P_DK).astype(bf16)
    dk_ref[...] = proj(P_DK, P_DV).astype(bf16)
    dv_ref[...] = proj(P_DV, P_END).astype(bf16)

    ckr = proj(_O_CKV, P_KR_END)

    u = proj(_O_BC, _O_BX) * proj(_O_BX, _O_CQ)

    @pl.when(pl.program_id(0) % tiles_per_seq == 0)
    def _():
        ubuf_ref[0:8, :] = jnp.zeros((8, CONV_WIDTH), jnp.float32)

    ubuf_ref[8:tm + 8, :] = u
    u1 = ubuf_ref[7:tm + 7, :]
    u2 = ubuf_ref[6:tm + 6, :]
    conv = (convw_ref[0:1, :] * u2 + convw_ref[1:2, :] * u1 + convw_ref[2:3, :] * u
            + convb_ref[...])
    yb_ref[...] = proj(_O_BB, _O_BC) * conv
    ubuf_ref[0:8, :] = ubuf_ref[tm:tm + 8, :]

    cosv = cos_ref[...]
    sinv = sin_ref[...]
    lane = lax.broadcasted_iota(jnp.int32, (tm, LANES), 1)
    in_rope = lane < MLA_ROPE_DIM
    qscale = (MLA_NOPE_DIM + MLA_ROPE_DIM) ** -0.5 * np.log2(np.e)

    cq = _rms(proj(_O_CQ, _O_CKV), gcq_ref[...]).astype(bf16)
    qnope = _dot(cq, wuqn_ref[...]).astype(bf16)
    qlat = _dot(qnope, wabs_ref[...])
    qr = _dot(cq, wuqr_ref[...])
    qroped = (qr[:, 0:LANES] * cosv + qr[:, LANES:2 * LANES] * sinv) * qscale
    for h in range(MLA_HEADS):
        lat_h = (qlat[:, h * LANES:(h + 1) * LANES] * qscale).astype(bf16)
        mine = qroped if h == 0 else pltpu.roll(qroped, LANES - h * MLA_ROPE_DIM, axis=1)
        rope_h = jnp.where(in_rope, mine, 0.0).astype(bf16)
        for t in range(tm // MLA_TQ):
            src = slice(t * MLA_TQ, (t + 1) * MLA_TQ)
            dst = slice((t * MLA_HEADS + h) * MLA_TQ, (t * MLA_HEADS + h + 1) * MLA_TQ)
            qa_ref[dst, 0:LANES] = lat_h[src]
            qa_ref[dst, LANES:2 * LANES] = rope_h[src]
    kv_ref[:, 0:LANES] = _rms(ckr[:, 0:LANES], gckv_ref[...]).astype(bf16)
    kr = ckr[:, LANES:2 * LANES]
    half = MLA_ROPE_DIM // 2
    partner = jnp.where(lane < half, -pltpu.roll(kr, LANES - half, axis=1),
                        pltpu.roll(kr, half, axis=1))
    krope = jnp.where(in_rope, kr * cosv + partner * sinv, 0.0)
    kv_ref[:, LANES:2 * LANES] = jnp.where(lane == ONES_LANE - LANES, 1.0, krope).astype(bf16)


def _proj_call(x2, rope, pw, l, seq):
    n = x2.shape[0]
    tm = PROJ_ROWS
    row = lambda w: pl.BlockSpec((tm, w), lambda i: (i, 0))
    bf16 = jnp.bfloat16
    outs = [("aq", 1, 256, bf16), ("ak", 1, 256, bf16), ("av", 1, 256, bf16),
            ("yb", 1, 256, jnp.float32), ("qa", MLA_HEADS, LAT_WIDTH, bf16),
            ("kv", 1, LAT_WIDTH, bf16), ("dq", 1, 256, bf16), ("dk", 1, 256, bf16),
            ("dv", 1, 256, bf16)]
    params = [pw[k] for k in ("gpre", "win", "convw", "convb", "gcq", "wuqn", "wabs", "wuqr",
                              "gckv")]
    res = pl.pallas_call(
        functools.partial(_proj_kernel, tiles_per_seq=seq // tm),
        grid=(n // tm,),
        in_specs=[row(D_MODEL), row(LANES), row(LANES)]
        + [_layer_spec(a, l, single=(k == "win")) for k, a in zip(
            ("gpre", "win", "convw", "convb", "gcq", "wuqn", "wabs", "wuqr", "gckv"), params)],
        out_specs=[pl.BlockSpec((r * tm, w), lambda i: (i, 0)) for _, r, w, _ in outs],
        out_shape=[jax.ShapeDtypeStruct((r * n, w), dt) for _, r, w, dt in outs],
        scratch_shapes=[pltpu.VMEM((D_MODEL, P_END), bf16),
                        pltpu.VMEM((tm + 8, CONV_WIDTH), jnp.float32)],
        compiler_params=pltpu.CompilerParams(dimension_semantics=("arbitrary",),
                                             vmem_limit_bytes=VMEM_LIMIT),
        name="proj",
    )(x2, *rope, *params)
    return {name: r for (name, _, _, _), r in zip(outs, res)}


def _swa_tile(sink_ref, q_ref, k_ref, kp_ref, v_ref, vp_ref, layer, first_tile):
    bf16 = jnp.bfloat16
    nsub = q_ref.shape[0] // BLOCK
    out = []
    row = lax.broadcasted_iota(jnp.int32, (BLOCK, 2 * BLOCK), 0)
    col = lax.broadcasted_iota(jnp.int32, (BLOCK, 2 * BLOCK), 1)
    band = (col > row) & (col <= row + BLOCK)
    band_first = band & ((col >= BLOCK) | jnp.logical_not(first_tile))
    upper = lax.broadcasted_iota(jnp.int32, (BLOCK, LANES), 1) >= SWA_HEAD_DIM
    for j in range(nsub):
        rows = slice(j * BLOCK, (j + 1) * BLOCK)
        if j == 0:
            kk = jnp.concatenate([kp_ref[...], k_ref[0:BLOCK, :]], axis=0)
            vv = jnp.concatenate([vp_ref[...], v_ref[0:BLOCK, :]], axis=0)
            mask = band_first
        else:
            kk = k_ref[(j - 1) * BLOCK:(j + 1) * BLOCK, :]
            vv = v_ref[(j - 1) * BLOCK:(j + 1) * BLOCK, :]
            mask = band
        blocks = []
        for lb in range(2):
            q2 = q_ref[rows, lb * LANES:(lb + 1) * LANES]
            halves = []
            for hh in range(2):
                sink = sink_ref[layer, 2 * lb + hh]
                qm = jnp.where(upper == (hh == 1), q2, jnp.zeros_like(q2))
                sel = slice(0, LANES) if lb == hh else slice(LANES, 2 * LANES)
                s = jnp.where(mask, _dot_nt(qm, kk[:, sel]), NEG_BIG)
                m = jnp.maximum(jnp.max(s, axis=1, keepdims=True), sink)
                p = jnp.exp(s - m)
                den = jnp.sum(p, axis=1, keepdims=True) + jnp.exp(sink - m)
                halves.append(_dot(p.astype(bf16), vv[:, sel]) / den)
            blocks.append(jnp.where(upper, halves[1], halves[0]))
        out.append(jnp.concatenate(blocks, axis=1))
    return jnp.concatenate(out, axis=0)


def _attn_kernel(q_ref, kv_ref, wuv_ref, dq_ref, dk_ref, dv_ref, tri_ref, o_ref, od_ref, m_ref,
                 acc_ref):
    tq, tk = MLA_TQ, MLA_TK
    bf16 = jnp.bfloat16
    rows = MLA_HEADS * tq
    i = pl.program_id(1)
    q = q_ref[...]

    def exact_update(off, masked):
        kv = kv_ref[pl.ds(off, tk), :]
        s = _dot_nt(q, kv)
        if masked:
            row = lax.broadcasted_iota(jnp.int32, (rows, tk), 0) & (tq - 1)
            col = lax.broadcasted_iota(jnp.int32, (rows, tk), 1)
            s = jnp.where(col <= row, s, NEG_BIG)
        m = m_ref[:, 0:1]
        m_new = jnp.maximum(m, jnp.max(s, axis=1, keepdims=True))
        p = jnp.exp2(s - m_new)
        acc_ref[...] = jnp.exp2(m - m_new) * acc_ref[...] + _dot(p.astype(bf16), kv)
        m_ref[...] = jnp.broadcast_to(m_new, (rows, LANES))

    def lazy_update(off, width):
        kv = kv_ref[pl.ds(off, width), :]
        d = _dot_nt(q, kv) - jnp.concatenate([m_ref[...]] * (width // LANES), axis=1)
        acc_new = acc_ref[...] + _dot(jnp.exp2(d).astype(bf16), kv)
        safe = jnp.max(d) <= MLA_JUMP

        @pl.when(safe)
        def _():
            acc_ref[...] = acc_new

        @pl.when(jnp.logical_not(safe))
        def _():
            def redo(j, carry):
                exact_update(pl.multiple_of(off + j * tk, tk), False)
                return carry
            lax.fori_loop(0, width // tk, redo, 0)

    off_d = pl.multiple_of(i * tk, tk)
    kv_d = kv_ref[pl.ds(off_d, tk), :]
    row = lax.broadcasted_iota(jnp.int32, (rows, tk), 0) & (tq - 1)
    col = lax.broadcasted_iota(jnp.int32, (rows, tk), 1)
    d0 = jnp.where(col <= row, _dot_nt(q, kv_d), NEG_BIG)
    acc0 = _dot(jnp.exp2(d0).astype(bf16), kv_d)
    sums = acc0[:, LANES:2 * LANES]
    ones_lane = lax.broadcasted_iota(jnp.int32, (rows, LANES), 1) == ONES_LANE - LANES
    smallest = jnp.min(jnp.where(ones_lane, sums, 1.0))
    fine = jnp.logical_and(jnp.max(d0) <= MLA_JUMP, smallest >= MLA_MIN_SUM)

    sb_tiles = [_sb_open(dq_ref, j, dk_ref, dv_ref, tri_ref, (tq // SB_TQ) * i + j)
                for j in range(tq // SB_TQ)]

    @pl.when(fine)
    def _():
        m_ref[...] = jnp.zeros((rows, LANES), jnp.float32)
        acc_ref[...] = acc0

    @pl.when(jnp.logical_not(fine))
    def _():
        m_ref[...] = jnp.full((rows, LANES), NEG_BIG, jnp.float32)
        acc_ref[...] = jnp.zeros((rows, LAT_WIDTH), jnp.float32)
        exact_update(off_d, True)

    for j, opened in enumerate(sb_tiles):
        _sb_close(opened, (tq // SB_TQ) * i + j, od_ref, j)

    def pair(c, carry):
        lazy_update(pl.multiple_of(c * 2 * tk, 2 * tk), 2 * tk)
        return carry

    lax.fori_loop(0, i // 2, pair, 0)

    @pl.when(i % 2 == 1)
    def _():
        lazy_update(pl.multiple_of((i - 1) * tk, tk), tk)

    acc = acc_ref[...]
    lat = (acc[:, 0:LANES] / acc[:, ONES_LANE:ONES_LANE + 1]).astype(bf16)
    heads_on_lanes = jnp.concatenate([lat[h * tq:(h + 1) * tq, :] for h in range(MLA_HEADS)], axis=1)
    o_ref[...] = _dot(heads_on_lanes, wuv_ref[...].reshape(MLA_HEADS * MLA_KV_RANK, -1))


def _sb_open(q_ref, j, k_ref, v_ref, tri_ref, i):
    tq, tk = SB_TQ, SB_TK
    bf16 = jnp.bfloat16
    upper_q = lax.broadcasted_iota(jnp.int32, (tq, LANES), 1) >= SB_HEAD_DIM
    qs = []
    for lb in range(2):
        q2 = q_ref[j * tq:(j + 1) * tq, lb * LANES:(lb + 1) * LANES]
        zero = jnp.zeros_like(q2)
        qs.append(jnp.concatenate([jnp.where(upper_q, zero, q2), jnp.where(upper_q, q2, zero)],
                                  axis=0))
    row = lax.broadcasted_iota(jnp.int32, (2 * tq, tk), 0) & (tq - 1)
    col = lax.broadcasted_iota(jnp.int32, (2 * tq, tk), 1)
    strict = col < row
    tri = tri_ref[...]

    def step(c, state, masked):
        off = pl.multiple_of(c * tk, tk)
        new = []
        for lb in range(2):
            csum, acc = state[lb]
            k2 = k_ref[pl.ds(off, tk), lb * LANES:(lb + 1) * LANES]
            v2 = v_ref[pl.ds(off, tk), lb * LANES:(lb + 1) * LANES]
            z = _dot_nt(qs[lb], k2)
            if masked:
                z = jnp.where(strict, z, NEG_BIG)
            log_beta = jnp.minimum(z, 0.0) - jnp.log2(1.0 + jnp.exp2(-jnp.abs(z)))
            log_keep = log_beta - z
            hi = log_keep.astype(bf16)
            lo = (log_keep - hi.astype(jnp.float32)).astype(bf16)
            within = _dot(jnp.concatenate([hi, lo], axis=1), tri)
            a = jnp.exp2(log_beta + (csum + within))
            acc = acc + _dot(a.astype(bf16), v2)
            csum = csum + jnp.sum(log_keep, axis=1, keepdims=True)
            new.append((csum, acc))
        return tuple(new)

    init = tuple((jnp.zeros((2 * tq, 1), jnp.float32), jnp.zeros((2 * tq, LANES), jnp.float32))
                 for _ in range(2))
    state = step(i, init, True)
    gone = jnp.where(i > 0, 0.0, NEG_BIG)
    state = step(jnp.maximum(i - 1, 0), tuple((csum + gone, acc) for csum, acc in state), False)
    return step, state


def _sb_close(opened, i, o_ref, j):
    tq = SB_TQ
    step, state = opened

    def least_decayed(st):
        return jnp.max(jnp.maximum(st[0][0], st[1][0]))

    def cond(carry):
        t, worst, _ = carry
        return jnp.logical_and(t < i - 1, worst > SB_DEAD)

    def body(carry):
        t, _, st = carry
        st = step(i - 2 - t, st, False)
        return t + 1, least_decayed(st), st

    _, _, state = lax.while_loop(cond, body, (jnp.int32(0), least_decayed(state), state))
    upper_q = lax.broadcasted_iota(jnp.int32, (tq, LANES), 1) >= SB_HEAD_DIM
    for lb in range(2):
        acc = state[lb][1]
        o_ref[j * tq:(j + 1) * tq, lb * LANES:(lb + 1) * LANES] = jnp.where(
            upper_q, acc[tq:2 * tq], acc[0:tq])


def _attn_call(qa, kv, dq, dk, dv, tri, pw, l, batch, seq):
    tq = MLA_TQ
    assert tq & (tq - 1) == 0 and PROJ_ROWS % tq == 0 and tq % SB_TQ == 0
    nq = seq // tq
    rows = MLA_HEADS * tq
    per_seq = lambda a, w: a.reshape(batch, seq, w)
    whole = lambda w: pl.BlockSpec((None, seq, w), lambda b, i: (b, 0, 0))
    tile = lambda w: pl.BlockSpec((None, tq, w), lambda b, i: (b, i, 0))
    out_mla, out_sb = pl.pallas_call(
        _attn_kernel,
        grid=(batch, nq),
        in_specs=[pl.BlockSpec((rows, LAT_WIDTH), lambda b, i: (b * nq + i, 0)),
                  whole(LAT_WIDTH), _layer_spec(pw["wuv"], l),
                  tile(256), whole(256), whole(256),
                  pl.BlockSpec((2 * SB_TK, SB_TK), lambda b, i: (0, 0))],
        out_specs=[tile(256), tile(256)],
        out_shape=[jax.ShapeDtypeStruct((batch, seq, 256), jnp.float32)] * 2,
        scratch_shapes=[pltpu.VMEM((rows, LANES), jnp.float32),
                        pltpu.VMEM((rows, LAT_WIDTH), jnp.float32)],
        compiler_params=pltpu.CompilerParams(
            dimension_semantics=("arbitrary", "arbitrary"),
            vmem_limit_bytes=VMEM_LIMIT),
        name="attn",
    )(qa, per_seq(kv, LAT_WIDTH), pw["wuv"], per_seq(dq, 256), per_seq(dk, 256),
      per_seq(dv, 256), tri)
    return out_mla.reshape(batch * seq, 256), out_sb.reshape(batch * seq, 256)


def _out_kernel(sink_ref, x_ref, aq_ref, ak_ref, akp_ref, av_ref, avp_ref, yb_ref, yc_ref, yd_ref,
                gpre_ref, win_ref, ggrp_ref, wout_ref, gpost_ref, o_ref, wgate_ref, woutb_ref, *,
                layer, tiles_per_seq):
    bf16 = jnp.bfloat16

    @pl.when(pl.program_id(0) == 0)
    def _():
        for c in range(0, D_MIX, 256):
            lo = _O_GATE - GATE_SLAB + c
            wgate_ref[:, c:c + 256] = win_ref[lo:lo + 256, :].T.astype(bf16)
        woutb_ref[...] = wout_ref[...].astype(bf16)

    first_tile = (pl.program_id(0) % tiles_per_seq) == 0
    ya = _swa_tile(sink_ref, aq_ref, ak_ref, akp_ref, av_ref, avp_ref, layer, first_tile)
    x = x_ref[...]
    xb = _rms(x, gpre_ref[...]).astype(bf16)
    gates = _dot(xb, wgate_ref[...])
    ys = []
    for g, y_in in enumerate((ya, yb_ref, yc_ref, yd_ref)):
        sl = slice(g * GROUP_WIDTH, (g + 1) * GROUP_WIDTH)
        gate = gates[:, sl]
        y = _rms(y_in[...], ggrp_ref[:, sl]) * (gate * jax.nn.sigmoid(gate))
        ys.append(y.astype(bf16))
    d = _dot(jnp.concatenate(ys, axis=1), woutb_ref[...])
    o_ref[...] = x + _rms(d, gpost_ref[...])


def _out_call(x2, sinks, aq, ak, av, yb, yc, yd, pw, l, seq):
    n = x2.shape[0]
    tm = OUT_ROWS
    per = tm // BLOCK
    row = lambda w: pl.BlockSpec((tm, w), lambda i: (i, 0))
    prev = lambda w: pl.BlockSpec((BLOCK, w), lambda i: (jnp.maximum(i * per - 1, 0), 0))
    names = ("gpre", "win", "ggrp", "wout", "gpost")
    params = [pw[k] for k in names]
    gate_slab = pl.BlockSpec((None, D_IN - GATE_SLAB, D_MODEL), lambda i: (l, 1, 0),
                             pipeline_mode=pl.Buffered(1))
    return pl.pallas_call(
        functools.partial(_out_kernel, layer=l, tiles_per_seq=seq // tm),
        grid=(n // tm,),
        in_specs=[pl.BlockSpec(memory_space=pltpu.SMEM), row(D_MODEL), row(256), row(256),
                  prev(256), row(256), prev(256), row(256), row(256), row(256)]
        + [gate_slab if k == "win" else _layer_spec(a, l, single=(k == "wout"))
           for k, a in zip(names, params)],
        out_specs=row(D_MODEL),
        out_shape=jax.ShapeDtypeStruct((n, D_MODEL), jnp.float32),
        scratch_shapes=[pltpu.VMEM((D_MODEL, D_MIX), jnp.bfloat16),
                        pltpu.VMEM((D_MIX, D_MODEL), jnp.bfloat16)],
        compiler_params=pltpu.CompilerParams(dimension_semantics=("arbitrary",),
                                             vmem_limit_bytes=VMEM_LIMIT),
        name="out",
    )(sinks, x2, aq, ak, ak, av, av, yb, yc, yd, *params)


def _rope_swap(w):
    half = MLA_ROPE_DIM // 2
    return jnp.concatenate([-w[..., half:], w[..., :half]], axis=-1)


def _prep_weights(norm_pre, w_in, conv_w, conv_b, mla_q_norm, mla_w_uq, mla_kv_norm, mla_w_ukv,
                  group_norm, w_out, norm_post):
    f32, bf16 = jnp.float32, jnp.bfloat16
    depth = w_in.shape[0]
    dqk = MLA_NOPE_DIM + MLA_ROPE_DIM
    uq = mla_w_uq.reshape(depth, MLA_Q_RANK, MLA_HEADS, dqk)
    wuqn = uq[..., :MLA_NOPE_DIM].reshape(depth, MLA_Q_RANK, -1).astype(bf16)
    wuqr = jnp.concatenate(
        [uq[..., MLA_NOPE_DIM:].reshape(depth, MLA_Q_RANK, -1),
         _rope_swap(uq[..., MLA_NOPE_DIM:]).reshape(depth, MLA_Q_RANK, -1)], axis=-1).astype(bf16)
    ukv = mla_w_ukv.reshape(depth, MLA_KV_RANK, MLA_HEADS, MLA_NOPE_DIM + MLA_V_DIM)
    own_head = jnp.eye(MLA_HEADS, dtype=f32)[None, :, None, :, None]
    uk_t = jnp.transpose(ukv[..., :MLA_NOPE_DIM], (0, 2, 3, 1))
    wabs = (uk_t[:, :, :, None, :] * own_head).reshape(
        depth, MLA_HEADS * MLA_NOPE_DIM, MLA_HEADS * MLA_KV_RANK).astype(bf16)
    uv = jnp.transpose(ukv[..., MLA_NOPE_DIM:], (0, 2, 1, 3))
    wuv = (uv[:, :, :, None, :] * own_head).reshape(
        depth, MLA_HEADS, MLA_KV_RANK, MLA_HEADS * MLA_V_DIM).astype(bf16)

    half = MLA_ROPE_DIM // 2
    freqs = (ROPE_THETA ** (-np.arange(half, dtype=np.float32) / half)).astype(np.float32)
    freq = np.tile(np.concatenate([freqs, freqs]), LANES // MLA_ROPE_DIM).reshape(1, LANES)

    return dict(
        gpre=norm_pre.reshape(depth, 1, D_MODEL), win=jnp.swapaxes(w_in, 1, 2),
        convw=conv_w, convb=conv_b.reshape(depth, 1, CONV_WIDTH),
        gcq=mla_q_norm.reshape(depth, 1, MLA_Q_RANK), wuqn=wuqn, wabs=wabs, wuqr=wuqr,
        gckv=mla_kv_norm.reshape(depth, 1, MLA_KV_RANK), wuv=wuv,
        freq=jnp.asarray(freq), ggrp=group_norm.reshape(depth, 1, D_MIX),
        wout=w_out,
        gpost=norm_post.reshape(depth, 1, D_MODEL))


def kernel(x, positions, norm_pre, w_in, attn_sinks, conv_w, conv_b, mla_q_norm, mla_w_uq,
           mla_kv_norm, mla_w_ukv, group_norm, w_out, norm_post):
    batch, seq, _ = x.shape
    depth = w_in.shape[0]
    assert seq % max(PROJ_ROWS, MLA_TQ, SB_TQ, OUT_ROWS) == 0 and OUT_ROWS % BLOCK == 0
    assert MLA_TQ == MLA_TK and SB_TQ == SB_TK and SB_TQ & (SB_TQ - 1) == 0
    n = batch * seq
    x2 = x.reshape(n, D_MODEL)
    pos2 = positions.astype(jnp.float32).reshape(n, 1)
    tri = jnp.asarray(np.tile(np.tril(np.ones((SB_TK, SB_TK), np.float32), -1), (2, 1)),
                      jnp.bfloat16)
    pw = _prep_weights(norm_pre, w_in, conv_w, conv_b, mla_q_norm, mla_w_uq, mla_kv_norm,
                       mla_w_ukv, group_norm, w_out, norm_post)
    rope = _rope_call(pos2, pw["freq"])
    for l in range(depth):
        p = _proj_call(x2, rope, pw, l, seq)
        yc, yd = _attn_call(p["qa"], p["kv"], p["dq"], p["dk"], p["dv"], tri, pw, l, batch, seq)
        x2 = _out_call(x2, attn_sinks, p["aq"], p["ak"], p["av"], p["yb"], yc, yd, pw, l, seq)
    return x2.reshape(batch, seq, D_MODEL)
```

```python
import functools

import numpy as np
import jax
import jax.numpy as jnp
from jax import lax
from jax.experimental import pallas as pl
from jax.experimental.pallas import tpu as pltpu

D_MODEL = 1024
BLOCK = 128
NORM_EPS = 1e-6
SWA_HEADS = 4
SWA_HEAD_DIM = 64
CONV_WIDTH = 256
CONV_K = 3
MLA_HEADS = 4
MLA_Q_RANK = 256
MLA_KV_RANK = 128
MLA_NOPE_DIM = 64
MLA_ROPE_DIM = 32
MLA_V_DIM = 64
ROPE_THETA = 10000.0
SB_HEADS = 4
SB_HEAD_DIM = 64
GROUP_WIDTH = 256
N_GROUPS = 4
D_MIX = GROUP_WIDTH * N_GROUPS

LANES = 128
LAT_WIDTH = 256
ONES_LANE = 160
MLA_JUMP = 32.0
MLA_MIN_SUM = 2.0 ** -64
NEG_BIG = -1e30
SB_DEAD = -160.0

_IN_SIZES = (256, 128, 128, 256, 256, 256, 256, 128, 32, 256, 256, 256, 1024)
_IN_OFF = np.concatenate([[0], np.cumsum(_IN_SIZES)]).astype(int)
(_O_AQ, _O_AK, _O_AV, _O_BB, _O_BC, _O_BX, _O_CQ, _O_CKV, _O_CKR, _O_DQ, _O_DK, _O_DV, _O_GATE,
 _O_END) = [int(v) for v in _IN_OFF]

D_IN = _O_END
GATE_SLAB = D_IN // 2
assert D_IN % 2 == 0 and GATE_SLAB % 8 == 0 and GATE_SLAB <= _O_GATE
P_KR_END = _O_CKR + LANES
P_DQ, P_DK, P_DV, P_END = P_KR_END, P_KR_END + 256, P_KR_END + 512, P_KR_END + 768

ROPE_ROWS = 1024
PROJ_ROWS = 1024
MLA_TQ = 512
MLA_TK = 512
SB_TQ = 256
SB_TK = 256
OUT_ROWS = 1024
VMEM_LIMIT = 56 * 1024 * 1024


def _rms(v, g):
    return v * lax.rsqrt(jnp.mean(v * v, axis=-1, keepdims=True) + NORM_EPS) * g


def _dot(a, b):
    return jnp.dot(a, b, preferred_element_type=jnp.float32)


def _dot_nt(a, b):
    return lax.dot_general(a, b, (((1,), (1,)), ((), ())), preferred_element_type=jnp.float32)


def _layer_spec(a, l, single=False):
    zeros = (0,) * (a.ndim - 1)
    mode = dict(pipeline_mode=pl.Buffered(1)) if single else {}
    return pl.BlockSpec((None,) + a.shape[1:], lambda *_: (l,) + zeros, **mode)


def _rope_kernel(pos_ref, freq_ref, cos_ref, sin_ref):
    ang = pos_ref[...] * freq_ref[...]
    cos_ref[...] = jnp.cos(ang)
    sin_ref[...] = jnp.sin(ang)


def _rope_call(pos2, freq):
    n = pos2.shape[0]
    tm = ROPE_ROWS
    table = jax.ShapeDtypeStruct((n, LANES), jnp.float32)
    return pl.pallas_call(
        _rope_kernel,
        grid=(n // tm,),
        in_specs=[pl.BlockSpec((tm, 1), lambda i: (i, 0)),
                  pl.BlockSpec(freq.shape, lambda i: (0, 0))],
        out_specs=[pl.BlockSpec((tm, LANES), lambda i: (i, 0))] * 2,
        out_shape=[table, table],
        compiler_params=pltpu.CompilerParams(dimension_semantics=("arbitrary",)),
        name="rope",
    )(pos2, freq)


def _proj_kernel(x_ref, cos_ref, sin_ref, gpre_ref, win_ref, convw_ref, convb_ref, gcq_ref,
                 wuqn_ref, wabs_ref, wuqr_ref, gckv_ref,
                 aq_ref, ak_ref, av_ref, yb_ref, qa_ref, kv_ref, dq_ref, dk_ref, dv_ref,
                 wbf_ref, ubuf_ref, *, tiles_per_seq):
    tm = x_ref.shape[0]
    bf16 = jnp.bfloat16

    @pl.when(pl.program_id(0) == 0)
    def _():
        def put(dst, src, scale=None):
            blk = win_ref[src:src + 256, :].T
            blk = blk if scale is None else blk * scale
            wbf_ref[:, dst:dst + 256] = blk.astype(bf16)

        put(_O_AQ, _O_AQ, SWA_HEAD_DIM ** -0.5)
        for c in range(_O_AK, P_KR_END, 256):
            put(c, c)
        put(P_DQ, _O_DQ, SB_HEAD_DIM ** -0.5 * np.log2(np.e))
        put(P_DK, _O_DK)
        put(P_DV, _O_DV)

    xb = _rms(x_ref[...], gpre_ref[...]).astype(bf16)

    def proj(lo, hi):
        return _dot(xb, wbf_ref[:, lo:hi])

    aq_ref[...] = proj(_O_AQ, _O_AK).astype(bf16)
    akv = proj(_O_AK, _O_BB)
    for ref, blk in ((ak_ref, akv[:, 0:LANES]), (av_ref, akv[:, LANES:2 * LANES])):
        ref[:, 0:LANES] = blk.astype(bf16)
        ref[:, LANES:2 * LANES] = pltpu.roll(blk, SWA_HEAD_DIM, axis=1).astype(bf16)

    dq_ref[...] = proj(P_DQ, P_DK).astype(bf16)
    dk_ref[...] = proj(P_DK, P_DV).astype(bf16)
    dv_ref[...] = proj(P_DV, P_END).astype(bf16)

    ckr = proj(_O_CKV, P_KR_END)

    u = proj(_O_BC, _O_BX) * proj(_O_BX, _O_CQ)

    @pl.when(pl.program_id(0) % tiles_per_seq == 0)
    def _():
        ubuf_ref[0:8, :] = jnp.zeros((8, CONV_WIDTH), jnp.float32)

    ubuf_ref[8:tm + 8, :] = u
    u1 = ubuf_ref[7:tm + 7, :]
    u2 = ubuf_ref[6:tm + 6, :]
    conv = (convw_ref[0:1, :] * u2 + convw_ref[1:2, :] * u1 + convw_ref[2:3, :] * u
            + convb_ref[...])
    yb_ref[...] = proj(_O_BB, _O_BC) * conv
    ubuf_ref[0:8, :] = ubuf_ref[tm:tm + 8, :]

    cosv = cos_ref[...]
    sinv = sin_ref[...]
    lane = lax.broadcasted_iota(jnp.int32, (tm, LANES), 1)
    in_rope = lane < MLA_ROPE_DIM
    qscale = (MLA_NOPE_DIM + MLA_ROPE_DIM) ** -0.5 * np.log2(np.e)

    cq = _rms(proj(_O_CQ, _O_CKV), gcq_ref[...]).astype(bf16)
    qnope = _dot(cq, wuqn_ref[...]).astype(bf16)
    qlat = _dot(qnope, wabs_ref[...])
    qr = _dot(cq, wuqr_ref[...])
    qroped = (qr[:, 0:LANES] * cosv + qr[:, LANES:2 * LANES] * sinv) * qscale
    for h in range(MLA_HEADS):
        lat_h = (qlat[:, h * LANES:(h + 1) * LANES] * qscale).astype(bf16)
        mine = qroped if h == 0 else pltpu.roll(qroped, LANES - h * MLA_ROPE_DIM, axis=1)
        rope_h = jnp.where(in_rope, mine, 0.0).astype(bf16)
        for t in range(tm // MLA_TQ):
            src = slice(t * MLA_TQ, (t + 1) * MLA_TQ)
            dst = slice((t * MLA_HEADS + h) * MLA_TQ, (t * MLA_HEADS + h + 1) * MLA_TQ)
            qa_ref[dst, 0:LANES] = lat_h[src]
            qa_ref[dst, LANES:2 * LANES] = rope_h[src]
    kv_ref[:, 0:LANES] = _rms(ckr[:, 0:LANES], gckv_ref[...]).astype(bf16)
    kr = ckr[:, LANES:2 * LANES]
    half = MLA_ROPE_DIM // 2
    partner = jnp.where(lane < half, -pltpu.roll(kr, LANES - half, axis=1),
                        pltpu.roll(kr, half, axis=1))
    krope = jnp.where(in_rope, kr * cosv + partner * sinv, 0.0)
    kv_ref[:, LANES:2 * LANES] = jnp.where(lane == ONES_LANE - LANES, 1.0, krope).astype(bf16)


def _proj_call(x2, rope, pw, l, seq):
    n = x2.shape[0]
    tm = PROJ_ROWS
    row = lambda w: pl.BlockSpec((tm, w), lambda i: (i, 0))
    bf16 = jnp.bfloat16
    outs = [("aq", 1, 256, bf16), ("ak", 1, 256, bf16), ("av", 1, 256, bf16),
            ("yb", 1, 256, jnp.float32), ("qa", MLA_HEADS, LAT_WIDTH, bf16),
            ("kv", 1, LAT_WIDTH, bf16), ("dq", 1, 256, bf16), ("dk", 1, 256, bf16),
            ("dv", 1, 256, bf16)]
    params = [pw[k] for k in ("gpre", "win", "convw", "convb", "gcq", "wuqn", "wabs", "wuqr",
                              "gckv")]
    res = pl.pallas_call(
        functools.partial(_proj_kernel, tiles_per_seq=seq // tm),
        grid=(n // tm,),
        in_specs=[row(D_MODEL), row(LANES), row(LANES)]
        + [_layer_spec(a, l, single=(k == "win")) for k, a in zip(
            ("gpre", "win", "convw", "convb", "gcq", "wuqn", "wabs", "wuqr", "gckv"), params)],
        out_specs=[pl.BlockSpec((r * tm, w), lambda i: (i, 0)) for _, r, w, _ in outs],
        out_shape=[jax.ShapeDtypeStruct((r * n, w), dt) for _, r, w, dt in outs],
        scratch_shapes=[pltpu.VMEM((D_MODEL, P_END), bf16),
                        pltpu.VMEM((tm + 8, CONV_WIDTH), jnp.float32)],
        compiler_params=pltpu.CompilerParams(dimension_semantics=("arbitrary",),
                                             vmem_limit_bytes=VMEM_LIMIT),
        name="proj",
    )(x2, *rope, *params)
    return {name: r for (name, _, _, _), r in zip(outs, res)}


def _swa_tile(sink_ref, q_ref, k_ref, kp_ref, v_ref, vp_ref, layer, first_tile):
    bf16 = jnp.bfloat16
    nsub = q_ref.shape[0] // BLOCK
    out = []
    row = lax.broadcasted_iota(jnp.int32, (BLOCK, 2 * BLOCK), 0)
    col = lax.broadcasted_iota(jnp.int32, (BLOCK, 2 * BLOCK), 1)
    band = (col > row) & (col <= row + BLOCK)
    band_first = band & ((col >= BLOCK) | jnp.logical_not(first_tile))
    upper = lax.broadcasted_iota(jnp.int32, (BLOCK, LANES), 1) >= SWA_HEAD_DIM
    for j in range(nsub):
        rows = slice(j * BLOCK, (j + 1) * BLOCK)
        if j == 0:
            kk = jnp.concatenate([kp_ref[...], k_ref[0:BLOCK, :]], axis=0)
            vv = jnp.concatenate([vp_ref[...], v_ref[0:BLOCK, :]], axis=0)
            mask = band_first
        else:
            kk = k_ref[(j - 1) * BLOCK:(j + 1) * BLOCK, :]
            vv = v_ref[(j - 1) * BLOCK:(j + 1) * BLOCK, :]
            mask = band
        blocks = []
        for lb in range(2):
            q2 = q_ref[rows, lb * LANES:(lb + 1) * LANES]
            halves = []
            for hh in range(2):
                sink = sink_ref[layer, 2 * lb + hh]
                qm = jnp.where(upper == (hh == 1), q2, jnp.zeros_like(q2))
                sel = slice(0, LANES) if lb == hh else slice(LANES, 2 * LANES)
                s = jnp.where(mask, _dot_nt(qm, kk[:, sel]), NEG_BIG)
                m = jnp.maximum(jnp.max(s, axis=1, keepdims=True), sink)
                p = jnp.exp(s - m)
                den = jnp.sum(p, axis=1, keepdims=True) + jnp.exp(sink - m)
                halves.append(_dot(p.astype(bf16), vv[:, sel]) / den)
            blocks.append(jnp.where(upper, halves[1], halves[0]))
        out.append(jnp.concatenate(blocks, axis=1))
    return jnp.concatenate(out, axis=0)


def _attn_kernel(q_ref, kv_ref, wuv_ref, dq_ref, dk_ref, dv_ref, tri_ref, o_ref, od_ref, m_ref,
                 acc_ref):
    tq, tk = MLA_TQ, MLA_TK
    bf16 = jnp.bfloat16
    rows = MLA_HEADS * tq
    i = pl.program_id(1)
    q = q_ref[...]

    def exact_update(off, masked):
        kv = kv_ref[pl.ds(off, tk), :]
        s = _dot_nt(q, kv)
        if masked:
            row = lax.broadcasted_iota(jnp.int32, (rows, tk), 0) & (tq - 1)
            col = lax.broadcasted_iota(jnp.int32, (rows, tk), 1)
            s = jnp.where(col <= row, s, NEG_BIG)
        m = m_ref[:, 0:1]
        m_new = jnp.maximum(m, jnp.max(s, axis=1, keepdims=True))
        p = jnp.exp2(s - m_new)
        acc_ref[...] = jnp.exp2(m - m_new) * acc_ref[...] + _dot(p.astype(bf16), kv)
        m_ref[...] = jnp.broadcast_to(m_new, (rows, LANES))

    def lazy_update(off, width):
        kv = kv_ref[pl.ds(off, width), :]
        d = _dot_nt(q, kv) - jnp.concatenate([m_ref[...]] * (width // LANES), axis=1)
        acc_new = acc_ref[...] + _dot(jnp.exp2(d).astype(bf16), kv)
        safe = jnp.max(d) <= MLA_JUMP

        @pl.when(safe)
        def _():
            acc_ref[...] = acc_new

        @pl.when(jnp.logical_not(safe))
        def _():
            def redo(j, carry):
                exact_update(pl.multiple_of(off + j * tk, tk), False)
                return carry
            lax.fori_loop(0, width // tk, redo, 0)

    off_d = pl.multiple_of(i * tk, tk)
    kv_d = kv_ref[pl.ds(off_d, tk), :]
    row = lax.broadcasted_iota(jnp.int32, (rows, tk), 0) & (tq - 1)
    col = lax.broadcasted_iota(jnp.int32, (rows, tk), 1)
    d0 = jnp.where(col <= row, _dot_nt(q, kv_d), NEG_BIG)
    acc0 = _dot(jnp.exp2(d0).astype(bf16), kv_d)
    sums = acc0[:, LANES:2 * LANES]
    ones_lane = lax.broadcasted_iota(jnp.int32, (rows, LANES), 1) == ONES_LANE - LANES
    smallest = jnp.min(jnp.where(ones_lane, sums, 1.0))
    fine = jnp.logical_and(jnp.max(d0) <= MLA_JUMP, smallest >= MLA_MIN_SUM)

    sb_tiles = [_sb_open(dq_ref, j, dk_ref, dv_ref, tri_ref, (tq // SB_TQ) * i + j)
                for j in range(tq // SB_TQ)]

    @pl.when(fine)
    def _():
        m_ref[...] = jnp.zeros((rows, LANES), jnp.float32)
        acc_ref[...] = acc0

    @pl.when(jnp.logical_not(fine))
    def _():
        m_ref[...] = jnp.full((rows, LANES), NEG_BIG, jnp.float32)
        acc_ref[...] = jnp.zeros((rows, LAT_WIDTH), jnp.float32)
        exact_update(off_d, True)

    for j, opened in enumerate(sb_tiles):
        _sb_close(opened, (tq // SB_TQ) * i + j, od_ref, j)

    def quad(c, carry):
        lazy_update(pl.multiple_of(c * 4 * tk, tk), 4 * tk)
        return carry

    lax.fori_loop(0, i // 4, quad, 0)

    @pl.when(i % 4 >= 2)
    def _():
        lazy_update(pl.multiple_of((i // 4) * 4 * tk, tk), 2 * tk)

    @pl.when(i % 2 == 1)
    def _():
        lazy_update(pl.multiple_of((i - 1) * tk, tk), tk)

    acc = acc_ref[...]
    lat = (acc[:, 0:LANES] / acc[:, ONES_LANE:ONES_LANE + 1]).astype(bf16)
    heads_on_lanes = jnp.concatenate([lat[h * tq:(h + 1) * tq, :] for h in range(MLA_HEADS)], axis=1)
    o_ref[...] = _dot(heads_on_lanes, wuv_ref[...].reshape(MLA_HEADS * MLA_KV_RANK, -1))


def _sb_open(q_ref, j, k_ref, v_ref, tri_ref, i):
    tq, tk = SB_TQ, SB_TK
    bf16 = jnp.bfloat16
    upper_q = lax.broadcasted_iota(jnp.int32, (tq, LANES), 1) >= SB_HEAD_DIM
    qs = []
    for lb in range(2):
        q2 = q_ref[j * tq:(j + 1) * tq, lb * LANES:(lb + 1) * LANES]
        zero = jnp.zeros_like(q2)
        qs.append(jnp.concatenate([jnp.where(upper_q, zero, q2), jnp.where(upper_q, q2, zero)],
                                  axis=0))
    row = lax.broadcasted_iota(jnp.int32, (2 * tq, tk), 0) & (tq - 1)
    col = lax.broadcasted_iota(jnp.int32, (2 * tq, tk), 1)
    strict = col < row
    tri = tri_ref[...]

    def step(c, state, masked):
        off = pl.multiple_of(c * tk, tk)
        new = []
        for lb in range(2):
            csum, acc = state[lb]
            k2 = k_ref[pl.ds(off, tk), lb * LANES:(lb + 1) * LANES]
            v2 = v_ref[pl.ds(off, tk), lb * LANES:(lb + 1) * LANES]
            z = _dot_nt(qs[lb], k2)
            if masked:
                z = jnp.where(strict, z, NEG_BIG)
            log_beta = jnp.minimum(z, 0.0) - jnp.log2(1.0 + jnp.exp2(-jnp.abs(z)))
            log_keep = log_beta - z
            hi = log_keep.astype(bf16)
            lo = (log_keep - hi.astype(jnp.float32)).astype(bf16)
            within = _dot(jnp.concatenate([hi, lo], axis=1), tri)
            a = jnp.exp2(log_beta + (csum + within))
            acc = acc + _dot(a.astype(bf16), v2)
            csum = csum + jnp.sum(log_keep, axis=1, keepdims=True)
            new.append((csum, acc))
        return tuple(new)

    init = tuple((jnp.zeros((2 * tq, 1), jnp.float32), jnp.zeros((2 * tq, LANES), jnp.float32))
                 for _ in range(2))
    state = step(i, init, True)
    gone = jnp.where(i > 0, 0.0, NEG_BIG)
    state = step(jnp.maximum(i - 1, 0), tuple((csum + gone, acc) for csum, acc in state), False)
    return step, state


def _sb_close(opened, i, o_ref, j):
    tq = SB_TQ
    step, state = opened

    def least_decayed(st):
        return jnp.max(jnp.maximum(st[0][0], st[1][0]))

    def cond(carry):
        t, worst, _ = carry
        return jnp.logical_and(t < i - 1, worst > SB_DEAD)

    def body(carry):
        t, _, st = carry
        st = step(i - 2 - t, st, False)
        return t + 1, least_decayed(st), st

    _, _, state = lax.while_loop(cond, body, (jnp.int32(0), least_decayed(state), state))
    upper_q = lax.broadcasted_iota(jnp.int32, (tq, LANES), 1) >= SB_HEAD_DIM
    for lb in range(2):
        acc = state[lb][1]
        o_ref[j * tq:(j + 1) * tq, lb * LANES:(lb + 1) * LANES] = jnp.where(
            upper_q, acc[tq:2 * tq], acc[0:tq])


def _attn_call(qa, kv, dq, dk, dv, tri, pw, l, batch, seq):
    tq = MLA_TQ
    assert tq & (tq - 1) == 0 and PROJ_ROWS % tq == 0 and tq % SB_TQ == 0
    nq = seq // tq
    rows = MLA_HEADS * tq
    per_seq = lambda a, w: a.reshape(batch, seq, w)
    whole = lambda w: pl.BlockSpec((None, seq, w), lambda b, i: (b, 0, 0))
    tile = lambda w: pl.BlockSpec((None, tq, w), lambda b, i: (b, i, 0))
    out_mla, out_sb = pl.pallas_call(
        _attn_kernel,
        grid=(batch, nq),
        in_specs=[pl.BlockSpec((rows, LAT_WIDTH), lambda b, i: (b * nq + i, 0)),
                  whole(LAT_WIDTH), _layer_spec(pw["wuv"], l),
                  tile(256), whole(256), whole(256),
                  pl.BlockSpec((2 * SB_TK, SB_TK), lambda b, i: (0, 0))],
        out_specs=[tile(256), tile(256)],
        out_shape=[jax.ShapeDtypeStruct((batch, seq, 256), jnp.float32)] * 2,
        scratch_shapes=[pltpu.VMEM((rows, LANES), jnp.float32),
                        pltpu.VMEM((rows, LAT_WIDTH), jnp.float32)],
        compiler_params=pltpu.CompilerParams(
            dimension_semantics=("arbitrary", "arbitrary"),
            vmem_limit_bytes=VMEM_LIMIT),
        name="attn",
    )(qa, per_seq(kv, LAT_WIDTH), pw["wuv"], per_seq(dq, 256), per_seq(dk, 256),
      per_seq(dv, 256), tri)
    return out_mla.reshape(batch * seq, 256), out_sb.reshape(batch * seq, 256)


def _out_kernel(sink_ref, x_ref, aq_ref, ak_ref, akp_ref, av_ref, avp_ref, yb_ref, yc_ref, yd_ref,
                gpre_ref, win_ref, ggrp_ref, wout_ref, gpost_ref, o_ref, wgate_ref, woutb_ref, *,
                layer, tiles_per_seq):
    bf16 = jnp.bfloat16

    @pl.when(pl.program_id(0) == 0)
    def _():
        for c in range(0, D_MIX, 256):
            lo = _O_GATE - GATE_SLAB + c
            wgate_ref[:, c:c + 256] = win_ref[lo:lo + 256, :].T.astype(bf16)
        woutb_ref[...] = wout_ref[...].astype(bf16)

    first_tile = (pl.program_id(0) % tiles_per_seq) == 0
    ya = _swa_tile(sink_ref, aq_ref, ak_ref, akp_ref, av_ref, avp_ref, layer, first_tile)
    x = x_ref[...]
    xb = _rms(x, gpre_ref[...]).astype(bf16)
    gates = _dot(xb, wgate_ref[...])
    ys = []
    for g, y_in in enumerate((ya, yb_ref, yc_ref, yd_ref)):
        sl = slice(g * GROUP_WIDTH, (g + 1) * GROUP_WIDTH)
        gate = gates[:, sl]
        y = _rms(y_in[...], ggrp_ref[:, sl]) * (gate * jax.nn.sigmoid(gate))
        ys.append(y.astype(bf16))
    d = _dot(jnp.concatenate(ys, axis=1), woutb_ref[...])
    o_ref[...] = x + _rms(d, gpost_ref[...])


def _out_call(x2, sinks, aq, ak, av, yb, yc, yd, pw, l, seq):
    n = x2.shape[0]
    tm = OUT_ROWS
    per = tm // BLOCK
    row = lambda w: pl.BlockSpec((tm, w), lambda i: (i, 0))
    prev = lambda w: pl.BlockSpec((BLOCK, w), lambda i: (jnp.maximum(i * per - 1, 0), 0))
    names = ("gpre", "win", "ggrp", "wout", "gpost")
    params = [pw[k] for k in names]
    gate_slab = pl.BlockSpec((None, D_IN - GATE_SLAB, D_MODEL), lambda i: (l, 1, 0),
                             pipeline_mode=pl.Buffered(1))
    return pl.pallas_call(
        functools.partial(_out_kernel, layer=l, tiles_per_seq=seq // tm),
        grid=(n // tm,),
        in_specs=[pl.BlockSpec(memory_space=pltpu.SMEM), row(D_MODEL), row(256), row(256),
                  prev(256), row(256), prev(256), row(256), row(256), row(256)]
        + [gate_slab if k == "win" else _layer_spec(a, l, single=(k == "wout"))
           for k, a in zip(names, params)],
        out_specs=row(D_MODEL),
        out_shape=jax.ShapeDtypeStruct((n, D_MODEL), jnp.float32),
        scratch_shapes=[pltpu.VMEM((D_MODEL, D_MIX), jnp.bfloat16),
                        pltpu.VMEM((D_MIX, D_MODEL), jnp.bfloat16)],
        compiler_params=pltpu.CompilerParams(dimension_semantics=("arbitrary",),
                                             vmem_limit_bytes=VMEM_LIMIT),
        name="out",
    )(sinks, x2, aq, ak, ak, av, av, yb, yc, yd, *params)


def _rope_swap(w):
    half = MLA_ROPE_DIM // 2
    return jnp.concatenate([-w[..., half:], w[..., :half]], axis=-1)


def _prep_weights(norm_pre, w_in, conv_w, conv_b, mla_q_norm, mla_w_uq, mla_kv_norm, mla_w_ukv,
                  group_norm, w_out, norm_post):
    f32, bf16 = jnp.float32, jnp.bfloat16
    depth = w_in.shape[0]
    dqk = MLA_NOPE_DIM + MLA_ROPE_DIM
    uq = mla_w_uq.reshape(depth, MLA_Q_RANK, MLA_HEADS, dqk)
    wuqn = uq[..., :MLA_NOPE_DIM].reshape(depth, MLA_Q_RANK, -1).astype(bf16)
    wuqr = jnp.concatenate(
        [uq[..., MLA_NOPE_DIM:].reshape(depth, MLA_Q_RANK, -1),
         _rope_swap(uq[..., MLA_NOPE_DIM:]).reshape(depth, MLA_Q_RANK, -1)], axis=-1).astype(bf16)
    ukv = mla_w_ukv.reshape(depth, MLA_KV_RANK, MLA_HEADS, MLA_NOPE_DIM + MLA_V_DIM)
    own_head = jnp.eye(MLA_HEADS, dtype=f32)[None, :, None, :, None]
    uk_t = jnp.transpose(ukv[..., :MLA_NOPE_DIM], (0, 2, 3, 1))
    wabs = (uk_t[:, :, :, None, :] * own_head).reshape(
        depth, MLA_HEADS * MLA_NOPE_DIM, MLA_HEADS * MLA_KV_RANK).astype(bf16)
    uv = jnp.transpose(ukv[..., MLA_NOPE_DIM:], (0, 2, 1, 3))
    wuv = (uv[:, :, :, None, :] * own_head).reshape(
        depth, MLA_HEADS, MLA_KV_RANK, MLA_HEADS * MLA_V_DIM).astype(bf16)

    half = MLA_ROPE_DIM // 2
    freqs = (ROPE_THETA ** (-np.arange(half, dtype=np.float32) / half)).astype(np.float32)
    freq = np.tile(np.concatenate([freqs, freqs]), LANES // MLA_ROPE_DIM).reshape(1, LANES)

    return dict(
        gpre=norm_pre.reshape(depth, 1, D_MODEL), win=jnp.swapaxes(w_in, 1, 2),
        convw=conv_w, convb=conv_b.reshape(depth, 1, CONV_WIDTH),
        gcq=mla_q_norm.reshape(depth, 1, MLA_Q_RANK), wuqn=wuqn, wabs=wabs, wuqr=wuqr,
        gckv=mla_kv_norm.reshape(depth, 1, MLA_KV_RANK), wuv=wuv,
        freq=jnp.asarray(freq), ggrp=group_norm.reshape(depth, 1, D_MIX),
        wout=w_out,
        gpost=norm_post.reshape(depth, 1, D_MODEL))


def kernel(x, positions, norm_pre, w_in, attn_sinks, conv_w, conv_b, mla_q_norm, mla_w_uq,
           mla_kv_norm, mla_w_ukv, group_norm, w_out, norm_post):
    batch, seq, _ = x.shape
    depth = w_in.shape[0]
    assert seq % max(PROJ_ROWS, MLA_TQ, SB_TQ, OUT_ROWS) == 0 and OUT_ROWS % BLOCK == 0
    assert MLA_TQ == MLA_TK and SB_TQ == SB_TK and SB_TQ & (SB_TQ - 1) == 0
    n = batch * seq
    x2 = x.reshape(n, D_MODEL)
    pos2 = positions.astype(jnp.float32).reshape(n, 1)
    tri = jnp.asarray(np.tile(np.tril(np.ones((SB_TK, SB_TK), np.float32), -1), (2, 1)),
                      jnp.bfloat16)
    pw = _prep_weights(norm_pre, w_in, conv_w, conv_b, mla_q_norm, mla_w_uq, mla_kv_norm,
                       mla_w_ukv, group_norm, w_out, norm_post)
    rope = _rope_call(pos2, pw["freq"])
    for l in range(depth):
        p = _proj_call(x2, rope, pw, l, seq)
        yc, yd = _attn_call(p["qa"], p["kv"], p["dq"], p["dk"], p["dv"], tri, pw, l, batch, seq)
        x2 = _out_call(x2, attn_sinks, p["aq"], p["ak"], p["av"], p["yb"], yc, yd, pw, l, seq)
    return x2.reshape(batch, seq, D_MODEL)
```

```python
import functools

import numpy as np
import jax
import jax.numpy as jnp
from jax import lax
from jax.experimental import pallas as pl
from jax.experimental.pallas import tpu as pltpu

D_MODEL = 1024
BLOCK = 128
NORM_EPS = 1e-6
SWA_HEADS = 4
SWA_HEAD_DIM = 64
CONV_WIDTH = 256
CONV_K = 3
MLA_HEADS = 4
MLA_Q_RANK = 256
MLA_KV_RANK = 128
MLA_NOPE_DIM = 64
MLA_ROPE_DIM = 32
MLA_V_DIM = 64
ROPE_THETA = 10000.0
SB_HEADS = 4
SB_HEAD_DIM = 64
GROUP_WIDTH = 256
N_GROUPS = 4
D_MIX = GROUP_WIDTH * N_GROUPS

LANES = 128
LAT_WIDTH = 256
ONES_LANE = 160
MLA_JUMP = 32.0
MLA_MIN_SUM = 2.0 ** -64
NEG_BIG = -1e30
SB_DEAD = -160.0

_IN_SIZES = (256, 128, 128, 256, 256, 256, 256, 128, 32, 256, 256, 256, 1024)
_IN_OFF = np.concatenate([[0], np.cumsum(_IN_SIZES)]).astype(int)
(_O_AQ, _O_AK, _O_AV, _O_BB, _O_BC, _O_BX, _O_CQ, _O_CKV, _O_CKR, _O_DQ, _O_DK, _O_DV, _O_GATE,
 _O_END) = [int(v) for v in _IN_OFF]

D_IN = _O_END
GATE_SLAB = D_IN // 2
assert D_IN % 2 == 0 and GATE_SLAB % 8 == 0 and GATE_SLAB <= _O_GATE
P_KR_END = _O_CKR + LANES
P_DQ, P_DK, P_DV, P_END = P_KR_END, P_KR_END + 256, P_KR_END + 512, P_KR_END + 768

ROPE_ROWS = 1024
PROJ_ROWS = 1024
MLA_TQ = 512
MLA_TK = 512
SB_TQ = 256
SB_TK = 256
OUT_ROWS = 1024
VMEM_LIMIT = 56 * 1024 * 1024


def _rms(v, g):
    return v * lax.rsqrt(jnp.mean(v * v, axis=-1, keepdims=True) + NORM_EPS) * g


def _dot(a, b):
    return jnp.dot(a, b, preferred_element_type=jnp.float32)


def _dot_nt(a, b):
    return lax.dot_general(a, b, (((1,), (1,)), ((), ())), preferred_element_type=jnp.float32)


def _layer_spec(a, l, single=False):
    if a.ndim == 2:
        return pl.BlockSpec(a.shape, lambda *_: (0, 0))
    zeros = (0,) * (a.ndim - 1)
    mode = dict(pipeline_mode=pl.Buffered(1)) if single else {}
    return pl.BlockSpec((None,) + a.shape[1:], lambda *_: (l,) + zeros, **mode)


def _row(ref, layer):
    return ref[layer:layer + 1, :]


def _rope_kernel(pos_ref, freq_ref, spread_ref, cos_ref, sin_ref):
    ang = freq_ref[...] * pos_ref[...]
    for ref, table in ((cos_ref, jnp.cos(ang)), (sin_ref, jnp.sin(ang))):
        ref[...] = jnp.dot(spread_ref[...], table, preferred_element_type=jnp.float32,
                           precision=lax.Precision.HIGHEST).T


def _rope_call(positions):
    n = positions.size
    tm = ROPE_ROWS
    half = MLA_ROPE_DIM // 2
    freqs = (ROPE_THETA ** (-np.arange(half, dtype=np.float32) / half)).astype(np.float32)
    spread = np.zeros((LANES, half), np.float32)
    spread[np.arange(LANES), np.arange(LANES) % half] = 1.0
    table = jax.ShapeDtypeStruct((n, LANES), jnp.float32)
    return pl.pallas_call(
        _rope_kernel,
        grid=(n // tm,),
        in_specs=[pl.BlockSpec((None, 1, tm), lambda i: (i, 0, 0)),
                  pl.BlockSpec((half, 1), lambda i: (0, 0)),
                  pl.BlockSpec((LANES, half), lambda i: (0, 0))],
        out_specs=[pl.BlockSpec((tm, LANES), lambda i: (i, 0))] * 2,
        out_shape=[table, table],
        compiler_params=pltpu.CompilerParams(dimension_semantics=("arbitrary",)),
        name="rope",
    )(positions.astype(jnp.float32).reshape(n // tm, 1, tm), jnp.asarray(freqs.reshape(half, 1)),
      jnp.asarray(spread))


def _proj_kernel(x_ref, cos_ref, sin_ref, gpre_ref, win_ref, convw_ref, convb_ref, gcq_ref,
                 wuqn_ref, wabs_ref, wuqr_ref, gckv_ref,
                 aq_ref, ak_ref, av_ref, yb_ref, qa_ref, kv_ref, dq_ref, dk_ref, dv_ref,
                 wbf_ref, ubuf_ref, *, layer, tiles_per_seq):
    tm = x_ref.shape[0]
    bf16 = jnp.bfloat16

    @pl.when(pl.program_id(0) == 0)
    def _():
        def put(dst, src, scale=None):
            blk = win_ref[src:src + 256, :].T
            blk = blk if scale is None else blk * scale
            wbf_ref[:, dst:dst + 256] = blk.astype(bf16)

        put(_O_AQ, _O_AQ, SWA_HEAD_DIM ** -0.5)
        for c in range(_O_AK, P_KR_END, 256):
            put(c, c)
        put(P_DQ, _O_DQ, SB_HEAD_DIM ** -0.5 * np.log2(np.e))
        put(P_DK, _O_DK)
        put(P_DV, _O_DV)

    xb = _rms(x_ref[...], _row(gpre_ref, layer)).astype(bf16)

    def proj(lo, hi):
        return _dot(xb, wbf_ref[:, lo:hi])

    aq_ref[...] = proj(_O_AQ, _O_AK).astype(bf16)
    akv = proj(_O_AK, _O_BB)
    for ref, blk in ((ak_ref, akv[:, 0:LANES]), (av_ref, akv[:, LANES:2 * LANES])):
        ref[:, 0:LANES] = blk.astype(bf16)
        ref[:, LANES:2 * LANES] = pltpu.roll(blk, SWA_HEAD_DIM, axis=1).astype(bf16)

    dq_ref[...] = proj(P_DQ, P_DK).astype(bf16)
    dk_ref[...] = proj(P_DK, P_DV).astype(bf16)
    dv_ref[...] = proj(P_DV, P_END).astype(bf16)

    ckr = proj(_O_CKV, P_KR_END)

    u = proj(_O_BC, _O_BX) * proj(_O_BX, _O_CQ)

    @pl.when(pl.program_id(0) % tiles_per_seq == 0)
    def _():
        ubuf_ref[0:8, :] = jnp.zeros((8, CONV_WIDTH), jnp.float32)

    ubuf_ref[8:tm + 8, :] = u
    u1 = ubuf_ref[7:tm + 7, :]
    u2 = ubuf_ref[6:tm + 6, :]
    conv = (convw_ref[0:1, :] * u2 + convw_ref[1:2, :] * u1 + convw_ref[2:3, :] * u
            + _row(convb_ref, layer))
    yb_ref[...] = proj(_O_BB, _O_BC) * conv
    ubuf_ref[0:8, :] = ubuf_ref[tm:tm + 8, :]

    cosv = cos_ref[...]
    sinv = sin_ref[...]
    lane = lax.broadcasted_iota(jnp.int32, (tm, LANES), 1)
    in_rope = lane < MLA_ROPE_DIM
    qscale = (MLA_NOPE_DIM + MLA_ROPE_DIM) ** -0.5 * np.log2(np.e)

    cq = _rms(proj(_O_CQ, _O_CKV), _row(gcq_ref, layer)).astype(bf16)
    qnope = _dot(cq, wuqn_ref[...]).astype(bf16)
    qlat = _dot(qnope, wabs_ref[...])
    qr = _dot(cq, wuqr_ref[...])
    qroped = (qr[:, 0:LANES] * cosv + qr[:, LANES:2 * LANES] * sinv) * qscale
    for h in range(MLA_HEADS):
        lat_h = (qlat[:, h * LANES:(h + 1) * LANES] * qscale).astype(bf16)
        mine = qroped if h == 0 else pltpu.roll(qroped, LANES - h * MLA_ROPE_DIM, axis=1)
        rope_h = jnp.where(in_rope, mine, 0.0).astype(bf16)
        for t in range(tm // MLA_TQ):
            src = slice(t * MLA_TQ, (t + 1) * MLA_TQ)
            dst = slice((t * MLA_HEADS + h) * MLA_TQ, (t * MLA_HEADS + h + 1) * MLA_TQ)
            qa_ref[dst, 0:LANES] = lat_h[src]
            qa_ref[dst, LANES:2 * LANES] = rope_h[src]
    kv_ref[:, 0:LANES] = _rms(ckr[:, 0:LANES], _row(gckv_ref, layer)).astype(bf16)
    kr = ckr[:, LANES:2 * LANES]
    half = MLA_ROPE_DIM // 2
    partner = jnp.where(lane < half, -pltpu.roll(kr, LANES - half, axis=1),
                        pltpu.roll(kr, half, axis=1))
    krope = jnp.where(in_rope, kr * cosv + partner * sinv, 0.0)
    kv_ref[:, LANES:2 * LANES] = jnp.where(lane == ONES_LANE - LANES, 1.0, krope).astype(bf16)


def _proj_call(x2, rope, pw, l, seq):
    n = x2.shape[0]
    tm = PROJ_ROWS
    row = lambda w: pl.BlockSpec((tm, w), lambda i: (i, 0))
    bf16 = jnp.bfloat16
    outs = [("aq", 1, 256, bf16), ("ak", 1, 256, bf16), ("av", 1, 256, bf16),
            ("yb", 1, 256, jnp.float32), ("qa", MLA_HEADS, LAT_WIDTH, bf16),
            ("kv", 1, LAT_WIDTH, bf16), ("dq", 1, 256, bf16), ("dk", 1, 256, bf16),
            ("dv", 1, 256, bf16)]
    params = [pw[k] for k in ("gpre", "win", "convw", "convb", "gcq", "wuqn", "wabs", "wuqr",
                              "gckv")]
    res = pl.pallas_call(
        functools.partial(_proj_kernel, layer=l, tiles_per_seq=seq // tm),
        grid=(n // tm,),
        in_specs=[row(D_MODEL), row(LANES), row(LANES)]
        + [_layer_spec(a, l, single=(k == "win")) for k, a in zip(
            ("gpre", "win", "convw", "convb", "gcq", "wuqn", "wabs", "wuqr", "gckv"), params)],
        out_specs=[pl.BlockSpec((r * tm, w), lambda i: (i, 0)) for _, r, w, _ in outs],
        out_shape=[jax.ShapeDtypeStruct((r * n, w), dt) for _, r, w, dt in outs],
        scratch_shapes=[pltpu.VMEM((D_MODEL, P_END), bf16),
                        pltpu.VMEM((tm + 8, CONV_WIDTH), jnp.float32)],
        compiler_params=pltpu.CompilerParams(dimension_semantics=("arbitrary",),
                                             vmem_limit_bytes=VMEM_LIMIT),
        name="proj",
    )(x2, *rope, *params)
    return {name: r for (name, _, _, _), r in zip(outs, res)}


def _swa_tile(sink_ref, q_ref, k_ref, kp_ref, v_ref, vp_ref, layer, first_tile):
    bf16 = jnp.bfloat16
    nsub = q_ref.shape[0] // BLOCK
    out = []
    row = lax.broadcasted_iota(jnp.int32, (BLOCK, 2 * BLOCK), 0)
    col = lax.broadcasted_iota(jnp.int32, (BLOCK, 2 * BLOCK), 1)
    band = (col > row) & (col <= row + BLOCK)
    band_first = band & ((col >= BLOCK) | jnp.logical_not(first_tile))
    upper = lax.broadcasted_iota(jnp.int32, (BLOCK, LANES), 1) >= SWA_HEAD_DIM
    for j in range(nsub):
        rows = slice(j * BLOCK, (j + 1) * BLOCK)
        if j == 0:
            kk = jnp.concatenate([kp_ref[...], k_ref[0:BLOCK, :]], axis=0)
            vv = jnp.concatenate([vp_ref[...], v_ref[0:BLOCK, :]], axis=0)
            mask = band_first
        else:
            kk = k_ref[(j - 1) * BLOCK:(j + 1) * BLOCK, :]
            vv = v_ref[(j - 1) * BLOCK:(j + 1) * BLOCK, :]
            mask = band
        blocks = []
        for lb in range(2):
            q2 = q_ref[rows, lb * LANES:(lb + 1) * LANES]
            halves = []
            for hh in range(2):
                sink = sink_ref[layer, 2 * lb + hh]
                qm = jnp.where(upper == (hh == 1), q2, jnp.zeros_like(q2))
                sel = slice(0, LANES) if lb == hh else slice(LANES, 2 * LANES)
                s = jnp.where(mask, _dot_nt(qm, kk[:, sel]), NEG_BIG)
                m = jnp.maximum(jnp.max(s, axis=1, keepdims=True), sink)
                p = jnp.exp(s - m)
                den = jnp.sum(p, axis=1, keepdims=True) + jnp.exp(sink - m)
                halves.append(_dot(p.astype(bf16), vv[:, sel]) / den)
            blocks.append(jnp.where(upper, halves[1], halves[0]))
        out.append(jnp.concatenate(blocks, axis=1))
    return jnp.concatenate(out, axis=0)


def _attn_kernel(q_ref, kv_ref, wuv_ref, dq_ref, dk_ref, dv_ref, tri_ref, o_ref, od_ref, m_ref,
                 acc_ref):
    tq, tk = MLA_TQ, MLA_TK
    bf16 = jnp.bfloat16
    rows = MLA_HEADS * tq
    i = pl.program_id(1)
    q = q_ref[...]

    def exact_update(off, masked):
        kv = kv_ref[pl.ds(off, tk), :]
        s = _dot_nt(q, kv)
        if masked:
            row = lax.broadcasted_iota(jnp.int32, (rows, tk), 0) & (tq - 1)
            col = lax.broadcasted_iota(jnp.int32, (rows, tk), 1)
            s = jnp.where(col <= row, s, NEG_BIG)
        m = m_ref[:, 0:1]
        m_new = jnp.maximum(m, jnp.max(s, axis=1, keepdims=True))
        p = jnp.exp2(s - m_new)
        acc_ref[...] = jnp.exp2(m - m_new) * acc_ref[...] + _dot(p.astype(bf16), kv)
        m_ref[...] = jnp.broadcast_to(m_new, (rows, LANES))

    def lazy_update(off, width):
        kv = kv_ref[pl.ds(off, width), :]
        d = _dot_nt(q, kv) - jnp.concatenate([m_ref[...]] * (width // LANES), axis=1)
        acc_new = acc_ref[...] + _dot(jnp.exp2(d).astype(bf16), kv)
        safe = jnp.max(d) <= MLA_JUMP

        @pl.when(safe)
        def _():
            acc_ref[...] = acc_new

        @pl.when(jnp.logical_not(safe))
        def _():
            def redo(j, carry):
                exact_update(pl.multiple_of(off + j * tk, tk), False)
                return carry
            lax.fori_loop(0, width // tk, redo, 0)

    off_d = pl.multiple_of(i * tk, tk)
    kv_d = kv_ref[pl.ds(off_d, tk), :]
    row = lax.broadcasted_iota(jnp.int32, (rows, tk), 0) & (tq - 1)
    col = lax.broadcasted_iota(jnp.int32, (rows, tk), 1)
    d0 = jnp.where(col <= row, _dot_nt(q, kv_d), NEG_BIG)
    acc0 = _dot(jnp.exp2(d0).astype(bf16), kv_d)
    sums = acc0[:, LANES:2 * LANES]
    ones_lane = lax.broadcasted_iota(jnp.int32, (rows, LANES), 1) == ONES_LANE - LANES
    smallest = jnp.min(jnp.where(ones_lane, sums, 1.0))
    fine = jnp.logical_and(jnp.max(d0) <= MLA_JUMP, smallest >= MLA_MIN_SUM)

    sb_tiles = [_sb_open(dq_ref, j, dk_ref, dv_ref, tri_ref, (tq // SB_TQ) * i + j)
                for j in range(tq // SB_TQ)]

    @pl.when(fine)
    def _():
        m_ref[...] = jnp.zeros((rows, LANES), jnp.float32)
        acc_ref[...] = acc0

    @pl.when(jnp.logical_not(fine))
    def _():
        m_ref[...] = jnp.full((rows, LANES), NEG_BIG, jnp.float32)
        acc_ref[...] = jnp.zeros((rows, LAT_WIDTH), jnp.float32)
        exact_update(off_d, True)

    for j, opened in enumerate(sb_tiles):
        _sb_close(opened, (tq // SB_TQ) * i + j, od_ref, j)

    def quad(c, carry):
        lazy_update(pl.multiple_of(c * 4 * tk, tk), 4 * tk)
        return carry

    lax.fori_loop(0, i // 4, quad, 0)

    @pl.when(i % 4 >= 2)
    def _():
        lazy_update(pl.multiple_of((i // 4) * 4 * tk, tk), 2 * tk)

    @pl.when(i % 2 == 1)
    def _():
        lazy_update(pl.multiple_of((i - 1) * tk, tk), tk)

    acc = acc_ref[...]
    lat = (acc[:, 0:LANES] / acc[:, ONES_LANE:ONES_LANE + 1]).astype(bf16)
    heads_on_lanes = jnp.concatenate([lat[h * tq:(h + 1) * tq, :] for h in range(MLA_HEADS)], axis=1)
    o_ref[...] = _dot(heads_on_lanes, wuv_ref[...].reshape(MLA_HEADS * MLA_KV_RANK, -1))


def _sb_open(q_ref, j, k_ref, v_ref, tri_ref, i):
    tq, tk = SB_TQ, SB_TK
    bf16 = jnp.bfloat16
    upper_q = lax.broadcasted_iota(jnp.int32, (tq, LANES), 1) >= SB_HEAD_DIM
    qs = []
    for lb in range(2):
        q2 = q_ref[j * tq:(j + 1) * tq, lb * LANES:(lb + 1) * LANES]
        zero = jnp.zeros_like(q2)
        qs.append(jnp.concatenate([jnp.where(upper_q, zero, q2), jnp.where(upper_q, q2, zero)],
                                  axis=0))
    row = lax.broadcasted_iota(jnp.int32, (2 * tq, tk), 0) & (tq - 1)
    col = lax.broadcasted_iota(jnp.int32, (2 * tq, tk), 1)
    strict = col < row
    tri = tri_ref[...]

    def step(c, state, masked):
        off = pl.multiple_of(c * tk, tk)
        new = []
        for lb in range(2):
            csum, acc = state[lb]
            k2 = k_ref[pl.ds(off, tk), lb * LANES:(lb + 1) * LANES]
            v2 = v_ref[pl.ds(off, tk), lb * LANES:(lb + 1) * LANES]
            z = _dot_nt(qs[lb], k2)
            if masked:
                z = jnp.where(strict, z, NEG_BIG)
            log_beta = jnp.minimum(z, 0.0) - jnp.log2(1.0 + jnp.exp2(-jnp.abs(z)))
            log_keep = log_beta - z
            hi = log_keep.astype(bf16)
            lo = (log_keep - hi.astype(jnp.float32)).astype(bf16)
            within = _dot(jnp.concatenate([hi, lo], axis=1), tri)
            a = jnp.exp2(log_beta + (csum + within))
            acc = acc + _dot(a.astype(bf16), v2)
            csum = csum + jnp.sum(log_keep, axis=1, keepdims=True)
            new.append((csum, acc))
        return tuple(new)

    init = tuple((jnp.zeros((2 * tq, 1), jnp.float32), jnp.zeros((2 * tq, LANES), jnp.float32))
                 for _ in range(2))
    state = step(i, init, True)
    gone = jnp.where(i > 0, 0.0, NEG_BIG)
    state = step(jnp.maximum(i - 1, 0), tuple((csum + gone, acc) for csum, acc in state), False)
    return step, state


def _sb_close(opened, i, o_ref, j):
    tq = SB_TQ
    step, state = opened

    def least_decayed(st):
        return jnp.max(jnp.maximum(st[0][0], st[1][0]))

    def cond(carry):
        t, worst, _ = carry
        return jnp.logical_and(t < i - 1, worst > SB_DEAD)

    def body(carry):
        t, _, st = carry
        st = step(i - 2 - t, st, False)
        return t + 1, least_decayed(st), st

    _, _, state = lax.while_loop(cond, body, (jnp.int32(0), least_decayed(state), state))
    upper_q = lax.broadcasted_iota(jnp.int32, (tq, LANES), 1) >= SB_HEAD_DIM
    for lb in range(2):
        acc = state[lb][1]
        o_ref[j * tq:(j + 1) * tq, lb * LANES:(lb + 1) * LANES] = jnp.where(
            upper_q, acc[tq:2 * tq], acc[0:tq])


def _attn_call(qa, kv, dq, dk, dv, tri, pw, l, batch, seq):
    tq = MLA_TQ
    assert tq & (tq - 1) == 0 and PROJ_ROWS % tq == 0 and tq % SB_TQ == 0
    nq = seq // tq
    rows = MLA_HEADS * tq
    per_seq = lambda a, w: a.reshape(batch, seq, w)
    whole = lambda w: pl.BlockSpec((None, seq, w), lambda b, i: (b, 0, 0))
    tile = lambda w: pl.BlockSpec((None, tq, w), lambda b, i: (b, i, 0))
    out_mla, out_sb = pl.pallas_call(
        _attn_kernel,
        grid=(batch, nq),
        in_specs=[pl.BlockSpec((rows, LAT_WIDTH), lambda b, i: (b * nq + i, 0)),
                  whole(LAT_WIDTH), _layer_spec(pw["wuv"], l),
                  tile(256), whole(256), whole(256),
                  pl.BlockSpec((2 * SB_TK, SB_TK), lambda b, i: (0, 0))],
        out_specs=[tile(256), tile(256)],
        out_shape=[jax.ShapeDtypeStruct((batch, seq, 256), jnp.float32)] * 2,
        scratch_shapes=[pltpu.VMEM((rows, LANES), jnp.float32),
                        pltpu.VMEM((rows, LAT_WIDTH), jnp.float32)],
        compiler_params=pltpu.CompilerParams(
            dimension_semantics=("arbitrary", "arbitrary"),
            vmem_limit_bytes=VMEM_LIMIT),
        name="attn",
    )(qa, per_seq(kv, LAT_WIDTH), pw["wuv"], per_seq(dq, 256), per_seq(dk, 256),
      per_seq(dv, 256), tri)
    return out_mla.reshape(batch * seq, 256), out_sb.reshape(batch * seq, 256)


def _out_kernel(sink_ref, x_ref, aq_ref, ak_ref, akp_ref, av_ref, avp_ref, yb_ref, yc_ref, yd_ref,
                gpre_ref, win_ref, ggrp_ref, wout_ref, gpost_ref, o_ref, wgate_ref, woutb_ref, *,
                layer, tiles_per_seq):
    bf16 = jnp.bfloat16

    @pl.when(pl.program_id(0) == 0)
    def _():
        for c in range(0, D_MIX, 256):
            lo = _O_GATE - GATE_SLAB + c
            wgate_ref[:, c:c + 256] = win_ref[lo:lo + 256, :].T.astype(bf16)
        woutb_ref[...] = wout_ref[...].astype(bf16)

    first_tile = (pl.program_id(0) % tiles_per_seq) == 0
    ya = _swa_tile(sink_ref, aq_ref, ak_ref, akp_ref, av_ref, avp_ref, layer, first_tile)
    x = x_ref[...]
    xb = _rms(x, _row(gpre_ref, layer)).astype(bf16)
    gates = _dot(xb, wgate_ref[...])
    ys = []
    for g, y_in in enumerate((ya, yb_ref, yc_ref, yd_ref)):
        sl = slice(g * GROUP_WIDTH, (g + 1) * GROUP_WIDTH)
        gate = gates[:, sl]
        y = _rms(y_in[...], ggrp_ref[layer:layer + 1, sl]) * (gate * jax.nn.sigmoid(gate))
        ys.append(y.astype(bf16))
    d = _dot(jnp.concatenate(ys, axis=1), woutb_ref[...])
    o_ref[...] = x + _rms(d, _row(gpost_ref, layer))


def _out_call(x2, sinks, aq, ak, av, yb, yc, yd, pw, l, seq):
    n = x2.shape[0]
    tm = OUT_ROWS
    per = tm // BLOCK
    row = lambda w: pl.BlockSpec((tm, w), lambda i: (i, 0))
    prev = lambda w: pl.BlockSpec((BLOCK, w), lambda i: (jnp.maximum(i * per - 1, 0), 0))
    names = ("gpre", "win", "ggrp", "wout", "gpost")
    params = [pw[k] for k in names]
    gate_slab = pl.BlockSpec((None, D_IN - GATE_SLAB, D_MODEL), lambda i: (l, 1, 0),
                             pipeline_mode=pl.Buffered(1))
    return pl.pallas_call(
        functools.partial(_out_kernel, layer=l, tiles_per_seq=seq // tm),
        grid=(n // tm,),
        in_specs=[pl.BlockSpec(memory_space=pltpu.SMEM), row(D_MODEL), row(256), row(256),
                  prev(256), row(256), prev(256), row(256), row(256), row(256)]
        + [gate_slab if k == "win" else _layer_spec(a, l, single=(k == "wout"))
           for k, a in zip(names, params)],
        out_specs=row(D_MODEL),
        out_shape=jax.ShapeDtypeStruct((n, D_MODEL), jnp.float32),
        scratch_shapes=[pltpu.VMEM((D_MODEL, D_MIX), jnp.bfloat16),
                        pltpu.VMEM((D_MIX, D_MODEL), jnp.bfloat16)],
        compiler_params=pltpu.CompilerParams(dimension_semantics=("arbitrary",),
                                             vmem_limit_bytes=VMEM_LIMIT),
        name="out",
    )(sinks, x2, aq, ak, ak, av, av, yb, yc, yd, *params)


def _rope_swap(w):
    half = MLA_ROPE_DIM // 2
    return jnp.concatenate([-w[..., half:], w[..., :half]], axis=-1)


def _prep_weights(norm_pre, w_in, conv_w, conv_b, mla_q_norm, mla_w_uq, mla_kv_norm, mla_w_ukv,
                  group_norm, w_out, norm_post):
    f32, bf16 = jnp.float32, jnp.bfloat16
    depth = w_in.shape[0]
    dqk = MLA_NOPE_DIM + MLA_ROPE_DIM
    uq = mla_w_uq.reshape(depth, MLA_Q_RANK, MLA_HEADS, dqk)
    wuqn = uq[..., :MLA_NOPE_DIM].reshape(depth, MLA_Q_RANK, -1).astype(bf16)
    wuqr = jnp.concatenate(
        [uq[..., MLA_NOPE_DIM:].reshape(depth, MLA_Q_RANK, -1),
         _rope_swap(uq[..., MLA_NOPE_DIM:]).reshape(depth, MLA_Q_RANK, -1)], axis=-1).astype(bf16)
    ukv = mla_w_ukv.reshape(depth, MLA_KV_RANK, MLA_HEADS, MLA_NOPE_DIM + MLA_V_DIM)
    own_head = jnp.eye(MLA_HEADS, dtype=f32)[None, :, None, :, None]
    uk_t = jnp.transpose(ukv[..., :MLA_NOPE_DIM], (0, 2, 3, 1))
    wabs = (uk_t[:, :, :, None, :] * own_head).reshape(
        depth, MLA_HEADS * MLA_NOPE_DIM, MLA_HEADS * MLA_KV_RANK).astype(bf16)
    uv = jnp.transpose(ukv[..., MLA_NOPE_DIM:], (0, 2, 1, 3))
    wuv = (uv[:, :, :, None, :] * own_head).reshape(
        depth, MLA_HEADS, MLA_KV_RANK, MLA_HEADS * MLA_V_DIM).astype(bf16)

    return dict(
        gpre=norm_pre, win=jnp.swapaxes(w_in, 1, 2), convw=conv_w, convb=conv_b,
        gcq=mla_q_norm, wuqn=wuqn, wabs=wabs, wuqr=wuqr, gckv=mla_kv_norm, wuv=wuv,
        ggrp=group_norm, wout=w_out, gpost=norm_post)


def kernel(x, positions, norm_pre, w_in, attn_sinks, conv_w, conv_b, mla_q_norm, mla_w_uq,
           mla_kv_norm, mla_w_ukv, group_norm, w_out, norm_post):
    batch, seq, _ = x.shape
    depth = w_in.shape[0]
    assert seq % max(PROJ_ROWS, MLA_TQ, SB_TQ, OUT_ROWS) == 0 and OUT_ROWS % BLOCK == 0
    assert MLA_TQ == MLA_TK and SB_TQ == SB_TK and SB_TQ & (SB_TQ - 1) == 0
    n = batch * seq
    x2 = x.reshape(n, D_MODEL)
    tri = jnp.asarray(np.tile(np.tril(np.ones((SB_TK, SB_TK), np.float32), -1), (2, 1)),
                      jnp.bfloat16)
    pw = _prep_weights(norm_pre, w_in, conv_w, conv_b, mla_q_norm, mla_w_uq, mla_kv_norm,
                       mla_w_ukv, group_norm, w_out, norm_post)
    rope = _rope_call(positions)
    for l in range(depth):
        p = _proj_call(x2, rope, pw, l, seq)
        yc, yd = _attn_call(p["qa"], p["kv"], p["dq"], p["dk"], p["dv"], tri, pw, l, batch, seq)
        x2 = _out_call(x2, attn_sinks, p["aq"], p["ak"], p["av"], p["yb"], yc, yd, pw, l, seq)
    return x2.reshape(batch, seq, D_MODEL)
```

```python
import functools

import numpy as np
import jax
import jax.numpy as jnp
from jax import lax
from jax.experimental import pallas as pl
from jax.experimental.pallas import tpu as pltpu

D_MODEL = 1024
BLOCK = 128
NORM_EPS = 1e-6
SWA_HEADS = 4
SWA_HEAD_DIM = 64
CONV_WIDTH = 256
CONV_K = 3
MLA_HEADS = 4
MLA_Q_RANK = 256
MLA_KV_RANK = 128
MLA_NOPE_DIM = 64
MLA_ROPE_DIM = 32
MLA_V_DIM = 64
ROPE_THETA = 10000.0
SB_HEADS = 4
SB_HEAD_DIM = 64
GROUP_WIDTH = 256
N_GROUPS = 4
D_MIX = GROUP_WIDTH * N_GROUPS

LANES = 128
LAT_WIDTH = 256
ONES_LANE = 160
MLA_JUMP = 32.0
MLA_MIN_SUM = 2.0 ** -64
NEG_BIG = -1e30
SB_DEAD = -160.0

_IN_SIZES = (256, 128, 128, 256, 256, 256, 256, 128, 32, 256, 256, 256, 1024)
_IN_OFF = np.concatenate([[0], np.cumsum(_IN_SIZES)]).astype(int)
(_O_AQ, _O_AK, _O_AV, _O_BB, _O_BC, _O_BX, _O_CQ, _O_CKV, _O_CKR, _O_DQ, _O_DK, _O_DV, _O_GATE,
 _O_END) = [int(v) for v in _IN_OFF]

D_IN = _O_END
GATE_SLAB = D_IN // 2
assert D_IN % 2 == 0 and GATE_SLAB % 8 == 0 and GATE_SLAB <= _O_GATE
P_KR_END = _O_CKR + LANES
P_DQ, P_DK, P_DV, P_END = P_KR_END, P_KR_END + 256, P_KR_END + 512, P_KR_END + 768

ROPE_ROWS = 1024
PROJ_ROWS = 1024
MLA_TQ = 512
MLA_TK = 512
SB_TQ = 256
SB_TK = 256
OUT_ROWS = 1024
VMEM_LIMIT = 56 * 1024 * 1024


def _rms(v, g):
    return v * lax.rsqrt(jnp.mean(v * v, axis=-1, keepdims=True) + NORM_EPS) * g


def _dot(a, b):
    return jnp.dot(a, b, preferred_element_type=jnp.float32)


def _dot_nt(a, b):
    return lax.dot_general(a, b, (((1,), (1,)), ((), ())), preferred_element_type=jnp.float32)


def _layer_spec(a, l, single=False):
    if a.ndim == 2:
        return pl.BlockSpec(a.shape, lambda *_: (0, 0))
    zeros = (0,) * (a.ndim - 1)
    mode = dict(pipeline_mode=pl.Buffered(1)) if single else {}
    return pl.BlockSpec((None,) + a.shape[1:], lambda *_: (l,) + zeros, **mode)


def _row(ref, layer):
    return ref[layer:layer + 1, :]


def _rope_kernel(pos_ref, freq_ref, cos_ref, sin_ref):
    ang = freq_ref[...] * pos_ref[...]
    copies = LANES // ang.shape[0]
    for ref, table in ((cos_ref, jnp.cos(ang)), (sin_ref, jnp.sin(ang))):
        ref[...] = jnp.concatenate([table] * copies, axis=0).T


def _rope_call(positions):
    n = positions.size
    tm = ROPE_ROWS
    half = MLA_ROPE_DIM // 2
    freqs = (ROPE_THETA ** (-np.arange(half, dtype=np.float32) / half)).astype(np.float32)
    table = jax.ShapeDtypeStruct((n, LANES), jnp.float32)
    return pl.pallas_call(
        _rope_kernel,
        grid=(n // tm,),
        in_specs=[pl.BlockSpec((None, 1, tm), lambda i: (i, 0, 0)),
                  pl.BlockSpec((half, 1), lambda i: (0, 0))],
        out_specs=[pl.BlockSpec((tm, LANES), lambda i: (i, 0))] * 2,
        out_shape=[table, table],
        compiler_params=pltpu.CompilerParams(dimension_semantics=("arbitrary",)),
        name="rope",
    )(positions.astype(jnp.float32).reshape(n // tm, 1, tm), jnp.asarray(freqs.reshape(half, 1)))


def _proj_kernel(x_ref, cos_ref, sin_ref, gpre_ref, win_ref, convw_ref, convb_ref, gcq_ref,
                 wuqn_ref, wabs_ref, wuqr_ref, gckv_ref,
                 aq_ref, ak_ref, av_ref, yb_ref, qa_ref, kv_ref, dq_ref, dk_ref, dv_ref,
                 wbf_ref, ubuf_ref, *, layer, tiles_per_seq):
    tm = x_ref.shape[0]
    bf16 = jnp.bfloat16

    @pl.when(pl.program_id(0) == 0)
    def _():
        def put(dst, src, scale=None):
            blk = win_ref[src:src + 256, :].T
            blk = blk if scale is None else blk * scale
            wbf_ref[:, dst:dst + 256] = blk.astype(bf16)

        put(_O_AQ, _O_AQ, SWA_HEAD_DIM ** -0.5)
        for c in range(_O_AK, P_KR_END, 256):
            put(c, c)
        put(P_DQ, _O_DQ, SB_HEAD_DIM ** -0.5 * np.log2(np.e))
        put(P_DK, _O_DK)
        put(P_DV, _O_DV)

    xb = _rms(x_ref[...], _row(gpre_ref, layer)).astype(bf16)

    def proj(lo, hi):
        return _dot(xb, wbf_ref[:, lo:hi])

    aq_ref[...] = proj(_O_AQ, _O_AK).astype(bf16)
    akv = proj(_O_AK, _O_BB)
    for ref, blk in ((ak_ref, akv[:, 0:LANES]), (av_ref, akv[:, LANES:2 * LANES])):
        ref[:, 0:LANES] = blk.astype(bf16)
        ref[:, LANES:2 * LANES] = pltpu.roll(blk, SWA_HEAD_DIM, axis=1).astype(bf16)

    dq_ref[...] = proj(P_DQ, P_DK).astype(bf16)
    dk_ref[...] = proj(P_DK, P_DV).astype(bf16)
    dv_ref[...] = proj(P_DV, P_END).astype(bf16)

    ckr = proj(_O_CKV, P_KR_END)

    u = proj(_O_BC, _O_BX) * proj(_O_BX, _O_CQ)

    @pl.when(pl.program_id(0) % tiles_per_seq == 0)
    def _():
        ubuf_ref[0:8, :] = jnp.zeros((8, CONV_WIDTH), jnp.float32)

    ubuf_ref[8:tm + 8, :] = u
    u1 = ubuf_ref[7:tm + 7, :]
    u2 = ubuf_ref[6:tm + 6, :]
    conv = (convw_ref[0:1, :] * u2 + convw_ref[1:2, :] * u1 + convw_ref[2:3, :] * u
            + _row(convb_ref, layer))
    yb_ref[...] = proj(_O_BB, _O_BC) * conv
    ubuf_ref[0:8, :] = ubuf_ref[tm:tm + 8, :]

    cosv = cos_ref[...]
    sinv = sin_ref[...]
    lane = lax.broadcasted_iota(jnp.int32, (tm, LANES), 1)
    in_rope = lane < MLA_ROPE_DIM
    qscale = (MLA_NOPE_DIM + MLA_ROPE_DIM) ** -0.5 * np.log2(np.e)

    cq = _rms(proj(_O_CQ, _O_CKV), _row(gcq_ref, layer)).astype(bf16)
    qnope = _dot(cq, wuqn_ref[...]).astype(bf16)
    qlat = _dot(qnope, wabs_ref[...])
    qr = _dot(cq, wuqr_ref[...])
    qroped = (qr[:, 0:LANES] * cosv + qr[:, LANES:2 * LANES] * sinv) * qscale
    for h in range(MLA_HEADS):
        lat_h = (qlat[:, h * LANES:(h + 1) * LANES] * qscale).astype(bf16)
        mine = qroped if h == 0 else pltpu.roll(qroped, LANES - h * MLA_ROPE_DIM, axis=1)
        rope_h = jnp.where(in_rope, mine, 0.0).astype(bf16)
        for t in range(tm // MLA_TQ):
            src = slice(t * MLA_TQ, (t + 1) * MLA_TQ)
            dst = slice((t * MLA_HEADS + h) * MLA_TQ, (t * MLA_HEADS + h + 1) * MLA_TQ)
            qa_ref[dst, 0:LANES] = lat_h[src]
            qa_ref[dst, LANES:2 * LANES] = rope_h[src]
    kv_ref[:, 0:LANES] = _rms(ckr[:, 0:LANES], _row(gckv_ref, layer)).astype(bf16)
    kr = ckr[:, LANES:2 * LANES]
    half = MLA_ROPE_DIM // 2
    partner = jnp.where(lane < half, -pltpu.roll(kr, LANES - half, axis=1),
                        pltpu.roll(kr, half, axis=1))
    krope = jnp.where(in_rope, kr * cosv + partner * sinv, 0.0)
    kv_ref[:, LANES:2 * LANES] = jnp.where(lane == ONES_LANE - LANES, 1.0, krope).astype(bf16)


def _proj_call(x2, rope, pw, l, seq):
    n = x2.shape[0]
    tm = PROJ_ROWS
    row = lambda w: pl.BlockSpec((tm, w), lambda i: (i, 0))
    bf16 = jnp.bfloat16
    outs = [("aq", 1, 256, bf16), ("ak", 1, 256, bf16), ("av", 1, 256, bf16),
            ("yb", 1, 256, jnp.float32), ("qa", MLA_HEADS, LAT_WIDTH, bf16),
            ("kv", 1, LAT_WIDTH, bf16), ("dq", 1, 256, bf16), ("dk", 1, 256, bf16),
            ("dv", 1, 256, bf16)]
    params = [pw[k] for k in ("gpre", "win", "convw", "convb", "gcq", "wuqn", "wabs", "wuqr",
                              "gckv")]
    res = pl.pallas_call(
        functools.partial(_proj_kernel, layer=l, tiles_per_seq=seq // tm),
        grid=(n // tm,),
        in_specs=[row(D_MODEL), row(LANES), row(LANES)]
        + [_layer_spec(a, l, single=(k == "win")) for k, a in zip(
            ("gpre", "win", "convw", "convb", "gcq", "wuqn", "wabs", "wuqr", "gckv"), params)],
        out_specs=[pl.BlockSpec((r * tm, w), lambda i: (i, 0)) for _, r, w, _ in outs],
        out_shape=[jax.ShapeDtypeStruct((r * n, w), dt) for _, r, w, dt in outs],
        scratch_shapes=[pltpu.VMEM((D_MODEL, P_END), bf16),
                        pltpu.VMEM((tm + 8, CONV_WIDTH), jnp.float32)],
        compiler_params=pltpu.CompilerParams(dimension_semantics=("arbitrary",),
                                             vmem_limit_bytes=VMEM_LIMIT),
        name="proj",
    )(x2, *rope, *params)
    return {name: r for (name, _, _, _), r in zip(outs, res)}


def _swa_tile(sink_ref, q_ref, k_ref, kp_ref, v_ref, vp_ref, layer, first_tile):
    bf16 = jnp.bfloat16
    nsub = q_ref.shape[0] // BLOCK
    out = []
    row = lax.broadcasted_iota(jnp.int32, (BLOCK, 2 * BLOCK), 0)
    col = lax.broadcasted_iota(jnp.int32, (BLOCK, 2 * BLOCK), 1)
    band = (col > row) & (col <= row + BLOCK)
    band_first = band & ((col >= BLOCK) | jnp.logical_not(first_tile))
    upper = lax.broadcasted_iota(jnp.int32, (BLOCK, LANES), 1) >= SWA_HEAD_DIM
    for j in range(nsub):
        rows = slice(j * BLOCK, (j + 1) * BLOCK)
        if j == 0:
            kk = jnp.concatenate([kp_ref[...], k_ref[0:BLOCK, :]], axis=0)
            vv = jnp.concatenate([vp_ref[...], v_ref[0:BLOCK, :]], axis=0)
            mask = band_first
        else:
            kk = k_ref[(j - 1) * BLOCK:(j + 1) * BLOCK, :]
            vv = v_ref[(j - 1) * BLOCK:(j + 1) * BLOCK, :]
            mask = band
        blocks = []
        for lb in range(2):
            q2 = q_ref[rows, lb * LANES:(lb + 1) * LANES]
            halves = []
            for hh in range(2):
                sink = sink_ref[layer, 2 * lb + hh]
                qm = jnp.where(upper == (hh == 1), q2, jnp.zeros_like(q2))
                sel = slice(0, LANES) if lb == hh else slice(LANES, 2 * LANES)
                s = jnp.where(mask, _dot_nt(qm, kk[:, sel]), NEG_BIG)
                m = jnp.maximum(jnp.max(s, axis=1, keepdims=True), sink)
                p = jnp.exp(s - m)
                den = jnp.sum(p, axis=1, keepdims=True) + jnp.exp(sink - m)
                halves.append(_dot(p.astype(bf16), vv[:, sel]) / den)
            blocks.append(jnp.where(upper, halves[1], halves[0]))
        out.append(jnp.concatenate(blocks, axis=1))
    return jnp.concatenate(out, axis=0)


def _attn_kernel(q_ref, kv_ref, wuv_ref, dq_ref, dk_ref, dv_ref, tri_ref, o_ref, od_ref, m_ref,
                 acc_ref):
    tq, tk = MLA_TQ, MLA_TK
    bf16 = jnp.bfloat16
    rows = MLA_HEADS * tq
    i = pl.program_id(1)
    q = q_ref[...]

    def exact_update(off, masked):
        kv = kv_ref[pl.ds(off, tk), :]
        s = _dot_nt(q, kv)
        if masked:
            row = lax.broadcasted_iota(jnp.int32, (rows, tk), 0) & (tq - 1)
            col = lax.broadcasted_iota(jnp.int32, (rows, tk), 1)
            s = jnp.where(col <= row, s, NEG_BIG)
        m = m_ref[:, 0:1]
        m_new = jnp.maximum(m, jnp.max(s, axis=1, keepdims=True))
        p = jnp.exp2(s - m_new)
        acc_ref[...] = jnp.exp2(m - m_new) * acc_ref[...] + _dot(p.astype(bf16), kv)
        m_ref[...] = jnp.broadcast_to(m_new, (rows, LANES))

    def lazy_update(off, width):
        kv = kv_ref[pl.ds(off, width), :]
        d = _dot_nt(q, kv) - jnp.concatenate([m_ref[...]] * (width // LANES), axis=1)
        acc_new = acc_ref[...] + _dot(jnp.exp2(d).astype(bf16), kv)
        safe = jnp.max(d) <= MLA_JUMP

        @pl.when(safe)
        def _():
            acc_ref[...] = acc_new

        @pl.when(jnp.logical_not(safe))
        def _():
            def redo(j, carry):
                exact_update(pl.multiple_of(off + j * tk, tk), False)
                return carry
            lax.fori_loop(0, width // tk, redo, 0)

    off_d = pl.multiple_of(i * tk, tk)
    kv_d = kv_ref[pl.ds(off_d, tk), :]
    row = lax.broadcasted_iota(jnp.int32, (rows, tk), 0) & (tq - 1)
    col = lax.broadcasted_iota(jnp.int32, (rows, tk), 1)
    d0 = jnp.where(col <= row, _dot_nt(q, kv_d), NEG_BIG)
    acc0 = _dot(jnp.exp2(d0).astype(bf16), kv_d)
    sums = acc0[:, LANES:2 * LANES]
    ones_lane = lax.broadcasted_iota(jnp.int32, (rows, LANES), 1) == ONES_LANE - LANES
    smallest = jnp.min(jnp.where(ones_lane, sums, 1.0))
    fine = jnp.logical_and(jnp.max(d0) <= MLA_JUMP, smallest >= MLA_MIN_SUM)

    sb_tiles = [_sb_open(dq_ref, j, dk_ref, dv_ref, tri_ref, (tq // SB_TQ) * i + j)
                for j in range(tq // SB_TQ)]

    @pl.when(fine)
    def _():
        m_ref[...] = jnp.zeros((rows, LANES), jnp.float32)
        acc_ref[...] = acc0

    @pl.when(jnp.logical_not(fine))
    def _():
        m_ref[...] = jnp.full((rows, LANES), NEG_BIG, jnp.float32)
        acc_ref[...] = jnp.zeros((rows, LAT_WIDTH), jnp.float32)
        exact_update(off_d, True)

    for j, opened in enumerate(sb_tiles):
        _sb_close(opened, (tq // SB_TQ) * i + j, od_ref, j)

    def quad(c, carry):
        lazy_update(pl.multiple_of(c * 4 * tk, tk), 4 * tk)
        return carry

    lax.fori_loop(0, i // 4, quad, 0)

    @pl.when(i % 4 >= 2)
    def _():
        lazy_update(pl.multiple_of((i // 4) * 4 * tk, tk), 2 * tk)

    @pl.when(i % 2 == 1)
    def _():
        lazy_update(pl.multiple_of((i - 1) * tk, tk), tk)

    acc = acc_ref[...]
    lat = (acc[:, 0:LANES] / acc[:, ONES_LANE:ONES_LANE + 1]).astype(bf16)
    heads_on_lanes = jnp.concatenate([lat[h * tq:(h + 1) * tq, :] for h in range(MLA_HEADS)], axis=1)
    o_ref[...] = _dot(heads_on_lanes, wuv_ref[...].reshape(MLA_HEADS * MLA_KV_RANK, -1))


def _sb_open(q_ref, j, k_ref, v_ref, tri_ref, i):
    tq, tk = SB_TQ, SB_TK
    bf16 = jnp.bfloat16
    upper_q = lax.broadcasted_iota(jnp.int32, (tq, LANES), 1) >= SB_HEAD_DIM
    qs = []
    for lb in range(2):
        q2 = q_ref[j * tq:(j + 1) * tq, lb * LANES:(lb + 1) * LANES]
        zero = jnp.zeros_like(q2)
        qs.append(jnp.concatenate([jnp.where(upper_q, zero, q2), jnp.where(upper_q, q2, zero)],
                                  axis=0))
    row = lax.broadcasted_iota(jnp.int32, (2 * tq, tk), 0) & (tq - 1)
    col = lax.broadcasted_iota(jnp.int32, (2 * tq, tk), 1)
    strict = col < row
    tri = tri_ref[...]

    def step(c, state, masked):
        off = pl.multiple_of(c * tk, tk)
        new = []
        for lb in range(2):
            csum, acc = state[lb]
            k2 = k_ref[pl.ds(off, tk), lb * LANES:(lb + 1) * LANES]
            v2 = v_ref[pl.ds(off, tk), lb * LANES:(lb + 1) * LANES]
            z = _dot_nt(qs[lb], k2)
            if masked:
                z = jnp.where(strict, z, NEG_BIG)
            log_beta = jnp.minimum(z, 0.0) - jnp.log2(1.0 + jnp.exp2(-jnp.abs(z)))
            log_keep = log_beta - z
            hi = log_keep.astype(bf16)
            lo = (log_keep - hi.astype(jnp.float32)).astype(bf16)
            within = _dot(jnp.concatenate([hi, lo], axis=1), tri)
            a = jnp.exp2(log_beta + (csum + within))
            acc = acc + _dot(a.astype(bf16), v2)
            csum = csum + jnp.sum(log_keep, axis=1, keepdims=True)
            new.append((csum, acc))
        return tuple(new)

    init = tuple((jnp.zeros((2 * tq, 1), jnp.float32), jnp.zeros((2 * tq, LANES), jnp.float32))
                 for _ in range(2))
    state = step(i, init, True)
    gone = jnp.where(i > 0, 0.0, NEG_BIG)
    state = step(jnp.maximum(i - 1, 0), tuple((csum + gone, acc) for csum, acc in state), False)
    return step, state


def _sb_close(opened, i, o_ref, j):
    tq = SB_TQ
    step, state = opened

    def least_decayed(st):
        return jnp.max(jnp.maximum(st[0][0], st[1][0]))

    def cond(carry):
        t, worst, _ = carry
        return jnp.logical_and(t < i - 1, worst > SB_DEAD)

    def body(carry):
        t, _, st = carry
        st = step(i - 2 - t, st, False)
        return t + 1, least_decayed(st), st

    _, _, state = lax.while_loop(cond, body, (jnp.int32(0), least_decayed(state), state))
    upper_q = lax.broadcasted_iota(jnp.int32, (tq, LANES), 1) >= SB_HEAD_DIM
    for lb in range(2):
        acc = state[lb][1]
        o_ref[j * tq:(j + 1) * tq, lb * LANES:(lb + 1) * LANES] = jnp.where(
            upper_q, acc[tq:2 * tq], acc[0:tq])


def _attn_call(qa, kv, dq, dk, dv, tri, pw, l, batch, seq):
    tq = MLA_TQ
    assert tq & (tq - 1) == 0 and PROJ_ROWS % tq == 0 and tq % SB_TQ == 0
    nq = seq // tq
    rows = MLA_HEADS * tq
    per_seq = lambda a, w: a.reshape(batch, seq, w)
    whole = lambda w: pl.BlockSpec((None, seq, w), lambda b, i: (b, 0, 0))
    tile = lambda w: pl.BlockSpec((None, tq, w), lambda b, i: (b, i, 0))
    out_mla, out_sb = pl.pallas_call(
        _attn_kernel,
        grid=(batch, nq),
        in_specs=[pl.BlockSpec((rows, LAT_WIDTH), lambda b, i: (b * nq + i, 0)),
                  whole(LAT_WIDTH), _layer_spec(pw["wuv"], l),
                  tile(256), whole(256), whole(256),
                  pl.BlockSpec((2 * SB_TK, SB_TK), lambda b, i: (0, 0))],
        out_specs=[tile(256), tile(256)],
        out_shape=[jax.ShapeDtypeStruct((batch, seq, 256), jnp.float32)] * 2,
        scratch_shapes=[pltpu.VMEM((rows, LANES), jnp.float32),
                        pltpu.VMEM((rows, LAT_WIDTH), jnp.float32)],
        compiler_params=pltpu.CompilerParams(
            dimension_semantics=("arbitrary", "arbitrary"),
            vmem_limit_bytes=VMEM_LIMIT),
        name="attn",
    )(qa, per_seq(kv, LAT_WIDTH), pw["wuv"], per_seq(dq, 256), per_seq(dk, 256),
      per_seq(dv, 256), tri)
    return out_mla.reshape(batch * seq, 256), out_sb.reshape(batch * seq, 256)


def _out_kernel(sink_ref, x_ref, aq_ref, ak_ref, akp_ref, av_ref, avp_ref, yb_ref, yc_ref, yd_ref,
                gpre_ref, win_ref, ggrp_ref, wout_ref, gpost_ref, o_ref, wgate_ref, woutb_ref, *,
                layer, tiles_per_seq):
    bf16 = jnp.bfloat16

    @pl.when(pl.program_id(0) == 0)
    def _():
        for c in range(0, D_MIX, 256):
            lo = _O_GATE - GATE_SLAB + c
            wgate_ref[:, c:c + 256] = win_ref[lo:lo + 256, :].T.astype(bf16)
        woutb_ref[...] = wout_ref[...].astype(bf16)

    first_tile = (pl.program_id(0) % tiles_per_seq) == 0
    ya = _swa_tile(sink_ref, aq_ref, ak_ref, akp_ref, av_ref, avp_ref, layer, first_tile)
    x = x_ref[...]
    xb = _rms(x, _row(gpre_ref, layer)).astype(bf16)
    gates = _dot(xb, wgate_ref[...])
    ys = []
    for g, y_in in enumerate((ya, yb_ref, yc_ref, yd_ref)):
        sl = slice(g * GROUP_WIDTH, (g + 1) * GROUP_WIDTH)
        gate = gates[:, sl]
        y = _rms(y_in[...], ggrp_ref[layer:layer + 1, sl]) * (gate * jax.nn.sigmoid(gate))
        ys.append(y.astype(bf16))
    d = _dot(jnp.concatenate(ys, axis=1), woutb_ref[...])
    o_ref[...] = x + _rms(d, _row(gpost_ref, layer))


def _out_call(x2, sinks, aq, ak, av, yb, yc, yd, pw, l, seq):
    n = x2.shape[0]
    tm = OUT_ROWS
    per = tm // BLOCK
    row = lambda w: pl.BlockSpec((tm, w), lambda i: (i, 0))
    prev = lambda w: pl.BlockSpec((BLOCK, w), lambda i: (jnp.maximum(i * per - 1, 0), 0))
    names = ("gpre", "win", "ggrp", "wout", "gpost")
    params = [pw[k] for k in names]
    gate_slab = pl.BlockSpec((None, D_IN - GATE_SLAB, D_MODEL), lambda i: (l, 1, 0),
                             pipeline_mode=pl.Buffered(1))
    return pl.pallas_call(
        functools.partial(_out_kernel, layer=l, tiles_per_seq=seq // tm),
        grid=(n // tm,),
        in_specs=[pl.BlockSpec(memory_space=pltpu.SMEM), row(D_MODEL), row(256), row(256),
                  prev(256), row(256), prev(256), row(256), row(256), row(256)]
        + [gate_slab if k == "win" else _layer_spec(a, l, single=(k == "wout"))
           for k, a in zip(names, params)],
        out_specs=row(D_MODEL),
        out_shape=jax.ShapeDtypeStruct((n, D_MODEL), jnp.float32),
        scratch_shapes=[pltpu.VMEM((D_MODEL, D_MIX), jnp.bfloat16),
                        pltpu.VMEM((D_MIX, D_MODEL), jnp.bfloat16)],
        compiler_params=pltpu.CompilerParams(dimension_semantics=("arbitrary",),
                                             vmem_limit_bytes=VMEM_LIMIT),
        name="out",
    )(sinks, x2, aq, ak, ak, av, av, yb, yc, yd, *params)


def _rope_swap(w):
    half = MLA_ROPE_DIM // 2
    return jnp.concatenate([-w[..., half:], w[..., :half]], axis=-1)


def _prep_weights(norm_pre, w_in, conv_w, conv_b, mla_q_norm, mla_w_uq, mla_kv_norm, mla_w_ukv,
                  group_norm, w_out, norm_post):
    f32, bf16 = jnp.float32, jnp.bfloat16
    depth = w_in.shape[0]
    dqk = MLA_NOPE_DIM + MLA_ROPE_DIM
    uq = mla_w_uq.reshape(depth, MLA_Q_RANK, MLA_HEADS, dqk)
    wuqn = uq[..., :MLA_NOPE_DIM].reshape(depth, MLA_Q_RANK, -1).astype(bf16)
    wuqr = jnp.concatenate(
        [uq[..., MLA_NOPE_DIM:].reshape(depth, MLA_Q_RANK, -1),
         _rope_swap(uq[..., MLA_NOPE_DIM:]).reshape(depth, MLA_Q_RANK, -1)], axis=-1).astype(bf16)
    ukv = mla_w_ukv.reshape(depth, MLA_KV_RANK, MLA_HEADS, MLA_NOPE_DIM + MLA_V_DIM)
    own_head = jnp.eye(MLA_HEADS, dtype=f32)[None, :, None, :, None]
    uk_t = jnp.transpose(ukv[..., :MLA_NOPE_DIM], (0, 2, 3, 1))
    wabs = (uk_t[:, :, :, None, :] * own_head).reshape(
        depth, MLA_HEADS * MLA_NOPE_DIM, MLA_HEADS * MLA_KV_RANK).astype(bf16)
    uv = jnp.transpose(ukv[..., MLA_NOPE_DIM:], (0, 2, 1, 3))
    wuv = (uv[:, :, :, None, :] * own_head).reshape(
        depth, MLA_HEADS, MLA_KV_RANK, MLA_HEADS * MLA_V_DIM).astype(bf16)

    return dict(
        gpre=norm_pre, win=jnp.swapaxes(w_in, 1, 2), convw=conv_w, convb=conv_b,
        gcq=mla_q_norm, wuqn=wuqn, wabs=wabs, wuqr=wuqr, gckv=mla_kv_norm, wuv=wuv,
        ggrp=group_norm, wout=w_out, gpost=norm_post)


def kernel(x, positions, norm_pre, w_in, attn_sinks, conv_w, conv_b, mla_q_norm, mla_w_uq,
           mla_kv_norm, mla_w_ukv, group_norm, w_out, norm_post):
    batch, seq, _ = x.shape
    depth = w_in.shape[0]
    assert seq % max(PROJ_ROWS, MLA_TQ, SB_TQ, OUT_ROWS) == 0 and OUT_ROWS % BLOCK == 0
    assert MLA_TQ == MLA_TK and SB_TQ == SB_TK and SB_TQ & (SB_TQ - 1) == 0
    n = batch * seq
    x2 = x.reshape(n, D_MODEL)
    tri = jnp.asarray(np.tile(np.tril(np.ones((SB_TK, SB_TK), np.float32), -1), (2, 1)),
                      jnp.bfloat16)
    pw = _prep_weights(norm_pre, w_in, conv_w, conv_b, mla_q_norm, mla_w_uq, mla_kv_norm,
                       mla_w_ukv, group_norm, w_out, norm_post)
    rope = _rope_call(positions)
    for l in range(depth):
        p = _proj_call(x2, rope, pw, l, seq)
        yc, yd = _attn_call(p["qa"], p["kv"], p["dq"], p["dk"], p["dv"], tri, pw, l, batch, seq)
        x2 = _out_call(x2, attn_sinks, p["aq"], p["ak"], p["av"], p["yb"], yc, yd, pw, l, seq)
    return x2.reshape(batch, seq, D_MODEL)
```

```python
import functools

import numpy as np
import jax
import jax.numpy as jnp
from jax import lax
from jax.experimental import pallas as pl
from jax.experimental.pallas import tpu as pltpu

D_MODEL = 1024
BLOCK = 128
NORM_EPS = 1e-6
SWA_HEADS = 4
SWA_HEAD_DIM = 64
CONV_WIDTH = 256
CONV_K = 3
MLA_HEADS = 4
MLA_Q_RANK = 256
MLA_KV_RANK = 128
MLA_NOPE_DIM = 64
MLA_ROPE_DIM = 32
MLA_V_DIM = 64
ROPE_THETA = 10000.0
SB_HEADS = 4
SB_HEAD_DIM = 64
GROUP_WIDTH = 256
N_GROUPS = 4
D_MIX = GROUP_WIDTH * N_GROUPS

LANES = 128
LAT_WIDTH = 256
ONES_LANE = 160
MLA_JUMP = 32.0
MLA_MIN_SUM = 2.0 ** -64
NEG_BIG = -1e30
SB_DEAD = -160.0

_IN_SIZES = (256, 128, 128, 256, 256, 256, 256, 128, 32, 256, 256, 256, 1024)
_IN_OFF = np.concatenate([[0], np.cumsum(_IN_SIZES)]).astype(int)
(_O_AQ, _O_AK, _O_AV, _O_BB, _O_BC, _O_BX, _O_CQ, _O_CKV, _O_CKR, _O_DQ, _O_DK, _O_DV, _O_GATE,
 _O_END) = [int(v) for v in _IN_OFF]

D_IN = _O_END
GATE_SLAB = D_IN // 2
assert D_IN % 2 == 0 and GATE_SLAB % 8 == 0 and GATE_SLAB <= _O_GATE
P_KR_END = _O_CKR + LANES
P_DQ, P_DK, P_DV, P_END = P_KR_END, P_KR_END + 256, P_KR_END + 512, P_KR_END + 768

ROPE_ROWS = 1024
PROJ_ROWS = 1024
MLA_TQ = 512
MLA_TK = 512
SB_TQ = 256
SB_TK = 256
OUT_ROWS = 1024
VMEM_LIMIT = 56 * 1024 * 1024


def _rms(v, g):
    return v * lax.rsqrt(jnp.mean(v * v, axis=-1, keepdims=True) + NORM_EPS) * g


def _dot(a, b):
    return jnp.dot(a, b, preferred_element_type=jnp.float32)


def _dot_nt(a, b):
    return lax.dot_general(a, b, (((1,), (1,)), ((), ())), preferred_element_type=jnp.float32)


def _layer_spec(a, l, single=False):
    if a.ndim == 2:
        return pl.BlockSpec(a.shape, lambda *_: (0, 0))
    zeros = (0,) * (a.ndim - 1)
    mode = dict(pipeline_mode=pl.Buffered(1)) if single else {}
    return pl.BlockSpec((None,) + a.shape[1:], lambda *_: (l,) + zeros, **mode)


def _row(ref, layer):
    return ref[layer:layer + 1, :]


def _rope_kernel(pos_ref, freq_ref, cos_ref, sin_ref):
    ang = freq_ref[...] * pos_ref[...]
    copies = LANES // ang.shape[0]
    for ref, table in ((cos_ref, jnp.cos(ang)), (sin_ref, jnp.sin(ang))):
        ref[...] = jnp.concatenate([table] * copies, axis=0).T


def _rope_call(positions):
    n = positions.size
    tm = ROPE_ROWS
    half = MLA_ROPE_DIM // 2
    freqs = (ROPE_THETA ** (-np.arange(half, dtype=np.float32) / half)).astype(np.float32)
    table = jax.ShapeDtypeStruct((n, LANES), jnp.float32)
    return pl.pallas_call(
        _rope_kernel,
        grid=(n // tm,),
        in_specs=[pl.BlockSpec((None, 1, tm), lambda i: (i, 0, 0)),
                  pl.BlockSpec((half, 1), lambda i: (0, 0))],
        out_specs=[pl.BlockSpec((tm, LANES), lambda i: (i, 0))] * 2,
        out_shape=[table, table],
        compiler_params=pltpu.CompilerParams(dimension_semantics=("arbitrary",)),
        name="rope",
    )(positions.astype(jnp.float32).reshape(n // tm, 1, tm), jnp.asarray(freqs.reshape(half, 1)))


def _proj_kernel(x_ref, cos_ref, sin_ref, gpre_ref, win_ref, convw_ref, convb_ref, gcq_ref,
                 wuqn_ref, wabs_ref, wuqr_ref, gckv_ref,
                 xn_ref, aq_ref, ak_ref, av_ref, yb_ref, qa_ref, kv_ref, dq_ref, dk_ref, dv_ref,
                 wbf_ref, ubuf_ref, *, layer, tiles_per_seq):
    tm = x_ref.shape[0]
    bf16 = jnp.bfloat16

    @pl.when(pl.program_id(0) == 0)
    def _():
        def put(dst, src, scale=None):
            blk = win_ref[src:src + 256, :].T
            blk = blk if scale is None else blk * scale
            wbf_ref[:, dst:dst + 256] = blk.astype(bf16)

        put(_O_AQ, _O_AQ, SWA_HEAD_DIM ** -0.5 * np.log2(np.e))
        for c in range(_O_AK, P_KR_END, 256):
            put(c, c)
        put(P_DQ, _O_DQ, SB_HEAD_DIM ** -0.5 * np.log2(np.e))
        put(P_DK, _O_DK)
        put(P_DV, _O_DV)

    xb = _rms(x_ref[...], _row(gpre_ref, layer)).astype(bf16)
    xn_ref[...] = xb

    def proj(lo, hi):
        return _dot(xb, wbf_ref[:, lo:hi])

    aq_ref[...] = proj(_O_AQ, _O_AK).astype(bf16)
    akv = proj(_O_AK, _O_BB)
    for ref, blk in ((ak_ref, akv[:, 0:LANES]), (av_ref, akv[:, LANES:2 * LANES])):
        ref[:, 0:LANES] = blk.astype(bf16)
        ref[:, LANES:2 * LANES] = pltpu.roll(blk, SWA_HEAD_DIM, axis=1).astype(bf16)

    dq_ref[...] = proj(P_DQ, P_DK).astype(bf16)
    dk_ref[...] = proj(P_DK, P_DV).astype(bf16)
    dv_ref[...] = proj(P_DV, P_END).astype(bf16)

    ckr = proj(_O_CKV, P_KR_END)

    u = proj(_O_BC, _O_BX) * proj(_O_BX, _O_CQ)

    @pl.when(pl.program_id(0) % tiles_per_seq == 0)
    def _():
        ubuf_ref[0:8, :] = jnp.zeros((8, CONV_WIDTH), jnp.float32)

    ubuf_ref[8:tm + 8, :] = u
    u1 = ubuf_ref[7:tm + 7, :]
    u2 = ubuf_ref[6:tm + 6, :]
    conv = (convw_ref[0:1, :] * u2 + convw_ref[1:2, :] * u1 + convw_ref[2:3, :] * u
            + _row(convb_ref, layer))
    yb_ref[...] = proj(_O_BB, _O_BC) * conv
    ubuf_ref[0:8, :] = ubuf_ref[tm:tm + 8, :]

    cosv = cos_ref[...]
    sinv = sin_ref[...]
    lane = lax.broadcasted_iota(jnp.int32, (tm, LANES), 1)
    in_rope = lane < MLA_ROPE_DIM
    qscale = (MLA_NOPE_DIM + MLA_ROPE_DIM) ** -0.5 * np.log2(np.e)

    cq = _rms(proj(_O_CQ, _O_CKV), _row(gcq_ref, layer)).astype(bf16)
    qnope = _dot(cq, wuqn_ref[...]).astype(bf16)
    qlat = _dot(qnope, wabs_ref[...])
    qr = _dot(cq, wuqr_ref[...])
    qroped = (qr[:, 0:LANES] * cosv + qr[:, LANES:2 * LANES] * sinv) * qscale
    for h in range(MLA_HEADS):
        lat_h = (qlat[:, h * LANES:(h + 1) * LANES] * qscale).astype(bf16)
        mine = qroped if h == 0 else pltpu.roll(qroped, LANES - h * MLA_ROPE_DIM, axis=1)
        rope_h = jnp.where(in_rope, mine, 0.0).astype(bf16)
        for t in range(tm // MLA_TQ):
            src = slice(t * MLA_TQ, (t + 1) * MLA_TQ)
            dst = slice((t * MLA_HEADS + h) * MLA_TQ, (t * MLA_HEADS + h + 1) * MLA_TQ)
            qa_ref[dst, 0:LANES] = lat_h[src]
            qa_ref[dst, LANES:2 * LANES] = rope_h[src]
    kv_ref[:, 0:LANES] = _rms(ckr[:, 0:LANES], _row(gckv_ref, layer)).astype(bf16)
    kr = ckr[:, LANES:2 * LANES]
    half = MLA_ROPE_DIM // 2
    partner = jnp.where(lane < half, -pltpu.roll(kr, LANES - half, axis=1),
                        pltpu.roll(kr, half, axis=1))
    krope = jnp.where(in_rope, kr * cosv + partner * sinv, 0.0)
    kv_ref[:, LANES:2 * LANES] = jnp.where(lane == ONES_LANE - LANES, 1.0, krope).astype(bf16)


def _proj_call(x2, rope, pw, l, seq):
    n = x2.shape[0]
    tm = PROJ_ROWS
    row = lambda w: pl.BlockSpec((tm, w), lambda i: (i, 0))
    bf16 = jnp.bfloat16
    outs = [("xn", 1, D_MODEL, bf16), ("aq", 1, 256, bf16), ("ak", 1, 256, bf16),
            ("av", 1, 256, bf16),
            ("yb", 1, 256, jnp.float32), ("qa", MLA_HEADS, LAT_WIDTH, bf16),
            ("kv", 1, LAT_WIDTH, bf16), ("dq", 1, 256, bf16), ("dk", 1, 256, bf16),
            ("dv", 1, 256, bf16)]
    params = [pw[k] for k in ("gpre", "win", "convw", "convb", "gcq", "wuqn", "wabs", "wuqr",
                              "gckv")]
    res = pl.pallas_call(
        functools.partial(_proj_kernel, layer=l, tiles_per_seq=seq // tm),
        grid=(n // tm,),
        in_specs=[row(D_MODEL), row(LANES), row(LANES)]
        + [_layer_spec(a, l, single=(k == "win")) for k, a in zip(
            ("gpre", "win", "convw", "convb", "gcq", "wuqn", "wabs", "wuqr", "gckv"), params)],
        out_specs=[pl.BlockSpec((r * tm, w), lambda i: (i, 0)) for _, r, w, _ in outs],
        out_shape=[jax.ShapeDtypeStruct((r * n, w), dt) for _, r, w, dt in outs],
        scratch_shapes=[pltpu.VMEM((D_MODEL, P_END), bf16),
                        pltpu.VMEM((tm + 8, CONV_WIDTH), jnp.float32)],
        compiler_params=pltpu.CompilerParams(dimension_semantics=("arbitrary",),
                                             vmem_limit_bytes=VMEM_LIMIT),
        name="proj",
    )(x2, *rope, *params)
    return {name: r for (name, _, _, _), r in zip(outs, res)}


def _swa_tile(sink_ref, q_ref, k_ref, kp_ref, v_ref, vp_ref, layer, first_tile):
    bf16 = jnp.bfloat16
    nsub = q_ref.shape[0] // BLOCK
    out = []
    row = lax.broadcasted_iota(jnp.int32, (BLOCK, 2 * BLOCK), 0)
    col = lax.broadcasted_iota(jnp.int32, (BLOCK, 2 * BLOCK), 1)
    band = (col > row) & (col <= row + BLOCK)
    band_first = band & ((col >= BLOCK) | jnp.logical_not(first_tile))
    upper = lax.broadcasted_iota(jnp.int32, (BLOCK, LANES), 1) >= SWA_HEAD_DIM
    for j in range(nsub):
        rows = slice(j * BLOCK, (j + 1) * BLOCK)
        if j == 0:
            kk = jnp.concatenate([kp_ref[...], k_ref[0:BLOCK, :]], axis=0)
            vv = jnp.concatenate([vp_ref[...], v_ref[0:BLOCK, :]], axis=0)
            mask = band_first
        else:
            kk = k_ref[(j - 1) * BLOCK:(j + 1) * BLOCK, :]
            vv = v_ref[(j - 1) * BLOCK:(j + 1) * BLOCK, :]
            mask = band
        blocks = []
        for lb in range(2):
            q2 = q_ref[rows, lb * LANES:(lb + 1) * LANES]
            halves = []
            for hh in range(2):
                sink = sink_ref[layer, 2 * lb + hh] * np.log2(np.e)
                qm = jnp.where(upper == (hh == 1), q2, jnp.zeros_like(q2))
                sel = slice(0, LANES) if lb == hh else slice(LANES, 2 * LANES)
                s = jnp.where(mask, _dot_nt(qm, kk[:, sel]), NEG_BIG)
                m = jnp.maximum(jnp.max(s, axis=1, keepdims=True), sink)
                p = jnp.exp2(s - m)
                den = jnp.sum(p, axis=1, keepdims=True) + jnp.exp2(sink - m)
                halves.append(_dot(p.astype(bf16), vv[:, sel]) / den)
            blocks.append(jnp.where(upper, halves[1], halves[0]))
        out.append(jnp.concatenate(blocks, axis=1))
    return jnp.concatenate(out, axis=0)


def _attn_kernel(q_ref, kv_ref, wuv_ref, dq_ref, dk_ref, dv_ref, tri_ref, o_ref, od_ref, m_ref,
                 acc_ref):
    tq, tk = MLA_TQ, MLA_TK
    bf16 = jnp.bfloat16
    rows = MLA_HEADS * tq
    i = pl.program_id(1)
    q = q_ref[...]

    def exact_update(off, masked):
        kv = kv_ref[pl.ds(off, tk), :]
        s = _dot_nt(q, kv)
        if masked:
            row = lax.broadcasted_iota(jnp.int32, (rows, tk), 0) & (tq - 1)
            col = lax.broadcasted_iota(jnp.int32, (rows, tk), 1)
            s = jnp.where(col <= row, s, NEG_BIG)
        m = m_ref[:, 0:1]
        m_new = jnp.maximum(m, jnp.max(s, axis=1, keepdims=True))
        p = jnp.exp2(s - m_new)
        acc_ref[...] = jnp.exp2(m - m_new) * acc_ref[...] + _dot(p.astype(bf16), kv)
        m_ref[...] = jnp.broadcast_to(m_new, (rows, LANES))

    def lazy_update(off, width):
        kv = kv_ref[pl.ds(off, width), :]
        d = _dot_nt(q, kv) - jnp.concatenate([m_ref[...]] * (width // LANES), axis=1)
        acc_new = acc_ref[...] + _dot(jnp.exp2(d).astype(bf16), kv)
        safe = jnp.max(d) <= MLA_JUMP

        @pl.when(safe)
        def _():
            acc_ref[...] = acc_new

        @pl.when(jnp.logical_not(safe))
        def _():
            def redo(j, carry):
                exact_update(pl.multiple_of(off + j * tk, tk), False)
                return carry
            lax.fori_loop(0, width // tk, redo, 0)

    off_d = pl.multiple_of(i * tk, tk)
    kv_d = kv_ref[pl.ds(off_d, tk), :]
    row = lax.broadcasted_iota(jnp.int32, (rows, tk), 0) & (tq - 1)
    col = lax.broadcasted_iota(jnp.int32, (rows, tk), 1)
    d0 = jnp.where(col <= row, _dot_nt(q, kv_d), NEG_BIG)
    acc0 = _dot(jnp.exp2(d0).astype(bf16), kv_d)
    sums = acc0[:, LANES:2 * LANES]
    ones_lane = lax.broadcasted_iota(jnp.int32, (rows, LANES), 1) == ONES_LANE - LANES
    smallest = jnp.min(jnp.where(ones_lane, sums, 1.0))
    fine = jnp.logical_and(jnp.max(d0) <= MLA_JUMP, smallest >= MLA_MIN_SUM)

    sb_tiles = [_sb_open(dq_ref, j, dk_ref, dv_ref, tri_ref, (tq // SB_TQ) * i + j)
                for j in range(tq // SB_TQ)]

    @pl.when(fine)
    def _():
        m_ref[...] = jnp.zeros((rows, LANES), jnp.float32)
        acc_ref[...] = acc0

    @pl.when(jnp.logical_not(fine))
    def _():
        m_ref[...] = jnp.full((rows, LANES), NEG_BIG, jnp.float32)
        acc_ref[...] = jnp.zeros((rows, LAT_WIDTH), jnp.float32)
        exact_update(off_d, True)

    for j, opened in enumerate(sb_tiles):
        _sb_close(opened, (tq // SB_TQ) * i + j, od_ref, j)

    def quad(c, carry):
        lazy_update(pl.multiple_of(c * 4 * tk, tk), 4 * tk)
        return carry

    lax.fori_loop(0, i // 4, quad, 0)

    @pl.when(i % 4 >= 2)
    def _():
        lazy_update(pl.multiple_of((i // 4) * 4 * tk, tk), 2 * tk)

    @pl.when(i % 2 == 1)
    def _():
        lazy_update(pl.multiple_of((i - 1) * tk, tk), tk)

    acc = acc_ref[...]
    lat = (acc[:, 0:LANES] / acc[:, ONES_LANE:ONES_LANE + 1]).astype(bf16)
    heads_on_lanes = jnp.concatenate([lat[h * tq:(h + 1) * tq, :] for h in range(MLA_HEADS)], axis=1)
    o_ref[...] = _dot(heads_on_lanes, wuv_ref[...].reshape(MLA_HEADS * MLA_KV_RANK, -1))


def _sb_open(q_ref, j, k_ref, v_ref, tri_ref, i):
    tq, tk = SB_TQ, SB_TK
    bf16 = jnp.bfloat16
    upper_q = lax.broadcasted_iota(jnp.int32, (tq, LANES), 1) >= SB_HEAD_DIM
    qs = []
    for lb in range(2):
        q2 = q_ref[j * tq:(j + 1) * tq, lb * LANES:(lb + 1) * LANES]
        zero = jnp.zeros_like(q2)
        qs.append(jnp.concatenate([jnp.where(upper_q, zero, q2), jnp.where(upper_q, q2, zero)],
                                  axis=0))
    row = lax.broadcasted_iota(jnp.int32, (2 * tq, tk), 0) & (tq - 1)
    col = lax.broadcasted_iota(jnp.int32, (2 * tq, tk), 1)
    strict = col < row
    tri = tri_ref[...]

    def step(c, state, masked):
        off = pl.multiple_of(c * tk, tk)
        new = []
        for lb in range(2):
            csum, acc = state[lb]
            k2 = k_ref[pl.ds(off, tk), lb * LANES:(lb + 1) * LANES]
            v2 = v_ref[pl.ds(off, tk), lb * LANES:(lb + 1) * LANES]
            z = _dot_nt(qs[lb], k2)
            if masked:
                z = jnp.where(strict, z, NEG_BIG)
            log_beta = jnp.minimum(z, 0.0) - jnp.log2(1.0 + jnp.exp2(-jnp.abs(z)))
            log_keep = log_beta - z
            hi = log_keep.astype(bf16)
            lo = (log_keep - hi.astype(jnp.float32)).astype(bf16)
            within = _dot(jnp.concatenate([hi, lo], axis=1), tri)
            a = jnp.exp2(log_beta + (csum + within))
            acc = acc + _dot(a.astype(bf16), v2)
            csum = csum + jnp.sum(log_keep, axis=1, keepdims=True)
            new.append((csum, acc))
        return tuple(new)

    init = tuple((jnp.zeros((2 * tq, 1), jnp.float32), jnp.zeros((2 * tq, LANES), jnp.float32))
                 for _ in range(2))
    state = step(i, init, True)
    gone = jnp.where(i > 0, 0.0, NEG_BIG)
    state = step(jnp.maximum(i - 1, 0), tuple((csum + gone, acc) for csum, acc in state), False)
    return step, state


def _sb_close(opened, i, o_ref, j):
    tq = SB_TQ
    step, state = opened

    def least_decayed(st):
        return jnp.max(jnp.maximum(st[0][0], st[1][0]))

    def cond(carry):
        t, worst, _ = carry
        return jnp.logical_and(t < i - 1, worst > SB_DEAD)

    def body(carry):
        t, _, st = carry
        st = step(i - 2 - t, st, False)
        return t + 1, least_decayed(st), st

    _, _, state = lax.while_loop(cond, body, (jnp.int32(0), least_decayed(state), state))
    upper_q = lax.broadcasted_iota(jnp.int32, (tq, LANES), 1) >= SB_HEAD_DIM
    for lb in range(2):
        acc = state[lb][1]
        o_ref[j * tq:(j + 1) * tq, lb * LANES:(lb + 1) * LANES] = jnp.where(
            upper_q, acc[tq:2 * tq], acc[0:tq])


def _attn_call(qa, kv, dq, dk, dv, tri, pw, l, batch, seq):
    tq = MLA_TQ
    assert tq & (tq - 1) == 0 and PROJ_ROWS % tq == 0 and tq % SB_TQ == 0
    nq = seq // tq
    rows = MLA_HEADS * tq
    per_seq = lambda a, w: a.reshape(batch, seq, w)
    whole = lambda w: pl.BlockSpec((None, seq, w), lambda b, i: (b, 0, 0))
    tile = lambda w: pl.BlockSpec((None, tq, w), lambda b, i: (b, i, 0))
    out_mla, out_sb = pl.pallas_call(
        _attn_kernel,
        grid=(batch, nq),
        in_specs=[pl.BlockSpec((rows, LAT_WIDTH), lambda b, i: (b * nq + i, 0)),
                  whole(LAT_WIDTH), _layer_spec(pw["wuv"], l),
                  tile(256), whole(256), whole(256),
                  pl.BlockSpec((2 * SB_TK, SB_TK), lambda b, i: (0, 0))],
        out_specs=[tile(256), tile(256)],
        out_shape=[jax.ShapeDtypeStruct((batch, seq, 256), jnp.float32)] * 2,
        scratch_shapes=[pltpu.VMEM((rows, LANES), jnp.float32),
                        pltpu.VMEM((rows, LAT_WIDTH), jnp.float32)],
        compiler_params=pltpu.CompilerParams(
            dimension_semantics=("arbitrary", "arbitrary"),
            vmem_limit_bytes=VMEM_LIMIT),
        name="attn",
    )(qa, per_seq(kv, LAT_WIDTH), pw["wuv"], per_seq(dq, 256), per_seq(dk, 256),
      per_seq(dv, 256), tri)
    return out_mla.reshape(batch * seq, 256), out_sb.reshape(batch * seq, 256)


def _out_kernel(sink_ref, x_ref, xn_ref, aq_ref, ak_ref, akp_ref, av_ref, avp_ref, yb_ref, yc_ref,
                yd_ref, win_ref, ggrp_ref, wout_ref, gpost_ref, o_ref, wgate_ref, woutb_ref, *,
                layer, tiles_per_seq):
    bf16 = jnp.bfloat16

    @pl.when(pl.program_id(0) == 0)
    def _():
        for c in range(0, D_MIX, 256):
            lo = _O_GATE - GATE_SLAB + c
            wgate_ref[:, c:c + 256] = win_ref[lo:lo + 256, :].T.astype(bf16)
        woutb_ref[...] = wout_ref[...].astype(bf16)

    first_tile = (pl.program_id(0) % tiles_per_seq) == 0
    ya = _swa_tile(sink_ref, aq_ref, ak_ref, akp_ref, av_ref, avp_ref, layer, first_tile)
    gates = _dot(xn_ref[...], wgate_ref[...])
    ys = []
    for g, y_in in enumerate((ya, yb_ref, yc_ref, yd_ref)):
        sl = slice(g * GROUP_WIDTH, (g + 1) * GROUP_WIDTH)
        gate = gates[:, sl]
        y = _rms(y_in[...], ggrp_ref[layer:layer + 1, sl]) * (gate * jax.nn.sigmoid(gate))
        ys.append(y.astype(bf16))
    d = _dot(jnp.concatenate(ys, axis=1), woutb_ref[...])
    o_ref[...] = x_ref[...] + _rms(d, _row(gpost_ref, layer))


def _out_call(x2, xn, sinks, aq, ak, av, yb, yc, yd, pw, l, seq):
    n = x2.shape[0]
    tm = OUT_ROWS
    per = tm // BLOCK
    row = lambda w: pl.BlockSpec((tm, w), lambda i: (i, 0))
    prev = lambda w: pl.BlockSpec((BLOCK, w), lambda i: (jnp.maximum(i * per - 1, 0), 0))
    names = ("win", "ggrp", "wout", "gpost")
    params = [pw[k] for k in names]
    gate_slab = pl.BlockSpec((None, D_IN - GATE_SLAB, D_MODEL), lambda i: (l, 1, 0),
                             pipeline_mode=pl.Buffered(1))
    return pl.pallas_call(
        functools.partial(_out_kernel, layer=l, tiles_per_seq=seq // tm),
        grid=(n // tm,),
        in_specs=[pl.BlockSpec(memory_space=pltpu.SMEM), row(D_MODEL), row(D_MODEL), row(256),
                  row(256), prev(256), row(256), prev(256), row(256), row(256), row(256)]
        + [gate_slab if k == "win" else _layer_spec(a, l, single=(k == "wout"))
           for k, a in zip(names, params)],
        out_specs=row(D_MODEL),
        out_shape=jax.ShapeDtypeStruct((n, D_MODEL), jnp.float32),
        scratch_shapes=[pltpu.VMEM((D_MODEL, D_MIX), jnp.bfloat16),
                        pltpu.VMEM((D_MIX, D_MODEL), jnp.bfloat16)],
        compiler_params=pltpu.CompilerParams(dimension_semantics=("arbitrary",),
                                             vmem_limit_bytes=VMEM_LIMIT),
        name="out",
    )(sinks, x2, xn, aq, ak, ak, av, av, yb, yc, yd, *params)


def _rope_swap(w):
    half = MLA_ROPE_DIM // 2
    return jnp.concatenate([-w[..., half:], w[..., :half]], axis=-1)


def _prep_weights(norm_pre, w_in, conv_w, conv_b, mla_q_norm, mla_w_uq, mla_kv_norm, mla_w_ukv,
                  group_norm, w_out, norm_post):
    f32, bf16 = jnp.float32, jnp.bfloat16
    depth = w_in.shape[0]
    dqk = MLA_NOPE_DIM + MLA_ROPE_DIM
    uq = mla_w_uq.reshape(depth, MLA_Q_RANK, MLA_HEADS, dqk)
    wuqn = uq[..., :MLA_NOPE_DIM].reshape(depth, MLA_Q_RANK, -1).astype(bf16)
    wuqr = jnp.concatenate(
        [uq[..., MLA_NOPE_DIM:].reshape(depth, MLA_Q_RANK, -1),
         _rope_swap(uq[..., MLA_NOPE_DIM:]).reshape(depth, MLA_Q_RANK, -1)], axis=-1).astype(bf16)
    ukv = mla_w_ukv.reshape(depth, MLA_KV_RANK, MLA_HEADS, MLA_NOPE_DIM + MLA_V_DIM)
    own_head = jnp.eye(MLA_HEADS, dtype=f32)[None, :, None, :, None]
    uk_t = jnp.transpose(ukv[..., :MLA_NOPE_DIM], (0, 2, 3, 1))
    wabs = (uk_t[:, :, :, None, :] * own_head).reshape(
        depth, MLA_HEADS * MLA_NOPE_DIM, MLA_HEADS * MLA_KV_RANK).astype(bf16)
    uv = jnp.transpose(ukv[..., MLA_NOPE_DIM:], (0, 2, 1, 3))
    wuv = (uv[:, :, :, None, :] * own_head).reshape(
        depth, MLA_HEADS, MLA_KV_RANK, MLA_HEADS * MLA_V_DIM).astype(bf16)

    return dict(
        gpre=norm_pre, win=jnp.swapaxes(w_in, 1, 2), convw=conv_w, convb=conv_b,
        gcq=mla_q_norm, wuqn=wuqn, wabs=wabs, wuqr=wuqr, gckv=mla_kv_norm, wuv=wuv,
        ggrp=group_norm, wout=w_out, gpost=norm_post)


def kernel(x, positions, norm_pre, w_in, attn_sinks, conv_w, conv_b, mla_q_norm, mla_w_uq,
           mla_kv_norm, mla_w_ukv, group_norm, w_out, norm_post):
    batch, seq, _ = x.shape
    depth = w_in.shape[0]
    assert seq % max(PROJ_ROWS, MLA_TQ, SB_TQ, OUT_ROWS) == 0 and OUT_ROWS % BLOCK == 0
    assert MLA_TQ == MLA_TK and SB_TQ == SB_TK and SB_TQ & (SB_TQ - 1) == 0
    n = batch * seq
    x2 = x.reshape(n, D_MODEL)
    tri = jnp.asarray(np.tile(np.tril(np.ones((SB_TK, SB_TK), np.float32), -1), (2, 1)),
                      jnp.bfloat16)
    pw = _prep_weights(norm_pre, w_in, conv_w, conv_b, mla_q_norm, mla_w_uq, mla_kv_norm,
                       mla_w_ukv, group_norm, w_out, norm_post)
    rope = _rope_call(positions)
    for l in range(depth):
        p = _proj_call(x2, rope, pw, l, seq)
        yc, yd = _attn_call(p["qa"], p["kv"], p["dq"], p["dk"], p["dv"], tri, pw, l, batch, seq)
        x2 = _out_call(x2, p["xn"], attn_sinks, p["aq"], p["ak"], p["av"], p["yb"], yc, yd, pw, l,
                       seq)
    return x2.reshape(batch, seq, D_MODEL)
```

```python
import functools

import numpy as np
import jax
import jax.numpy as jnp
from jax import lax
from jax.experimental import pallas as pl
from jax.experimental.pallas import tpu as pltpu

D_MODEL = 1024
BLOCK = 128
NORM_EPS = 1e-6
SWA_HEAD_DIM = 64
CONV_WIDTH = 256
MLA_HEADS = 4
MLA_Q_RANK = 256
MLA_KV_RANK = 128
MLA_NOPE_DIM = 64
MLA_ROPE_DIM = 32
MLA_V_DIM = 64
ROPE_THETA = 10000.0
SB_HEAD_DIM = 64
GROUP_WIDTH = 256
N_GROUPS = 4
D_MIX = GROUP_WIDTH * N_GROUPS

LANES = 128
LAT_WIDTH = 256
ONES_LANE = 160
MLA_JUMP = 32.0
MLA_MIN_SUM = 2.0 ** -64
NEG_BIG = -1e30
SB_DEAD = -160.0

_IN_SIZES = (256, 128, 128, 256, 256, 256, 256, 128, 32, 256, 256, 256, 1024)
_IN_OFF = np.concatenate([[0], np.cumsum(_IN_SIZES)]).astype(int)
(_O_AQ, _O_AK, _O_AV, _O_BB, _O_BC, _O_BX, _O_CQ, _O_CKV, _O_CKR, _O_DQ, _O_DK, _O_DV, _O_GATE,
 _O_END) = [int(v) for v in _IN_OFF]

D_IN = _O_END
GATE_SLAB = D_IN // 2
assert D_IN % 2 == 0 and GATE_SLAB % 8 == 0 and GATE_SLAB <= _O_GATE
P_KR_END = _O_CKR + LANES
P_DQ, P_DK, P_DV, P_END = P_KR_END, P_KR_END + 256, P_KR_END + 512, P_KR_END + 768

ROPE_ROWS = 1024
PROJ_ROWS = 1024
MLA_TQ = 512
MLA_TK = 512
SB_TQ = 256
SB_TK = 256
OUT_ROWS = 1024
VMEM_LIMIT = 56 * 1024 * 1024


def _rms(v, g):
    return v * lax.rsqrt(jnp.mean(v * v, axis=-1, keepdims=True) + NORM_EPS) * g


def _dot(a, b):
    return jnp.dot(a, b, preferred_element_type=jnp.float32)


def _dot_nt(a, b):
    return lax.dot_general(a, b, (((1,), (1,)), ((), ())), preferred_element_type=jnp.float32)


def _layer_spec(a, l, single=False):
    if a.ndim == 2:
        return pl.BlockSpec(a.shape, lambda *_: (0, 0))
    zeros = (0,) * (a.ndim - 1)
    mode = dict(pipeline_mode=pl.Buffered(1)) if single else {}
    return pl.BlockSpec((None,) + a.shape[1:], lambda *_: (l,) + zeros, **mode)


def _row(ref, layer):
    return ref[layer:layer + 1, :]


def _rope_kernel(pos_ref, freq_ref, cos_ref, sin_ref):
    ang = freq_ref[...] * pos_ref[...]
    copies = LANES // ang.shape[0]
    for ref, table in ((cos_ref, jnp.cos(ang)), (sin_ref, jnp.sin(ang))):
        ref[...] = jnp.concatenate([table] * copies, axis=0).T


def _rope_call(positions):
    n = positions.size
    tm = ROPE_ROWS
    half = MLA_ROPE_DIM // 2
    freqs = (ROPE_THETA ** (-np.arange(half, dtype=np.float32) / half)).astype(np.float32)
    table = jax.ShapeDtypeStruct((n, LANES), jnp.float32)
    return pl.pallas_call(
        _rope_kernel,
        grid=(n // tm,),
        in_specs=[pl.BlockSpec((None, 1, tm), lambda i: (i, 0, 0)),
                  pl.BlockSpec((half, 1), lambda i: (0, 0))],
        out_specs=[pl.BlockSpec((tm, LANES), lambda i: (i, 0))] * 2,
        out_shape=[table, table],
        compiler_params=pltpu.CompilerParams(dimension_semantics=("arbitrary",)),
        name="rope",
    )(positions.astype(jnp.float32).reshape(n // tm, 1, tm), jnp.asarray(freqs.reshape(half, 1)))


def _proj_kernel(x_ref, cos_ref, sin_ref, gpre_ref, win_ref, convw_ref, convb_ref, gcq_ref,
                 wuqn_ref, wabs_ref, wuqr_ref, gckv_ref,
                 xn_ref, aq_ref, ak_ref, av_ref, yb_ref, qa_ref, kv_ref, dq_ref, dk_ref, dv_ref,
                 wbf_ref, ubuf_ref, *, layer, tiles_per_seq):
    tm = x_ref.shape[0]
    bf16 = jnp.bfloat16

    @pl.when(pl.program_id(0) == 0)
    def _():
        def put(dst, src, scale=None):
            blk = win_ref[src:src + 256, :].T
            blk = blk if scale is None else blk * scale
            wbf_ref[:, dst:dst + 256] = blk.astype(bf16)

        put(_O_AQ, _O_AQ, SWA_HEAD_DIM ** -0.5 * np.log2(np.e))
        for c in range(_O_AK, P_KR_END, 256):
            put(c, c)
        put(P_DQ, _O_DQ, SB_HEAD_DIM ** -0.5 * np.log2(np.e))
        put(P_DK, _O_DK)
        put(P_DV, _O_DV)

    xb = _rms(x_ref[...], _row(gpre_ref, layer)).astype(bf16)
    xn_ref[...] = xb

    def proj(lo, hi):
        return _dot(xb, wbf_ref[:, lo:hi])

    aq_ref[...] = proj(_O_AQ, _O_AK).astype(bf16)
    akv = proj(_O_AK, _O_BB)
    for ref, blk in ((ak_ref, akv[:, 0:LANES]), (av_ref, akv[:, LANES:2 * LANES])):
        ref[:, 0:LANES] = blk.astype(bf16)
        ref[:, LANES:2 * LANES] = pltpu.roll(blk, SWA_HEAD_DIM, axis=1).astype(bf16)

    dq_ref[...] = proj(P_DQ, P_DK).astype(bf16)
    dk_ref[...] = proj(P_DK, P_DV).astype(bf16)
    dv_ref[...] = proj(P_DV, P_END).astype(bf16)

    ckr = proj(_O_CKV, P_KR_END)

    u = proj(_O_BC, _O_BX) * proj(_O_BX, _O_CQ)

    @pl.when(pl.program_id(0) % tiles_per_seq == 0)
    def _():
        ubuf_ref[0:8, :] = jnp.zeros((8, CONV_WIDTH), jnp.float32)

    ubuf_ref[8:tm + 8, :] = u
    u1 = ubuf_ref[7:tm + 7, :]
    u2 = ubuf_ref[6:tm + 6, :]
    conv = (convw_ref[0:1, :] * u2 + convw_ref[1:2, :] * u1 + convw_ref[2:3, :] * u
            + _row(convb_ref, layer))
    yb_ref[...] = proj(_O_BB, _O_BC) * conv
    ubuf_ref[0:8, :] = ubuf_ref[tm:tm + 8, :]

    cosv = cos_ref[...]
    sinv = sin_ref[...]
    lane = lax.broadcasted_iota(jnp.int32, (tm, LANES), 1)
    in_rope = lane < MLA_ROPE_DIM
    qscale = (MLA_NOPE_DIM + MLA_ROPE_DIM) ** -0.5 * np.log2(np.e)

    cq = _rms(proj(_O_CQ, _O_CKV), _row(gcq_ref, layer)).astype(bf16)
    qnope = _dot(cq, wuqn_ref[...]).astype(bf16)
    qlat = _dot(qnope, wabs_ref[...])
    qr = _dot(cq, wuqr_ref[...])
    qroped = (qr[:, 0:LANES] * cosv + qr[:, LANES:2 * LANES] * sinv) * qscale
    for h in range(MLA_HEADS):
        lat_h = (qlat[:, h * LANES:(h + 1) * LANES] * qscale).astype(bf16)
        mine = qroped if h == 0 else pltpu.roll(qroped, LANES - h * MLA_ROPE_DIM, axis=1)
        rope_h = jnp.where(in_rope, mine, 0.0).astype(bf16)
        for t in range(tm // MLA_TQ):
            src = slice(t * MLA_TQ, (t + 1) * MLA_TQ)
            dst = slice((t * MLA_HEADS + h) * MLA_TQ, (t * MLA_HEADS + h + 1) * MLA_TQ)
            qa_ref[dst, 0:LANES] = lat_h[src]
            qa_ref[dst, LANES:2 * LANES] = rope_h[src]
    kv_ref[:, 0:LANES] = _rms(ckr[:, 0:LANES], _row(gckv_ref, layer)).astype(bf16)
    kr = ckr[:, LANES:2 * LANES]
    half = MLA_ROPE_DIM // 2
    partner = jnp.where(lane < half, -pltpu.roll(kr, LANES - half, axis=1),
                        pltpu.roll(kr, half, axis=1))
    krope = jnp.where(in_rope, kr * cosv + partner * sinv, 0.0)
    kv_ref[:, LANES:2 * LANES] = jnp.where(lane == ONES_LANE - LANES, 1.0, krope).astype(bf16)


def _proj_call(x2, rope, pw, l, seq):
    n = x2.shape[0]
    tm = PROJ_ROWS
    row = lambda w: pl.BlockSpec((tm, w), lambda i: (i, 0))
    bf16 = jnp.bfloat16
    outs = [("xn", 1, D_MODEL, bf16), ("aq", 1, 256, bf16), ("ak", 1, 256, bf16),
            ("av", 1, 256, bf16),
            ("yb", 1, 256, jnp.float32), ("qa", MLA_HEADS, LAT_WIDTH, bf16),
            ("kv", 1, LAT_WIDTH, bf16), ("dq", 1, 256, bf16), ("dk", 1, 256, bf16),
            ("dv", 1, 256, bf16)]
    params = [pw[k] for k in ("gpre", "win", "convw", "convb", "gcq", "wuqn", "wabs", "wuqr",
                              "gckv")]
    res = pl.pallas_call(
        functools.partial(_proj_kernel, layer=l, tiles_per_seq=seq // tm),
        grid=(n // tm,),
        in_specs=[row(D_MODEL), row(LANES), row(LANES)]
        + [_layer_spec(a, l, single=(k == "win")) for k, a in zip(
            ("gpre", "win", "convw", "convb", "gcq", "wuqn", "wabs", "wuqr", "gckv"), params)],
        out_specs=[pl.BlockSpec((r * tm, w), lambda i: (i, 0)) for _, r, w, _ in outs],
        out_shape=[jax.ShapeDtypeStruct((r * n, w), dt) for _, r, w, dt in outs],
        scratch_shapes=[pltpu.VMEM((D_MODEL, P_END), bf16),
                        pltpu.VMEM((tm + 8, CONV_WIDTH), jnp.float32)],
        compiler_params=pltpu.CompilerParams(dimension_semantics=("arbitrary",),
                                             vmem_limit_bytes=VMEM_LIMIT),
        name="proj",
    )(x2, *rope, *params)
    return {name: r for (name, _, _, _), r in zip(outs, res)}


def _swa_tile(sink_ref, q_ref, k_ref, kp_ref, v_ref, vp_ref, layer, first_tile):
    bf16 = jnp.bfloat16
    nsub = q_ref.shape[0] // BLOCK
    out = []
    row = lax.broadcasted_iota(jnp.int32, (BLOCK, 2 * BLOCK), 0)
    col = lax.broadcasted_iota(jnp.int32, (BLOCK, 2 * BLOCK), 1)
    band = (col > row) & (col <= row + BLOCK)
    band_first = band & ((col >= BLOCK) | jnp.logical_not(first_tile))
    upper = lax.broadcasted_iota(jnp.int32, (BLOCK, LANES), 1) >= SWA_HEAD_DIM
    for j in range(nsub):
        rows = slice(j * BLOCK, (j + 1) * BLOCK)
        if j == 0:
            kk = jnp.concatenate([kp_ref[...], k_ref[0:BLOCK, :]], axis=0)
            vv = jnp.concatenate([vp_ref[...], v_ref[0:BLOCK, :]], axis=0)
            mask = band_first
        else:
            kk = k_ref[(j - 1) * BLOCK:(j + 1) * BLOCK, :]
            vv = v_ref[(j - 1) * BLOCK:(j + 1) * BLOCK, :]
            mask = band
        blocks = []
        for lb in range(2):
            q2 = q_ref[rows, lb * LANES:(lb + 1) * LANES]
            halves = []
            for hh in range(2):
                sink = sink_ref[layer, 2 * lb + hh] * np.log2(np.e)
                qm = jnp.where(upper == (hh == 1), q2, jnp.zeros_like(q2))
                sel = slice(0, LANES) if lb == hh else slice(LANES, 2 * LANES)
                s = jnp.where(mask, _dot_nt(qm, kk[:, sel]), NEG_BIG)
                m = jnp.maximum(jnp.max(s, axis=1, keepdims=True), sink)
                p = jnp.exp2(s - m)
                den = jnp.sum(p, axis=1, keepdims=True) + jnp.exp2(sink - m)
                halves.append(_dot(p.astype(bf16), vv[:, sel]) / den)
            blocks.append(jnp.where(upper, halves[1], halves[0]))
        out.append(jnp.concatenate(blocks, axis=1))
    return jnp.concatenate(out, axis=0)


def _attn_kernel(q_ref, kv_ref, wuv_ref, dq_ref, dk_ref, dv_ref, tri_ref, o_ref, od_ref, m_ref,
                 acc_ref):
    tq, tk = MLA_TQ, MLA_TK
    bf16 = jnp.bfloat16
    rows = MLA_HEADS * tq
    i = pl.program_id(1)
    q = q_ref[...]

    def exact_update(off, masked):
        kv = kv_ref[pl.ds(off, tk), :]
        s = _dot_nt(q, kv)
        if masked:
            row = lax.broadcasted_iota(jnp.int32, (rows, tk), 0) & (tq - 1)
            col = lax.broadcasted_iota(jnp.int32, (rows, tk), 1)
            s = jnp.where(col <= row, s, NEG_BIG)
        m = m_ref[:, 0:1]
        m_new = jnp.maximum(m, jnp.max(s, axis=1, keepdims=True))
        p = jnp.exp2(s - m_new)
        acc_ref[...] = jnp.exp2(m - m_new) * acc_ref[...] + _dot(p.astype(bf16), kv)
        m_ref[...] = jnp.broadcast_to(m_new, (rows, LANES))

    def lazy_update(off, width):
        kv = kv_ref[pl.ds(off, width), :]
        d = _dot_nt(q, kv) - jnp.concatenate([m_ref[...]] * (width // LANES), axis=1)
        acc_new = acc_ref[...] + _dot(jnp.exp2(d).astype(bf16), kv)
        safe = jnp.max(d) <= MLA_JUMP

        @pl.when(safe)
        def _():
            acc_ref[...] = acc_new

        @pl.when(jnp.logical_not(safe))
        def _():
            def redo(j, carry):
                exact_update(pl.multiple_of(off + j * tk, tk), False)
                return carry
            lax.fori_loop(0, width // tk, redo, 0)

    off_d = pl.multiple_of(i * tk, tk)
    kv_d = kv_ref[pl.ds(off_d, tk), :]
    row = lax.broadcasted_iota(jnp.int32, (rows, tk), 0) & (tq - 1)
    col = lax.broadcasted_iota(jnp.int32, (rows, tk), 1)
    d0 = jnp.where(col <= row, _dot_nt(q, kv_d), NEG_BIG)
    acc0 = _dot(jnp.exp2(d0).astype(bf16), kv_d)
    sums = acc0[:, LANES:2 * LANES]
    ones_lane = lax.broadcasted_iota(jnp.int32, (rows, LANES), 1) == ONES_LANE - LANES
    smallest = jnp.min(jnp.where(ones_lane, sums, 1.0))
    fine = jnp.logical_and(jnp.max(d0) <= MLA_JUMP, smallest >= MLA_MIN_SUM)

    sb_tiles = [_sb_open(dq_ref, j, dk_ref, dv_ref, tri_ref, (tq // SB_TQ) * i + j)
                for j in range(tq // SB_TQ)]

    @pl.when(fine)
    def _():
        m_ref[...] = jnp.zeros((rows, LANES), jnp.float32)
        acc_ref[...] = acc0

    @pl.when(jnp.logical_not(fine))
    def _():
        m_ref[...] = jnp.full((rows, LANES), NEG_BIG, jnp.float32)
        acc_ref[...] = jnp.zeros((rows, LAT_WIDTH), jnp.float32)
        exact_update(off_d, True)

    for j, opened in enumerate(sb_tiles):
        _sb_close(opened, (tq // SB_TQ) * i + j, od_ref, j)

    def quad(c, carry):
        lazy_update(pl.multiple_of(c * 4 * tk, tk), 4 * tk)
        return carry

    lax.fori_loop(0, i // 4, quad, 0)

    @pl.when(i % 4 >= 2)
    def _():
        lazy_update(pl.multiple_of((i // 4) * 4 * tk, tk), 2 * tk)

    @pl.when(i % 2 == 1)
    def _():
        lazy_update(pl.multiple_of((i - 1) * tk, tk), tk)

    acc = acc_ref[...]
    lat = (acc[:, 0:LANES] / acc[:, ONES_LANE:ONES_LANE + 1]).astype(bf16)
    heads_on_lanes = jnp.concatenate([lat[h * tq:(h + 1) * tq, :] for h in range(MLA_HEADS)], axis=1)
    o_ref[...] = _dot(heads_on_lanes, wuv_ref[...].reshape(MLA_HEADS * MLA_KV_RANK, -1))


def _sb_open(q_ref, j, k_ref, v_ref, tri_ref, i):
    tq, tk = SB_TQ, SB_TK
    bf16 = jnp.bfloat16
    upper_q = lax.broadcasted_iota(jnp.int32, (tq, LANES), 1) >= SB_HEAD_DIM
    qs = []
    for lb in range(2):
        q2 = q_ref[j * tq:(j + 1) * tq, lb * LANES:(lb + 1) * LANES]
        zero = jnp.zeros_like(q2)
        qs.append(jnp.concatenate([jnp.where(upper_q, zero, q2), jnp.where(upper_q, q2, zero)],
                                  axis=0))
    row = lax.broadcasted_iota(jnp.int32, (2 * tq, tk), 0) & (tq - 1)
    col = lax.broadcasted_iota(jnp.int32, (2 * tq, tk), 1)
    strict = col < row
    tri = tri_ref[...]

    def step(c, state, masked):
        off = pl.multiple_of(c * tk, tk)
        new = []
        for lb in range(2):
            csum, acc = state[lb]
            k2 = k_ref[pl.ds(off, tk), lb * LANES:(lb + 1) * LANES]
            v2 = v_ref[pl.ds(off, tk), lb * LANES:(lb + 1) * LANES]
            z = _dot_nt(qs[lb], k2)
            if masked:
                z = jnp.where(strict, z, NEG_BIG)
            log_beta = jnp.minimum(z, 0.0) - jnp.log2(1.0 + jnp.exp2(-jnp.abs(z)))
            log_keep = log_beta - z
            hi = log_keep.astype(bf16)
            lo = (log_keep - hi.astype(jnp.float32)).astype(bf16)
            within = _dot(jnp.concatenate([hi, lo], axis=1), tri)
            a = jnp.exp2(log_beta + (csum + within))
            acc = acc + _dot(a.astype(bf16), v2)
            csum = csum + jnp.sum(log_keep, axis=1, keepdims=True)
            new.append((csum, acc))
        return tuple(new)

    init = tuple((jnp.zeros((2 * tq, 1), jnp.float32), jnp.zeros((2 * tq, LANES), jnp.float32))
                 for _ in range(2))
    state = step(i, init, True)
    gone = jnp.where(i > 0, 0.0, NEG_BIG)
    state = step(jnp.maximum(i - 1, 0), tuple((csum + gone, acc) for csum, acc in state), False)
    return step, state


def _sb_close(opened, i, o_ref, j):
    tq = SB_TQ
    step, state = opened

    def least_decayed(st):
        return jnp.max(jnp.maximum(st[0][0], st[1][0]))

    def cond(carry):
        t, worst, _ = carry
        return jnp.logical_and(t < i - 1, worst > SB_DEAD)

    def body(carry):
        t, _, st = carry
        st = step(i - 2 - t, st, False)
        return t + 1, least_decayed(st), st

    _, _, state = lax.while_loop(cond, body, (jnp.int32(0), least_decayed(state), state))
    upper_q = lax.broadcasted_iota(jnp.int32, (tq, LANES), 1) >= SB_HEAD_DIM
    for lb in range(2):
        acc = state[lb][1]
        o_ref[j * tq:(j + 1) * tq, lb * LANES:(lb + 1) * LANES] = jnp.where(
            upper_q, acc[tq:2 * tq], acc[0:tq])


def _attn_call(qa, kv, dq, dk, dv, tri, pw, l, batch, seq):
    tq = MLA_TQ
    assert tq & (tq - 1) == 0 and PROJ_ROWS % tq == 0 and tq % SB_TQ == 0
    nq = seq // tq
    rows = MLA_HEADS * tq
    per_seq = lambda a, w: a.reshape(batch, seq, w)
    whole = lambda w: pl.BlockSpec((None, seq, w), lambda b, i: (b, 0, 0))
    tile = lambda w: pl.BlockSpec((None, tq, w), lambda b, i: (b, i, 0))
    out_mla, out_sb = pl.pallas_call(
        _attn_kernel,
        grid=(batch, nq),
        in_specs=[pl.BlockSpec((rows, LAT_WIDTH), lambda b, i: (b * nq + i, 0)),
                  whole(LAT_WIDTH), _layer_spec(pw["wuv"], l),
                  tile(256), whole(256), whole(256),
                  pl.BlockSpec((2 * SB_TK, SB_TK), lambda b, i: (0, 0))],
        out_specs=[tile(256), tile(256)],
        out_shape=[jax.ShapeDtypeStruct((batch, seq, 256), jnp.float32)] * 2,
        scratch_shapes=[pltpu.VMEM((rows, LANES), jnp.float32),
                        pltpu.VMEM((rows, LAT_WIDTH), jnp.float32)],
        compiler_params=pltpu.CompilerParams(
            dimension_semantics=("arbitrary", "arbitrary"),
            vmem_limit_bytes=VMEM_LIMIT),
        name="attn",
    )(qa, per_seq(kv, LAT_WIDTH), pw["wuv"], per_seq(dq, 256), per_seq(dk, 256),
      per_seq(dv, 256), tri)
    return out_mla.reshape(batch * seq, 256), out_sb.reshape(batch * seq, 256)


def _out_kernel(sink_ref, x_ref, xn_ref, aq_ref, ak_ref, akp_ref, av_ref, avp_ref, yb_ref, yc_ref,
                yd_ref, win_ref, ggrp_ref, wout_ref, gpost_ref, o_ref, wgate_ref, woutb_ref, *,
                layer, tiles_per_seq):
    bf16 = jnp.bfloat16

    @pl.when(pl.program_id(0) == 0)
    def _():
        for c in range(0, D_MIX, 256):
            lo = _O_GATE - GATE_SLAB + c
            wgate_ref[:, c:c + 256] = win_ref[lo:lo + 256, :].T.astype(bf16)
        woutb_ref[...] = wout_ref[...].astype(bf16)

    first_tile = (pl.program_id(0) % tiles_per_seq) == 0
    ya = _swa_tile(sink_ref, aq_ref, ak_ref, akp_ref, av_ref, avp_ref, layer, first_tile)
    gates = _dot(xn_ref[...], wgate_ref[...])
    ys = []
    for g, y_in in enumerate((ya, yb_ref, yc_ref, yd_ref)):
        sl = slice(g * GROUP_WIDTH, (g + 1) * GROUP_WIDTH)
        gate = gates[:, sl]
        y = _rms(y_in[...], ggrp_ref[layer:layer + 1, sl]) * (gate * jax.nn.sigmoid(gate))
        ys.append(y.astype(bf16))
    d = _dot(jnp.concatenate(ys, axis=1), woutb_ref[...])
    o_ref[...] = x_ref[...] + _rms(d, _row(gpost_ref, layer))


def _out_call(x2, xn, sinks, aq, ak, av, yb, yc, yd, pw, l, seq):
    n = x2.shape[0]
    tm = OUT_ROWS
    per = tm // BLOCK
    row = lambda w: pl.BlockSpec((tm, w), lambda i: (i, 0))
    prev = lambda w: pl.BlockSpec((BLOCK, w), lambda i: (jnp.maximum(i * per - 1, 0), 0))
    names = ("win", "ggrp", "wout", "gpost")
    params = [pw[k] for k in names]
    gate_slab = pl.BlockSpec((None, D_IN - GATE_SLAB, D_MODEL), lambda i: (l, 1, 0),
                             pipeline_mode=pl.Buffered(1))
    return pl.pallas_call(
        functools.partial(_out_kernel, layer=l, tiles_per_seq=seq // tm),
        grid=(n // tm,),
        in_specs=[pl.BlockSpec(memory_space=pltpu.SMEM), row(D_MODEL), row(D_MODEL), row(256),
                  row(256), prev(256), row(256), prev(256), row(256), row(256), row(256)]
        + [gate_slab if k == "win" else _layer_spec(a, l, single=(k == "wout"))
           for k, a in zip(names, params)],
        out_specs=row(D_MODEL),
        out_shape=jax.ShapeDtypeStruct((n, D_MODEL), jnp.float32),
        scratch_shapes=[pltpu.VMEM((D_MODEL, D_MIX), jnp.bfloat16),
                        pltpu.VMEM((D_MIX, D_MODEL), jnp.bfloat16)],
        compiler_params=pltpu.CompilerParams(dimension_semantics=("arbitrary",),
                                             vmem_limit_bytes=VMEM_LIMIT),
        name="out",
    )(sinks, x2, xn, aq, ak, ak, av, av, yb, yc, yd, *params)


def _rope_swap(w):
    half = MLA_ROPE_DIM // 2
    return jnp.concatenate([-w[..., half:], w[..., :half]], axis=-1)


def _prep_weights(norm_pre, w_in, conv_w, conv_b, mla_q_norm, mla_w_uq, mla_kv_norm, mla_w_ukv,
                  group_norm, w_out, norm_post):
    f32, bf16 = jnp.float32, jnp.bfloat16
    depth = w_in.shape[0]
    dqk = MLA_NOPE_DIM + MLA_ROPE_DIM
    uq = mla_w_uq.reshape(depth, MLA_Q_RANK, MLA_HEADS, dqk)
    wuqn = uq[..., :MLA_NOPE_DIM].reshape(depth, MLA_Q_RANK, -1).astype(bf16)
    wuqr = jnp.concatenate(
        [uq[..., MLA_NOPE_DIM:].reshape(depth, MLA_Q_RANK, -1),
         _rope_swap(uq[..., MLA_NOPE_DIM:]).reshape(depth, MLA_Q_RANK, -1)], axis=-1).astype(bf16)
    ukv = mla_w_ukv.reshape(depth, MLA_KV_RANK, MLA_HEADS, MLA_NOPE_DIM + MLA_V_DIM)
    own_head = jnp.eye(MLA_HEADS, dtype=f32)[None, :, None, :, None]
    uk_t = jnp.transpose(ukv[..., :MLA_NOPE_DIM], (0, 2, 3, 1))
    wabs = (uk_t[:, :, :, None, :] * own_head).reshape(
        depth, MLA_HEADS * MLA_NOPE_DIM, MLA_HEADS * MLA_KV_RANK).astype(bf16)
    uv = jnp.transpose(ukv[..., MLA_NOPE_DIM:], (0, 2, 1, 3))
    wuv = (uv[:, :, :, None, :] * own_head).reshape(
        depth, MLA_HEADS, MLA_KV_RANK, MLA_HEADS * MLA_V_DIM).astype(bf16)

    return dict(
        gpre=norm_pre, win=jnp.swapaxes(w_in, 1, 2), convw=conv_w, convb=conv_b,
        gcq=mla_q_norm, wuqn=wuqn, wabs=wabs, wuqr=wuqr, gckv=mla_kv_norm, wuv=wuv,
        ggrp=group_norm, wout=w_out, gpost=norm_post)


def kernel(x, positions, norm_pre, w_in, attn_sinks, conv_w, conv_b, mla_q_norm, mla_w_uq,
           mla_kv_norm, mla_w_ukv, group_norm, w_out, norm_post):
    batch, seq, _ = x.shape
    depth = w_in.shape[0]
    assert seq % max(PROJ_ROWS, MLA_TQ, SB_TQ, OUT_ROWS) == 0 and OUT_ROWS % BLOCK == 0
    assert MLA_TQ == MLA_TK and SB_TQ == SB_TK and SB_TQ & (SB_TQ - 1) == 0
    n = batch * seq
    x2 = x.reshape(n, D_MODEL)
    tri = jnp.asarray(np.tile(np.tril(np.ones((SB_TK, SB_TK), np.float32), -1), (2, 1)),
                      jnp.bfloat16)
    pw = _prep_weights(norm_pre, w_in, conv_w, conv_b, mla_q_norm, mla_w_uq, mla_kv_norm,
                       mla_w_ukv, group_norm, w_out, norm_post)
    rope = _rope_call(positions)
    for l in range(depth):
        p = _proj_call(x2, rope, pw, l, seq)
        yc, yd = _attn_call(p["qa"], p["kv"], p["dq"], p["dk"], p["dv"], tri, pw, l, batch, seq)
        x2 = _out_call(x2, p["xn"], attn_sinks, p["aq"], p["ak"], p["av"], p["yb"], yc, yd, pw, l,
                       seq)
    return x2.reshape(batch, seq, D_MODEL)
```

```python
import functools

import numpy as np
import jax
import jax.numpy as jnp
from jax import lax
from jax.experimental import pallas as pl
from jax.experimental.pallas import tpu as pltpu

D_MODEL = 1024
BLOCK = 128
NORM_EPS = 1e-6
SWA_HEAD_DIM = 64
CONV_WIDTH = 256
MLA_HEADS = 4
MLA_Q_RANK = 256
MLA_KV_RANK = 128
MLA_NOPE_DIM = 64
MLA_ROPE_DIM = 32
MLA_V_DIM = 64
ROPE_THETA = 10000.0
SB_HEAD_DIM = 64
GROUP_WIDTH = 256
N_GROUPS = 4
D_MIX = GROUP_WIDTH * N_GROUPS

LANES = 128
LAT_WIDTH = 256
ONES_LANE = 160
MLA_JUMP = 32.0
MLA_MIN_SUM = 2.0 ** -64
NEG_BIG = -1e30
SB_DEAD = -160.0

_IN_SIZES = (256, 128, 128, 256, 256, 256, 256, 128, 32, 256, 256, 256, 1024)
_IN_OFF = np.concatenate([[0], np.cumsum(_IN_SIZES)]).astype(int)
(_O_AQ, _O_AK, _O_AV, _O_BB, _O_BC, _O_BX, _O_CQ, _O_CKV, _O_CKR, _O_DQ, _O_DK, _O_DV, _O_GATE,
 _O_END) = [int(v) for v in _IN_OFF]

D_IN = _O_END
GATE_SLAB = D_IN // 2
assert D_IN % 2 == 0 and GATE_SLAB % 8 == 0 and GATE_SLAB <= _O_GATE
P_KR_END = _O_CKR + LANES
P_DQ, P_DK, P_DV, P_END = P_KR_END, P_KR_END + 256, P_KR_END + 512, P_KR_END + 768

ROPE_ROWS = 1024
PROJ_ROWS = 1024
MLA_TQ = 512
MLA_TK = 512
SB_TQ = 256
SB_TK = 256
OUT_ROWS = 1024
VMEM_LIMIT = 56 * 1024 * 1024


def _rms(v, g):
    return v * lax.rsqrt(jnp.mean(v * v, axis=-1, keepdims=True) + NORM_EPS) * g


def _dot(a, b):
    return jnp.dot(a, b, preferred_element_type=jnp.float32)


def _dot_nt(a, b):
    return lax.dot_general(a, b, (((1,), (1,)), ((), ())), preferred_element_type=jnp.float32)


def _layer_spec(a, l, single=False):
    if a.ndim == 2:
        return pl.BlockSpec(a.shape, lambda *_: (0, 0))
    zeros = (0,) * (a.ndim - 1)
    mode = dict(pipeline_mode=pl.Buffered(1)) if single else {}
    return pl.BlockSpec((None,) + a.shape[1:], lambda *_: (l,) + zeros, **mode)


def _row(ref, layer):
    return ref[layer:layer + 1, :]


def _rope_kernel(pos_ref, freq_ref, cos_ref, sin_ref):
    ang = freq_ref[...] * pos_ref[...]
    copies = LANES // ang.shape[0]
    for ref, table in ((cos_ref, jnp.cos(ang)), (sin_ref, jnp.sin(ang))):
        ref[...] = jnp.concatenate([table] * copies, axis=0).T


def _rope_call(positions):
    n = positions.size
    tm = ROPE_ROWS
    half = MLA_ROPE_DIM // 2
    freqs = (ROPE_THETA ** (-np.arange(half, dtype=np.float32) / half)).astype(np.float32)
    table = jax.ShapeDtypeStruct((n, LANES), jnp.float32)
    return pl.pallas_call(
        _rope_kernel,
        grid=(n // tm,),
        in_specs=[pl.BlockSpec((None, 1, tm), lambda i: (i, 0, 0)),
                  pl.BlockSpec((half, 1), lambda i: (0, 0))],
        out_specs=[pl.BlockSpec((tm, LANES), lambda i: (i, 0))] * 2,
        out_shape=[table, table],
        compiler_params=pltpu.CompilerParams(dimension_semantics=("arbitrary",)),
        name="rope",
    )(positions.astype(jnp.float32).reshape(n // tm, 1, tm), jnp.asarray(freqs.reshape(half, 1)))


def _proj_kernel(x_ref, cos_ref, sin_ref, gpre_ref, win_ref, convw_ref, convb_ref, gcq_ref,
                 wuqn_ref, wabs_ref, wuqr_ref, gckv_ref,
                 xn_ref, aq_ref, ak_ref, av_ref, yb_ref, qa_ref, kv_ref, dq_ref, dk_ref, dv_ref,
                 wbf_ref, ubuf_ref, *, layer, tiles_per_seq):
    tm = x_ref.shape[0]
    bf16 = jnp.bfloat16

    @pl.when(pl.program_id(0) == 0)
    def _():
        def put(dst, src, scale=None):
            blk = win_ref[src:src + 256, :].T
            blk = blk if scale is None else blk * scale
            wbf_ref[:, dst:dst + 256] = blk.astype(bf16)

        put(_O_AQ, _O_AQ, SWA_HEAD_DIM ** -0.5 * np.log2(np.e))
        for c in range(_O_AK, P_KR_END, 256):
            put(c, c)
        put(P_DQ, _O_DQ, SB_HEAD_DIM ** -0.5 * np.log2(np.e))
        put(P_DK, _O_DK)
        put(P_DV, _O_DV)

    xb = _rms(x_ref[...], _row(gpre_ref, layer)).astype(bf16)
    xn_ref[...] = xb

    def proj(lo, hi):
        return _dot(xb, wbf_ref[:, lo:hi])

    aq_ref[...] = proj(_O_AQ, _O_AK).astype(bf16)
    akv = proj(_O_AK, _O_BB)
    for ref, blk in ((ak_ref, akv[:, 0:LANES]), (av_ref, akv[:, LANES:2 * LANES])):
        ref[:, 0:LANES] = blk.astype(bf16)
        ref[:, LANES:2 * LANES] = pltpu.roll(blk, SWA_HEAD_DIM, axis=1).astype(bf16)

    dq_ref[...] = proj(P_DQ, P_DK).astype(bf16)
    dk_ref[...] = proj(P_DK, P_DV).astype(bf16)
    dv_ref[...] = proj(P_DV, P_END).astype(bf16)

    ckr = proj(_O_CKV, P_KR_END)

    u = proj(_O_BC, _O_BX) * proj(_O_BX, _O_CQ)

    @pl.when(pl.program_id(0) % tiles_per_seq == 0)
    def _():
        ubuf_ref[0:8, :] = jnp.zeros((8, CONV_WIDTH), jnp.float32)

    ubuf_ref[8:tm + 8, :] = u
    u1 = ubuf_ref[7:tm + 7, :]
    u2 = ubuf_ref[6:tm + 6, :]
    conv = (convw_ref[0:1, :] * u2 + convw_ref[1:2, :] * u1 + convw_ref[2:3, :] * u
            + _row(convb_ref, layer))
    yb_ref[...] = proj(_O_BB, _O_BC) * conv
    ubuf_ref[0:8, :] = ubuf_ref[tm:tm + 8, :]

    cosv = cos_ref[...]
    sinv = sin_ref[...]
    lane = lax.broadcasted_iota(jnp.int32, (tm, LANES), 1)
    in_rope = lane < MLA_ROPE_DIM
    qscale = (MLA_NOPE_DIM + MLA_ROPE_DIM) ** -0.5 * np.log2(np.e)

    cq = _rms(proj(_O_CQ, _O_CKV), _row(gcq_ref, layer)).astype(bf16)
    qnope = _dot(cq, wuqn_ref[...]).astype(bf16)
    qlat = _dot(qnope, wabs_ref[...])
    qr = _dot(cq, wuqr_ref[...])
    qroped = (qr[:, 0:LANES] * cosv + qr[:, LANES:2 * LANES] * sinv) * qscale
    for h in range(MLA_HEADS):
        lat_h = (qlat[:, h * LANES:(h + 1) * LANES] * qscale).astype(bf16)
        mine = qroped if h == 0 else pltpu.roll(qroped, LANES - h * MLA_ROPE_DIM, axis=1)
        rope_h = jnp.where(in_rope, mine, 0.0).astype(bf16)
        for t in range(tm // MLA_TQ):
            src = slice(t * MLA_TQ, (t + 1) * MLA_TQ)
            dst = slice((t * MLA_HEADS + h) * MLA_TQ, (t * MLA_HEADS + h + 1) * MLA_TQ)
            qa_ref[dst, 0:LANES] = lat_h[src]
            qa_ref[dst, LANES:2 * LANES] = rope_h[src]
    kv_ref[:, 0:LANES] = _rms(ckr[:, 0:LANES], _row(gckv_ref, layer)).astype(bf16)
    kr = ckr[:, LANES:2 * LANES]
    half = MLA_ROPE_DIM // 2
    partner = jnp.where(lane < half, -pltpu.roll(kr, LANES - half, axis=1),
                        pltpu.roll(kr, half, axis=1))
    krope = jnp.where(in_rope, kr * cosv + partner * sinv, 0.0)
    kv_ref[:, LANES:2 * LANES] = jnp.where(lane == ONES_LANE - LANES, 1.0, krope).astype(bf16)


def _proj_call(x2, rope, pw, l, seq):
    n = x2.shape[0]
    tm = PROJ_ROWS
    row = lambda w: pl.BlockSpec((tm, w), lambda i: (i, 0))
    bf16 = jnp.bfloat16
    outs = [("xn", 1, D_MODEL, bf16), ("aq", 1, 256, bf16), ("ak", 1, 256, bf16),
            ("av", 1, 256, bf16),
            ("yb", 1, 256, jnp.float32), ("qa", MLA_HEADS, LAT_WIDTH, bf16),
            ("kv", 1, LAT_WIDTH, bf16), ("dq", 1, 256, bf16), ("dk", 1, 256, bf16),
            ("dv", 1, 256, bf16)]
    params = [pw[k] for k in ("gpre", "win", "convw", "convb", "gcq", "wuqn", "wabs", "wuqr",
                              "gckv")]
    res = pl.pallas_call(
        functools.partial(_proj_kernel, layer=l, tiles_per_seq=seq // tm),
        grid=(n // tm,),
        in_specs=[row(D_MODEL), row(LANES), row(LANES)]
        + [_layer_spec(a, l, single=(k == "win")) for k, a in zip(
            ("gpre", "win", "convw", "convb", "gcq", "wuqn", "wabs", "wuqr", "gckv"), params)],
        out_specs=[pl.BlockSpec((r * tm, w), lambda i: (i, 0)) for _, r, w, _ in outs],
        out_shape=[jax.ShapeDtypeStruct((r * n, w), dt) for _, r, w, dt in outs],
        scratch_shapes=[pltpu.VMEM((D_MODEL, P_END), bf16),
                        pltpu.VMEM((tm + 8, CONV_WIDTH), jnp.float32)],
        compiler_params=pltpu.CompilerParams(dimension_semantics=("arbitrary",),
                                             vmem_limit_bytes=VMEM_LIMIT),
        name="proj",
    )(x2, *rope, *params)
    return {name: r for (name, _, _, _), r in zip(outs, res)}


def _swa_tile(sink_ref, q_ref, k_ref, kp_ref, v_ref, vp_ref, layer, first_tile):
    bf16 = jnp.bfloat16
    nsub = q_ref.shape[0] // BLOCK
    out = []
    row = lax.broadcasted_iota(jnp.int32, (BLOCK, 2 * BLOCK), 0)
    col = lax.broadcasted_iota(jnp.int32, (BLOCK, 2 * BLOCK), 1)
    band = (col > row) & (col <= row + BLOCK)
    band_first = band & ((col >= BLOCK) | jnp.logical_not(first_tile))
    upper = lax.broadcasted_iota(jnp.int32, (BLOCK, LANES), 1) >= SWA_HEAD_DIM
    for j in range(nsub):
        rows = slice(j * BLOCK, (j + 1) * BLOCK)
        if j == 0:
            kk = jnp.concatenate([kp_ref[...], k_ref[0:BLOCK, :]], axis=0)
            vv = jnp.concatenate([vp_ref[...], v_ref[0:BLOCK, :]], axis=0)
            mask = band_first
        else:
            kk = k_ref[(j - 1) * BLOCK:(j + 1) * BLOCK, :]
            vv = v_ref[(j - 1) * BLOCK:(j + 1) * BLOCK, :]
            mask = band
        blocks = []
        for lb in range(2):
            q2 = q_ref[rows, lb * LANES:(lb + 1) * LANES]
            halves = []
            for hh in range(2):
                sink = sink_ref[layer, 2 * lb + hh] * np.log2(np.e)
                qm = jnp.where(upper == (hh == 1), q2, jnp.zeros_like(q2))
                sel = slice(0, LANES) if lb == hh else slice(LANES, 2 * LANES)
                s = jnp.where(mask, _dot_nt(qm, kk[:, sel]), NEG_BIG)
                m = jnp.maximum(jnp.max(s, axis=1, keepdims=True), sink)
                p = jnp.exp2(s - m)
                den = jnp.sum(p, axis=1, keepdims=True) + jnp.exp2(sink - m)
                halves.append(_dot(p.astype(bf16), vv[:, sel]) / den)
            blocks.append(jnp.where(upper, halves[1], halves[0]))
        out.append(jnp.concatenate(blocks, axis=1))
    return jnp.concatenate(out, axis=0)


def _attn_kernel(q_ref, kv_ref, wuv_ref, dq_ref, dk_ref, dv_ref, tri_ref, o_ref, od_ref, m_ref,
                 acc_ref):
    tq, tk = MLA_TQ, MLA_TK
    bf16 = jnp.bfloat16
    rows = MLA_HEADS * tq
    i = pl.program_id(1)
    q = q_ref[...]

    def exact_update(off, masked):
        kv = kv_ref[pl.ds(off, tk), :]
        s = _dot_nt(q, kv)
        if masked:
            row = lax.broadcasted_iota(jnp.int32, (rows, tk), 0) & (tq - 1)
            col = lax.broadcasted_iota(jnp.int32, (rows, tk), 1)
            s = jnp.where(col <= row, s, NEG_BIG)
        m = m_ref[:, 0:1]
        m_new = jnp.maximum(m, jnp.max(s, axis=1, keepdims=True))
        p = jnp.exp2(s - m_new)
        acc_ref[...] = jnp.exp2(m - m_new) * acc_ref[...] + _dot(p.astype(bf16), kv)
        m_ref[...] = jnp.broadcast_to(m_new, (rows, LANES))

    def lazy_update(off, width):
        kv = kv_ref[pl.ds(off, width), :]
        d = _dot_nt(q, kv) - jnp.concatenate([m_ref[...]] * (width // LANES), axis=1)
        acc_new = acc_ref[...] + _dot(jnp.exp2(d).astype(bf16), kv)
        safe = jnp.max(d) <= MLA_JUMP

        @pl.when(safe)
        def _():
            acc_ref[...] = acc_new

        @pl.when(jnp.logical_not(safe))
        def _():
            def redo(j, carry):
                exact_update(pl.multiple_of(off + j * tk, tk), False)
                return carry
            lax.fori_loop(0, width // tk, redo, 0)

    off_d = pl.multiple_of(i * tk, tk)
    half = tk // 2
    kv_a = kv_ref[pl.ds(off_d, half), :]
    kv_b = kv_ref[pl.ds(pl.multiple_of(off_d + half, half), half), :]
    q_late = jnp.concatenate([q[h * tq + half:(h + 1) * tq] for h in range(MLA_HEADS)], axis=0)
    row_a = lax.broadcasted_iota(jnp.int32, (rows, half), 0) & (tq - 1)
    col_a = lax.broadcasted_iota(jnp.int32, (rows, half), 1)
    d_a = jnp.where(col_a <= row_a, _dot_nt(q, kv_a), NEG_BIG)
    row_b = lax.broadcasted_iota(jnp.int32, (rows // 2, half), 0) & (half - 1)
    col_b = lax.broadcasted_iota(jnp.int32, (rows // 2, half), 1)
    d_b = jnp.where(col_b <= row_b, _dot_nt(q_late, kv_b), NEG_BIG)
    acc_a = _dot(jnp.exp2(d_a).astype(bf16), kv_a)
    acc_b = _dot(jnp.exp2(d_b).astype(bf16), kv_b)
    pieces = []
    for h in range(MLA_HEADS):
        pieces.append(acc_a[h * tq:h * tq + half])
        pieces.append(acc_a[h * tq + half:(h + 1) * tq] + acc_b[h * half:(h + 1) * half])
    acc0 = jnp.concatenate(pieces, axis=0)
    sums = acc0[:, LANES:2 * LANES]
    ones_lane = lax.broadcasted_iota(jnp.int32, (rows, LANES), 1) == ONES_LANE - LANES
    smallest = jnp.min(jnp.where(ones_lane, sums, 1.0))
    fine = jnp.logical_and(jnp.maximum(jnp.max(d_a), jnp.max(d_b)) <= MLA_JUMP,
                           smallest >= MLA_MIN_SUM)

    sb_tiles = [_sb_open(dq_ref, j, dk_ref, dv_ref, tri_ref, (tq // SB_TQ) * i + j)
                for j in range(tq // SB_TQ)]

    @pl.when(fine)
    def _():
        m_ref[...] = jnp.zeros((rows, LANES), jnp.float32)
        acc_ref[...] = acc0

    @pl.when(jnp.logical_not(fine))
    def _():
        m_ref[...] = jnp.full((rows, LANES), NEG_BIG, jnp.float32)
        acc_ref[...] = jnp.zeros((rows, LAT_WIDTH), jnp.float32)
        exact_update(off_d, True)

    for j, opened in enumerate(sb_tiles):
        _sb_close(opened, (tq // SB_TQ) * i + j, od_ref, j)

    def quad(c, carry):
        lazy_update(pl.multiple_of(c * 4 * tk, tk), 4 * tk)
        return carry

    lax.fori_loop(0, i // 4, quad, 0)

    @pl.when(i % 4 >= 2)
    def _():
        lazy_update(pl.multiple_of((i // 4) * 4 * tk, tk), 2 * tk)

    @pl.when(i % 2 == 1)
    def _():
        lazy_update(pl.multiple_of((i - 1) * tk, tk), tk)

    acc = acc_ref[...]
    lat = (acc[:, 0:LANES] / acc[:, ONES_LANE:ONES_LANE + 1]).astype(bf16)
    heads_on_lanes = jnp.concatenate([lat[h * tq:(h + 1) * tq, :] for h in range(MLA_HEADS)], axis=1)
    o_ref[...] = _dot(heads_on_lanes, wuv_ref[...].reshape(MLA_HEADS * MLA_KV_RANK, -1))


def _sb_open(q_ref, j, k_ref, v_ref, tri_ref, i):
    tq, tk = SB_TQ, SB_TK
    bf16 = jnp.bfloat16
    upper_q = lax.broadcasted_iota(jnp.int32, (tq, LANES), 1) >= SB_HEAD_DIM
    qs = []
    for lb in range(2):
        q2 = q_ref[j * tq:(j + 1) * tq, lb * LANES:(lb + 1) * LANES]
        zero = jnp.zeros_like(q2)
        qs.append(jnp.concatenate([jnp.where(upper_q, zero, q2), jnp.where(upper_q, q2, zero)],
                                  axis=0))
    row = lax.broadcasted_iota(jnp.int32, (2 * tq, tk), 0) & (tq - 1)
    col = lax.broadcasted_iota(jnp.int32, (2 * tq, tk), 1)
    strict = col < row
    tri = tri_ref[...]

    def step(c, state, masked):
        off = pl.multiple_of(c * tk, tk)
        new = []
        for lb in range(2):
            csum, acc = state[lb]
            k2 = k_ref[pl.ds(off, tk), lb * LANES:(lb + 1) * LANES]
            v2 = v_ref[pl.ds(off, tk), lb * LANES:(lb + 1) * LANES]
            z = _dot_nt(qs[lb], k2)
            if masked:
                z = jnp.where(strict, z, NEG_BIG)
            log_beta = jnp.minimum(z, 0.0) - jnp.log2(1.0 + jnp.exp2(-jnp.abs(z)))
            log_keep = log_beta - z
            hi = log_keep.astype(bf16)
            lo = (log_keep - hi.astype(jnp.float32)).astype(bf16)
            within = _dot(jnp.concatenate([hi, lo], axis=1), tri)
            a = jnp.exp2(log_beta + (csum + within))
            acc = acc + _dot(a.astype(bf16), v2)
            csum = csum + jnp.sum(log_keep, axis=1, keepdims=True)
            new.append((csum, acc))
        return tuple(new)

    init = tuple((jnp.zeros((2 * tq, 1), jnp.float32), jnp.zeros((2 * tq, LANES), jnp.float32))
                 for _ in range(2))
    state = step(i, init, True)
    gone = jnp.where(i > 0, 0.0, NEG_BIG)
    state = step(jnp.maximum(i - 1, 0), tuple((csum + gone, acc) for csum, acc in state), False)
    return step, state


def _sb_close(opened, i, o_ref, j):
    tq = SB_TQ
    step, state = opened

    def least_decayed(st):
        return jnp.max(jnp.maximum(st[0][0], st[1][0]))

    def cond(carry):
        t, worst, _ = carry
        return jnp.logical_and(t < i - 1, worst > SB_DEAD)

    def body(carry):
        t, _, st = carry
        st = step(i - 2 - t, st, False)
        return t + 1, least_decayed(st), st

    _, _, state = lax.while_loop(cond, body, (jnp.int32(0), least_decayed(state), state))
    upper_q = lax.broadcasted_iota(jnp.int32, (tq, LANES), 1) >= SB_HEAD_DIM
    for lb in range(2):
        acc = state[lb][1]
        o_ref[j * tq:(j + 1) * tq, lb * LANES:(lb + 1) * LANES] = jnp.where(
            upper_q, acc[tq:2 * tq], acc[0:tq])


def _attn_call(qa, kv, dq, dk, dv, tri, pw, l, batch, seq):
    tq = MLA_TQ
    assert tq & (tq - 1) == 0 and PROJ_ROWS % tq == 0 and tq % SB_TQ == 0
    nq = seq // tq
    rows = MLA_HEADS * tq
    per_seq = lambda a, w: a.reshape(batch, seq, w)
    whole = lambda w: pl.BlockSpec((None, seq, w), lambda b, i: (b, 0, 0))
    tile = lambda w: pl.BlockSpec((None, tq, w), lambda b, i: (b, i, 0))
    out_mla, out_sb = pl.pallas_call(
        _attn_kernel,
        grid=(batch, nq),
        in_specs=[pl.BlockSpec((rows, LAT_WIDTH), lambda b, i: (b * nq + i, 0)),
                  whole(LAT_WIDTH), _layer_spec(pw["wuv"], l),
                  tile(256), whole(256), whole(256),
                  pl.BlockSpec((2 * SB_TK, SB_TK), lambda b, i: (0, 0))],
        out_specs=[tile(256), tile(256)],
        out_shape=[jax.ShapeDtypeStruct((batch, seq, 256), jnp.float32)] * 2,
        scratch_shapes=[pltpu.VMEM((rows, LANES), jnp.float32),
                        pltpu.VMEM((rows, LAT_WIDTH), jnp.float32)],
        compiler_params=pltpu.CompilerParams(
            dimension_semantics=("arbitrary", "arbitrary"),
            vmem_limit_bytes=VMEM_LIMIT),
        name="attn",
    )(qa, per_seq(kv, LAT_WIDTH), pw["wuv"], per_seq(dq, 256), per_seq(dk, 256),
      per_seq(dv, 256), tri)
    return out_mla.reshape(batch * seq, 256), out_sb.reshape(batch * seq, 256)


def _out_kernel(sink_ref, x_ref, xn_ref, aq_ref, ak_ref, akp_ref, av_ref, avp_ref, yb_ref, yc_ref,
                yd_ref, win_ref, ggrp_ref, wout_ref, gpost_ref, o_ref, wgate_ref, woutb_ref, *,
                layer, tiles_per_seq):
    bf16 = jnp.bfloat16

    @pl.when(pl.program_id(0) == 0)
    def _():
        for c in range(0, D_MIX, 256):
            lo = _O_GATE - GATE_SLAB + c
            wgate_ref[:, c:c + 256] = win_ref[lo:lo + 256, :].T.astype(bf16)
        woutb_ref[...] = wout_ref[...].astype(bf16)

    first_tile = (pl.program_id(0) % tiles_per_seq) == 0
    ya = _swa_tile(sink_ref, aq_ref, ak_ref, akp_ref, av_ref, avp_ref, layer, first_tile)
    gates = _dot(xn_ref[...], wgate_ref[...])
    ys = []
    for g, y_in in enumerate((ya, yb_ref, yc_ref, yd_ref)):
        sl = slice(g * GROUP_WIDTH, (g + 1) * GROUP_WIDTH)
        gate = gates[:, sl]
        y = _rms(y_in[...], ggrp_ref[layer:layer + 1, sl]) * (gate * jax.nn.sigmoid(gate))
        ys.append(y.astype(bf16))
    d = _dot(jnp.concatenate(ys, axis=1), woutb_ref[...])
    o_ref[...] = x_ref[...] + _rms(d, _row(gpost_ref, layer))


def _out_call(x2, xn, sinks, aq, ak, av, yb, yc, yd, pw, l, seq):
    n = x2.shape[0]
    tm = OUT_ROWS
    per = tm // BLOCK
    row = lambda w: pl.BlockSpec((tm, w), lambda i: (i, 0))
    prev = lambda w: pl.BlockSpec((BLOCK, w), lambda i: (jnp.maximum(i * per - 1, 0), 0))
    names = ("win", "ggrp", "wout", "gpost")
    params = [pw[k] for k in names]
    gate_slab = pl.BlockSpec((None, D_IN - GATE_SLAB, D_MODEL), lambda i: (l, 1, 0),
                             pipeline_mode=pl.Buffered(1))
    return pl.pallas_call(
        functools.partial(_out_kernel, layer=l, tiles_per_seq=seq // tm),
        grid=(n // tm,),
        in_specs=[pl.BlockSpec(memory_space=pltpu.SMEM), row(D_MODEL), row(D_MODEL), row(256),
                  row(256), prev(256), row(256), prev(256), row(256), row(256), row(256)]
        + [gate_slab if k == "win" else _layer_spec(a, l, single=(k == "wout"))
           for k, a in zip(names, params)],
        out_specs=row(D_MODEL),
        out_shape=jax.ShapeDtypeStruct((n, D_MODEL), jnp.float32),
        scratch_shapes=[pltpu.VMEM((D_MODEL, D_MIX), jnp.bfloat16),
                        pltpu.VMEM((D_MIX, D_MODEL), jnp.bfloat16)],
        compiler_params=pltpu.CompilerParams(dimension_semantics=("arbitrary",),
                                             vmem_limit_bytes=VMEM_LIMIT),
        name="out",
    )(sinks, x2, xn, aq, ak, ak, av, av, yb, yc, yd, *params)


def _rope_swap(w):
    half = MLA_ROPE_DIM // 2
    return jnp.concatenate([-w[..., half:], w[..., :half]], axis=-1)


def _prep_weights(norm_pre, w_in, conv_w, conv_b, mla_q_norm, mla_w_uq, mla_kv_norm, mla_w_ukv,
                  group_norm, w_out, norm_post):
    f32, bf16 = jnp.float32, jnp.bfloat16
    depth = w_in.shape[0]
    dqk = MLA_NOPE_DIM + MLA_ROPE_DIM
    uq = mla_w_uq.reshape(depth, MLA_Q_RANK, MLA_HEADS, dqk)
    wuqn = uq[..., :MLA_NOPE_DIM].reshape(depth, MLA_Q_RANK, -1).astype(bf16)
    wuqr = jnp.concatenate(
        [uq[..., MLA_NOPE_DIM:].reshape(depth, MLA_Q_RANK, -1),
         _rope_swap(uq[..., MLA_NOPE_DIM:]).reshape(depth, MLA_Q_RANK, -1)], axis=-1).astype(bf16)
    ukv = mla_w_ukv.reshape(depth, MLA_KV_RANK, MLA_HEADS, MLA_NOPE_DIM + MLA_V_DIM)
    own_head = jnp.eye(MLA_HEADS, dtype=f32)[None, :, None, :, None]
    uk_t = jnp.transpose(ukv[..., :MLA_NOPE_DIM], (0, 2, 3, 1))
    wabs = (uk_t[:, :, :, None, :] * own_head).reshape(
        depth, MLA_HEADS * MLA_NOPE_DIM, MLA_HEADS * MLA_KV_RANK).astype(bf16)
    uv = jnp.transpose(ukv[..., MLA_NOPE_DIM:], (0, 2, 1, 3))
    wuv = (uv[:, :, :, None, :] * own_head).reshape(
        depth, MLA_HEADS, MLA_KV_RANK, MLA_HEADS * MLA_V_DIM).astype(bf16)

    return dict(
        gpre=norm_pre, win=jnp.swapaxes(w_in, 1, 2), convw=conv_w, convb=conv_b,
        gcq=mla_q_norm, wuqn=wuqn, wabs=wabs, wuqr=wuqr, gckv=mla_kv_norm, wuv=wuv,
        ggrp=group_norm, wout=w_out, gpost=norm_post)


def kernel(x, positions, norm_pre, w_in, attn_sinks, conv_w, conv_b, mla_q_norm, mla_w_uq,
           mla_kv_norm, mla_w_ukv, group_norm, w_out, norm_post):
    batch, seq, _ = x.shape
    depth = w_in.shape[0]
    assert seq % max(PROJ_ROWS, MLA_TQ, SB_TQ, OUT_ROWS) == 0 and OUT_ROWS % BLOCK == 0
    assert MLA_TQ == MLA_TK and SB_TQ == SB_TK and SB_TQ & (SB_TQ - 1) == 0
    n = batch * seq
    x2 = x.reshape(n, D_MODEL)
    tri = jnp.asarray(np.tile(np.tril(np.ones((SB_TK, SB_TK), np.float32), -1), (2, 1)),
                      jnp.bfloat16)
    pw = _prep_weights(norm_pre, w_in, conv_w, conv_b, mla_q_norm, mla_w_uq, mla_kv_norm,
                       mla_w_ukv, group_norm, w_out, norm_post)
    rope = _rope_call(positions)
    for l in range(depth):
        p = _proj_call(x2, rope, pw, l, seq)
        yc, yd = _attn_call(p["qa"], p["kv"], p["dq"], p["dk"], p["dv"], tri, pw, l, batch, seq)
        x2 = _out_call(x2, p["xn"], attn_sinks, p["aq"], p["ak"], p["av"], p["yb"], yc, yd, pw, l,
                       seq)
    return x2.reshape(batch, seq, D_MODEL)
```

```python
import functools

import numpy as np
import jax
import jax.numpy as jnp
from jax import lax
from jax.experimental import pallas as pl
from jax.experimental.pallas import tpu as pltpu

D_MODEL = 1024
BLOCK = 128
NORM_EPS = 1e-6
SWA_HEAD_DIM = 64
CONV_WIDTH = 256
MLA_HEADS = 4
MLA_Q_RANK = 256
MLA_KV_RANK = 128
MLA_NOPE_DIM = 64
MLA_ROPE_DIM = 32
MLA_V_DIM = 64
ROPE_THETA = 10000.0
SB_HEAD_DIM = 64
GROUP_WIDTH = 256
N_GROUPS = 4
D_MIX = GROUP_WIDTH * N_GROUPS

LANES = 128
LAT_WIDTH = 256
ONES_LANE = 160
MLA_JUMP = 32.0
MLA_MIN_SUM = 2.0 ** -64
NEG_BIG = -1e30
SB_DEAD = -160.0

_IN_SIZES = (256, 128, 128, 256, 256, 256, 256, 128, 32, 256, 256, 256, 1024)
_IN_OFF = np.concatenate([[0], np.cumsum(_IN_SIZES)]).astype(int)
(_O_AQ, _O_AK, _O_AV, _O_BB, _O_BC, _O_BX, _O_CQ, _O_CKV, _O_CKR, _O_DQ, _O_DK, _O_DV, _O_GATE,
 _O_END) = [int(v) for v in _IN_OFF]

D_IN = _O_END
GATE_SLAB = D_IN // 2
assert D_IN % 2 == 0 and GATE_SLAB % 8 == 0 and GATE_SLAB <= _O_GATE
P_KR_END = _O_CKR + LANES
P_DQ, P_DK, P_DV, P_END = P_KR_END, P_KR_END + 256, P_KR_END + 512, P_KR_END + 768

ROPE_ROWS = 1024
PROJ_ROWS = 1024
MLA_TQ = 512
MLA_TK = 512
SB_TQ = 256
SB_TK = 256
OUT_ROWS = 1024
VMEM_LIMIT = 56 * 1024 * 1024


def _rms(v, g):
    return v * lax.rsqrt(jnp.mean(v * v, axis=-1, keepdims=True) + NORM_EPS) * g


def _dot(a, b):
    return jnp.dot(a, b, preferred_element_type=jnp.float32)


def _dot_nt(a, b):
    return lax.dot_general(a, b, (((1,), (1,)), ((), ())), preferred_element_type=jnp.float32)


def _layer_spec(a, l, single=False):
    if a.ndim == 2:
        return pl.BlockSpec(a.shape, lambda *_: (0, 0))
    zeros = (0,) * (a.ndim - 1)
    mode = dict(pipeline_mode=pl.Buffered(1)) if single else {}
    return pl.BlockSpec((None,) + a.shape[1:], lambda *_: (l,) + zeros, **mode)


def _row(ref, layer):
    return ref[layer:layer + 1, :]


def _rope_kernel(pos_ref, freq_ref, cos_ref, sin_ref):
    ang = freq_ref[...] * pos_ref[...]
    copies = LANES // ang.shape[0]
    for ref, table in ((cos_ref, jnp.cos(ang)), (sin_ref, jnp.sin(ang))):
        ref[...] = jnp.concatenate([table] * copies, axis=0).T


def _rope_call(positions):
    n = positions.size
    tm = ROPE_ROWS
    half = MLA_ROPE_DIM // 2
    freqs = (ROPE_THETA ** (-np.arange(half, dtype=np.float32) / half)).astype(np.float32)
    table = jax.ShapeDtypeStruct((n, LANES), jnp.float32)
    return pl.pallas_call(
        _rope_kernel,
        grid=(n // tm,),
        in_specs=[pl.BlockSpec((None, 1, tm), lambda i: (i, 0, 0)),
                  pl.BlockSpec((half, 1), lambda i: (0, 0))],
        out_specs=[pl.BlockSpec((tm, LANES), lambda i: (i, 0))] * 2,
        out_shape=[table, table],
        compiler_params=pltpu.CompilerParams(dimension_semantics=("arbitrary",)),
        name="rope",
    )(positions.astype(jnp.float32).reshape(n // tm, 1, tm), jnp.asarray(freqs.reshape(half, 1)))


def _proj_kernel(x_ref, cos_ref, sin_ref, gpre_ref, win_ref, convw_ref, convb_ref, gcq_ref,
                 wuqn_ref, wabs_ref, wuqr_ref, gckv_ref,
                 xn_ref, aq_ref, ak_ref, av_ref, yb_ref, qa_ref, kv_ref, dq_ref, dk_ref, dv_ref,
                 wbf_ref, ubuf_ref, *, layer, tiles_per_seq):
    tm = x_ref.shape[0]
    bf16 = jnp.bfloat16

    @pl.when(pl.program_id(0) == 0)
    def _():
        def put(dst, src, scale=None):
            blk = win_ref[src:src + 256, :].T
            blk = blk if scale is None else blk * scale
            wbf_ref[:, dst:dst + 256] = blk.astype(bf16)

        put(_O_AQ, _O_AQ, SWA_HEAD_DIM ** -0.5 * np.log2(np.e))
        for c in range(_O_AK, P_KR_END, 256):
            put(c, c)
        put(P_DQ, _O_DQ, SB_HEAD_DIM ** -0.5 * np.log2(np.e))
        put(P_DK, _O_DK)
        put(P_DV, _O_DV)

    xb = _rms(x_ref[...], _row(gpre_ref, layer)).astype(bf16)
    xn_ref[...] = xb

    def proj(lo, hi):
        return _dot(xb, wbf_ref[:, lo:hi])

    aq_ref[...] = proj(_O_AQ, _O_AK).astype(bf16)
    akv = proj(_O_AK, _O_BB)
    for ref, blk in ((ak_ref, akv[:, 0:LANES]), (av_ref, akv[:, LANES:2 * LANES])):
        ref[:, 0:LANES] = blk.astype(bf16)
        ref[:, LANES:2 * LANES] = pltpu.roll(blk, SWA_HEAD_DIM, axis=1).astype(bf16)

    dq_ref[...] = proj(P_DQ, P_DK).astype(bf16)
    dk_ref[...] = proj(P_DK, P_DV).astype(bf16)
    dv_ref[...] = proj(P_DV, P_END).astype(bf16)

    ckr = proj(_O_CKV, P_KR_END)

    u = proj(_O_BC, _O_BX) * proj(_O_BX, _O_CQ)

    @pl.when(pl.program_id(0) % tiles_per_seq == 0)
    def _():
        ubuf_ref[0:8, :] = jnp.zeros((8, CONV_WIDTH), jnp.float32)

    ubuf_ref[8:tm + 8, :] = u
    u1 = ubuf_ref[7:tm + 7, :]
    u2 = ubuf_ref[6:tm + 6, :]
    conv = (convw_ref[0:1, :] * u2 + convw_ref[1:2, :] * u1 + convw_ref[2:3, :] * u
            + _row(convb_ref, layer))
    yb_ref[...] = proj(_O_BB, _O_BC) * conv
    ubuf_ref[0:8, :] = ubuf_ref[tm:tm + 8, :]

    cosv = cos_ref[...]
    sinv = sin_ref[...]
    lane = lax.broadcasted_iota(jnp.int32, (tm, LANES), 1)
    in_rope = lane < MLA_ROPE_DIM
    qscale = (MLA_NOPE_DIM + MLA_ROPE_DIM) ** -0.5 * np.log2(np.e)

    cq = _rms(proj(_O_CQ, _O_CKV), _row(gcq_ref, layer)).astype(bf16)
    qnope = _dot(cq, wuqn_ref[...]).astype(bf16)
    qlat = _dot(qnope, wabs_ref[...])
    qr = _dot(cq, wuqr_ref[...])
    qroped = (qr[:, 0:LANES] * cosv + qr[:, LANES:2 * LANES] * sinv) * qscale
    for h in range(MLA_HEADS):
        lat_h = (qlat[:, h * LANES:(h + 1) * LANES] * qscale).astype(bf16)
        mine = qroped if h == 0 else pltpu.roll(qroped, LANES - h * MLA_ROPE_DIM, axis=1)
        rope_h = jnp.where(in_rope, mine, 0.0).astype(bf16)
        for t in range(tm // MLA_TQ):
            src = slice(t * MLA_TQ, (t + 1) * MLA_TQ)
            dst = slice((t * MLA_HEADS + h) * MLA_TQ, (t * MLA_HEADS + h + 1) * MLA_TQ)
            qa_ref[dst, 0:LANES] = lat_h[src]
            qa_ref[dst, LANES:2 * LANES] = rope_h[src]
    kv_ref[:, 0:LANES] = _rms(ckr[:, 0:LANES], _row(gckv_ref, layer)).astype(bf16)
    kr = ckr[:, LANES:2 * LANES]
    half = MLA_ROPE_DIM // 2
    partner = jnp.where(lane < half, -pltpu.roll(kr, LANES - half, axis=1),
                        pltpu.roll(kr, half, axis=1))
    krope = jnp.where(in_rope, kr * cosv + partner * sinv, 0.0)
    kv_ref[:, LANES:2 * LANES] = jnp.where(lane == ONES_LANE - LANES, 1.0, krope).astype(bf16)


def _proj_call(x2, rope, pw, l, seq):
    n = x2.shape[0]
    tm = PROJ_ROWS
    row = lambda w: pl.BlockSpec((tm, w), lambda i: (i, 0))
    bf16 = jnp.bfloat16
    outs = [("xn", 1, D_MODEL, bf16), ("aq", 1, 256, bf16), ("ak", 1, 256, bf16),
            ("av", 1, 256, bf16),
            ("yb", 1, 256, jnp.float32), ("qa", MLA_HEADS, LAT_WIDTH, bf16),
            ("kv", 1, LAT_WIDTH, bf16), ("dq", 1, 256, bf16), ("dk", 1, 256, bf16),
            ("dv", 1, 256, bf16)]
    params = [pw[k] for k in ("gpre", "win", "convw", "convb", "gcq", "wuqn", "wabs", "wuqr",
                              "gckv")]
    res = pl.pallas_call(
        functools.partial(_proj_kernel, layer=l, tiles_per_seq=seq // tm),
        grid=(n // tm,),
        in_specs=[row(D_MODEL), row(LANES), row(LANES)]
        + [_layer_spec(a, l, single=(k == "win")) for k, a in zip(
            ("gpre", "win", "convw", "convb", "gcq", "wuqn", "wabs", "wuqr", "gckv"), params)],
        out_specs=[pl.BlockSpec((r * tm, w), lambda i: (i, 0)) for _, r, w, _ in outs],
        out_shape=[jax.ShapeDtypeStruct((r * n, w), dt) for _, r, w, dt in outs],
        scratch_shapes=[pltpu.VMEM((D_MODEL, P_END), bf16),
                        pltpu.VMEM((tm + 8, CONV_WIDTH), jnp.float32)],
        compiler_params=pltpu.CompilerParams(dimension_semantics=("arbitrary",),
                                             vmem_limit_bytes=VMEM_LIMIT),
        name="proj",
    )(x2, *rope, *params)
    return {name: r for (name, _, _, _), r in zip(outs, res)}


def _swa_tile(sink_ref, q_ref, k_ref, kp_ref, v_ref, vp_ref, layer, first_tile):
    bf16 = jnp.bfloat16
    nsub = q_ref.shape[0] // BLOCK
    out = []
    row = lax.broadcasted_iota(jnp.int32, (BLOCK, 2 * BLOCK), 0)
    col = lax.broadcasted_iota(jnp.int32, (BLOCK, 2 * BLOCK), 1)
    band = (col > row) & (col <= row + BLOCK)
    band_first = band & ((col >= BLOCK) | jnp.logical_not(first_tile))
    upper = lax.broadcasted_iota(jnp.int32, (BLOCK, LANES), 1) >= SWA_HEAD_DIM
    for j in range(nsub):
        rows = slice(j * BLOCK, (j + 1) * BLOCK)
        if j == 0:
            kk = jnp.concatenate([kp_ref[...], k_ref[0:BLOCK, :]], axis=0)
            vv = jnp.concatenate([vp_ref[...], v_ref[0:BLOCK, :]], axis=0)
            mask = band_first
        else:
            kk = k_ref[(j - 1) * BLOCK:(j + 1) * BLOCK, :]
            vv = v_ref[(j - 1) * BLOCK:(j + 1) * BLOCK, :]
            mask = band
        blocks = []
        for lb in range(2):
            q2 = q_ref[rows, lb * LANES:(lb + 1) * LANES]
            halves = []
            for hh in range(2):
                sink = sink_ref[layer, 2 * lb + hh] * np.log2(np.e)
                qm = jnp.where(upper == (hh == 1), q2, jnp.zeros_like(q2))
                sel = slice(0, LANES) if lb == hh else slice(LANES, 2 * LANES)
                s = jnp.where(mask, _dot_nt(qm, kk[:, sel]), NEG_BIG)
                m = jnp.maximum(jnp.max(s, axis=1, keepdims=True), sink)
                p = jnp.exp2(s - m)
                den = jnp.sum(p, axis=1, keepdims=True) + jnp.exp2(sink - m)
                halves.append(_dot(p.astype(bf16), vv[:, sel]) / den)
            blocks.append(jnp.where(upper, halves[1], halves[0]))
        out.append(jnp.concatenate(blocks, axis=1))
    return jnp.concatenate(out, axis=0)


def _attn_kernel(q_ref, kv_ref, wuv_ref, dq_ref, dk_ref, dv_ref, tri_ref, o_ref, od_ref, m_ref,
                 acc_ref):
    tq, tk = MLA_TQ, MLA_TK
    bf16 = jnp.bfloat16
    rows = MLA_HEADS * tq
    i = pl.program_id(1)
    q = q_ref[...]

    def exact_update(off, masked):
        kv = kv_ref[pl.ds(off, tk), :]
        s = _dot_nt(q, kv)
        if masked:
            row = lax.broadcasted_iota(jnp.int32, (rows, tk), 0) & (tq - 1)
            col = lax.broadcasted_iota(jnp.int32, (rows, tk), 1)
            s = jnp.where(col <= row, s, NEG_BIG)
        m = m_ref[:, 0:1]
        m_new = jnp.maximum(m, jnp.max(s, axis=1, keepdims=True))
        p = jnp.exp2(s - m_new)
        acc_ref[...] = jnp.exp2(m - m_new) * acc_ref[...] + _dot(p.astype(bf16), kv)
        m_ref[...] = jnp.broadcast_to(m_new, (rows, LANES))

    def lazy_update(off, width):
        kv = kv_ref[pl.ds(off, width), :]
        d = _dot_nt(q, kv) - jnp.concatenate([m_ref[...]] * (width // LANES), axis=1)
        acc_new = acc_ref[...] + _dot(jnp.exp2(d).astype(bf16), kv)
        safe = jnp.max(d) <= MLA_JUMP

        @pl.when(safe)
        def _():
            acc_ref[...] = acc_new

        @pl.when(jnp.logical_not(safe))
        def _():
            def redo(j, carry):
                exact_update(pl.multiple_of(off + j * tk, tk), False)
                return carry
            lax.fori_loop(0, width // tk, redo, 0)

    off_d = pl.multiple_of(i * tk, tk)
    half = tk // 2
    kv_a = kv_ref[pl.ds(off_d, half), :]
    kv_b = kv_ref[pl.ds(pl.multiple_of(off_d + half, half), half), :]
    q_late = jnp.concatenate([q[h * tq + half:(h + 1) * tq] for h in range(MLA_HEADS)], axis=0)
    row_a = lax.broadcasted_iota(jnp.int32, (rows, half), 0) & (tq - 1)
    col_a = lax.broadcasted_iota(jnp.int32, (rows, half), 1)
    d_a = jnp.where(col_a <= row_a, _dot_nt(q, kv_a), NEG_BIG)
    row_b = lax.broadcasted_iota(jnp.int32, (rows // 2, half), 0) & (half - 1)
    col_b = lax.broadcasted_iota(jnp.int32, (rows // 2, half), 1)
    d_b = jnp.where(col_b <= row_b, _dot_nt(q_late, kv_b), NEG_BIG)
    acc_a = _dot(jnp.exp2(d_a).astype(bf16), kv_a)
    acc_b = _dot(jnp.exp2(d_b).astype(bf16), kv_b)
    ones_lane = lax.broadcasted_iota(jnp.int32, (half, LANES), 1) == ONES_LANE - LANES
    smallest = None
    for h in range(MLA_HEADS):
        early = acc_a[h * tq:h * tq + half]
        late = acc_a[h * tq + half:(h + 1) * tq] + acc_b[h * half:(h + 1) * half]
        acc_ref[h * tq:h * tq + half, :] = early
        acc_ref[h * tq + half:(h + 1) * tq, :] = late
        for part in (early, late):
            low = jnp.min(jnp.where(ones_lane, part[:, LANES:2 * LANES], 1.0))
            smallest = low if smallest is None else jnp.minimum(smallest, low)
    m_ref[...] = jnp.zeros((rows, LANES), jnp.float32)
    fine = jnp.logical_and(jnp.maximum(jnp.max(d_a), jnp.max(d_b)) <= MLA_JUMP,
                           smallest >= MLA_MIN_SUM)

    sb_tiles = [_sb_open(dq_ref, j, dk_ref, dv_ref, tri_ref, (tq // SB_TQ) * i + j)
                for j in range(tq // SB_TQ)]

    @pl.when(jnp.logical_not(fine))
    def _():
        m_ref[...] = jnp.full((rows, LANES), NEG_BIG, jnp.float32)
        acc_ref[...] = jnp.zeros((rows, LAT_WIDTH), jnp.float32)
        exact_update(off_d, True)

    for j, opened in enumerate(sb_tiles):
        _sb_close(opened, (tq // SB_TQ) * i + j, od_ref, j)

    def quad(c, carry):
        lazy_update(pl.multiple_of(c * 4 * tk, tk), 4 * tk)
        return carry

    lax.fori_loop(0, i // 4, quad, 0)

    @pl.when(i % 4 >= 2)
    def _():
        lazy_update(pl.multiple_of((i // 4) * 4 * tk, tk), 2 * tk)

    @pl.when(i % 2 == 1)
    def _():
        lazy_update(pl.multiple_of((i - 1) * tk, tk), tk)

    acc = acc_ref[...]
    lat = (acc[:, 0:LANES] / acc[:, ONES_LANE:ONES_LANE + 1]).astype(bf16)
    heads_on_lanes = jnp.concatenate([lat[h * tq:(h + 1) * tq, :] for h in range(MLA_HEADS)], axis=1)
    o_ref[...] = _dot(heads_on_lanes, wuv_ref[...].reshape(MLA_HEADS * MLA_KV_RANK, -1))


def _sb_open(q_ref, j, k_ref, v_ref, tri_ref, i):
    tq, tk = SB_TQ, SB_TK
    bf16 = jnp.bfloat16
    upper_q = lax.broadcasted_iota(jnp.int32, (tq, LANES), 1) >= SB_HEAD_DIM
    qs = []
    for lb in range(2):
        q2 = q_ref[j * tq:(j + 1) * tq, lb * LANES:(lb + 1) * LANES]
        zero = jnp.zeros_like(q2)
        qs.append(jnp.concatenate([jnp.where(upper_q, zero, q2), jnp.where(upper_q, q2, zero)],
                                  axis=0))
    row = lax.broadcasted_iota(jnp.int32, (2 * tq, tk), 0) & (tq - 1)
    col = lax.broadcasted_iota(jnp.int32, (2 * tq, tk), 1)
    strict = col < row
    tri = tri_ref[...]

    def step(c, state, masked):
        off = pl.multiple_of(c * tk, tk)
        new = []
        for lb in range(2):
            csum, acc = state[lb]
            k2 = k_ref[pl.ds(off, tk), lb * LANES:(lb + 1) * LANES]
            v2 = v_ref[pl.ds(off, tk), lb * LANES:(lb + 1) * LANES]
            z = _dot_nt(qs[lb], k2)
            if masked:
                z = jnp.where(strict, z, NEG_BIG)
            log_beta = jnp.minimum(z, 0.0) - jnp.log2(1.0 + jnp.exp2(-jnp.abs(z)))
            log_keep = log_beta - z
            hi = log_keep.astype(bf16)
            lo = (log_keep - hi.astype(jnp.float32)).astype(bf16)
            within = _dot(jnp.concatenate([hi, lo], axis=1), tri)
            a = jnp.exp2(log_beta + (csum + within))
            acc = acc + _dot(a.astype(bf16), v2)
            csum = csum + jnp.sum(log_keep, axis=1, keepdims=True)
            new.append((csum, acc))
        return tuple(new)

    init = tuple((jnp.zeros((2 * tq, 1), jnp.float32), jnp.zeros((2 * tq, LANES), jnp.float32))
                 for _ in range(2))
    state = step(i, init, True)
    gone = jnp.where(i > 0, 0.0, NEG_BIG)
    state = step(jnp.maximum(i - 1, 0), tuple((csum + gone, acc) for csum, acc in state), False)
    return step, state


def _sb_close(opened, i, o_ref, j):
    tq = SB_TQ
    step, state = opened

    def least_decayed(st):
        return jnp.max(jnp.maximum(st[0][0], st[1][0]))

    def cond(carry):
        t, worst, _ = carry
        return jnp.logical_and(t < i - 1, worst > SB_DEAD)

    def body(carry):
        t, _, st = carry
        st = step(i - 2 - t, st, False)
        return t + 1, least_decayed(st), st

    _, _, state = lax.while_loop(cond, body, (jnp.int32(0), least_decayed(state), state))
    upper_q = lax.broadcasted_iota(jnp.int32, (tq, LANES), 1) >= SB_HEAD_DIM
    for lb in range(2):
        acc = state[lb][1]
        o_ref[j * tq:(j + 1) * tq, lb * LANES:(lb + 1) * LANES] = jnp.where(
            upper_q, acc[tq:2 * tq], acc[0:tq])


def _attn_call(qa, kv, dq, dk, dv, tri, pw, l, batch, seq):
    tq = MLA_TQ
    assert tq & (tq - 1) == 0 and PROJ_ROWS % tq == 0 and tq % SB_TQ == 0
    nq = seq // tq
    rows = MLA_HEADS * tq
    per_seq = lambda a, w: a.reshape(batch, seq, w)
    whole = lambda w: pl.BlockSpec((None, seq, w), lambda b, i: (b, 0, 0))
    tile = lambda w: pl.BlockSpec((None, tq, w), lambda b, i: (b, i, 0))
    out_mla, out_sb = pl.pallas_call(
        _attn_kernel,
        grid=(batch, nq),
        in_specs=[pl.BlockSpec((rows, LAT_WIDTH), lambda b, i: (b * nq + i, 0)),
                  whole(LAT_WIDTH), _layer_spec(pw["wuv"], l),
                  tile(256), whole(256), whole(256),
                  pl.BlockSpec((2 * SB_TK, SB_TK), lambda b, i: (0, 0))],
        out_specs=[tile(256), tile(256)],
        out_shape=[jax.ShapeDtypeStruct((batch, seq, 256), jnp.float32)] * 2,
        scratch_shapes=[pltpu.VMEM((rows, LANES), jnp.float32),
                        pltpu.VMEM((rows, LAT_WIDTH), jnp.float32)],
        compiler_params=pltpu.CompilerParams(
            dimension_semantics=("arbitrary", "arbitrary"),
            vmem_limit_bytes=VMEM_LIMIT),
        name="attn",
    )(qa, per_seq(kv, LAT_WIDTH), pw["wuv"], per_seq(dq, 256), per_seq(dk, 256),
      per_seq(dv, 256), tri)
    return out_mla.reshape(batch * seq, 256), out_sb.reshape(batch * seq, 256)


def _out_kernel(sink_ref, x_ref, xn_ref, aq_ref, ak_ref, akp_ref, av_ref, avp_ref, yb_ref, yc_ref,
                yd_ref, win_ref, ggrp_ref, wout_ref, gpost_ref, o_ref, wgate_ref, woutb_ref, *,
                layer, tiles_per_seq):
    bf16 = jnp.bfloat16

    @pl.when(pl.program_id(0) == 0)
    def _():
        for c in range(0, D_MIX, 256):
            lo = _O_GATE - GATE_SLAB + c
            wgate_ref[:, c:c + 256] = win_ref[lo:lo + 256, :].T.astype(bf16)
        woutb_ref[...] = wout_ref[...].astype(bf16)

    first_tile = (pl.program_id(0) % tiles_per_seq) == 0
    ya = _swa_tile(sink_ref, aq_ref, ak_ref, akp_ref, av_ref, avp_ref, layer, first_tile)
    gates = _dot(xn_ref[...], wgate_ref[...])
    ys = []
    for g, y_in in enumerate((ya, yb_ref, yc_ref, yd_ref)):
        sl = slice(g * GROUP_WIDTH, (g + 1) * GROUP_WIDTH)
        gate = gates[:, sl]
        y = _rms(y_in[...], ggrp_ref[layer:layer + 1, sl]) * (gate * jax.nn.sigmoid(gate))
        ys.append(y.astype(bf16))
    d = _dot(jnp.concatenate(ys, axis=1), woutb_ref[...])
    o_ref[...] = x_ref[...] + _rms(d, _row(gpost_ref, layer))


def _out_call(x2, xn, sinks, aq, ak, av, yb, yc, yd, pw, l, seq):
    n = x2.shape[0]
    tm = OUT_ROWS
    per = tm // BLOCK
    row = lambda w: pl.BlockSpec((tm, w), lambda i: (i, 0))
    prev = lambda w: pl.BlockSpec((BLOCK, w), lambda i: (jnp.maximum(i * per - 1, 0), 0))
    names = ("win", "ggrp", "wout", "gpost")
    params = [pw[k] for k in names]
    gate_slab = pl.BlockSpec((None, D_IN - GATE_SLAB, D_MODEL), lambda i: (l, 1, 0),
                             pipeline_mode=pl.Buffered(1))
    return pl.pallas_call(
        functools.partial(_out_kernel, layer=l, tiles_per_seq=seq // tm),
        grid=(n // tm,),
        in_specs=[pl.BlockSpec(memory_space=pltpu.SMEM), row(D_MODEL), row(D_MODEL), row(256),
                  row(256), prev(256), row(256), prev(256), row(256), row(256), row(256)]
        + [gate_slab if k == "win" else _layer_spec(a, l, single=(k == "wout"))
           for k, a in zip(names, params)],
        out_specs=row(D_MODEL),
        out_shape=jax.ShapeDtypeStruct((n, D_MODEL), jnp.float32),
        scratch_shapes=[pltpu.VMEM((D_MODEL, D_MIX), jnp.bfloat16),
                        pltpu.VMEM((D_MIX, D_MODEL), jnp.bfloat16)],
        compiler_params=pltpu.CompilerParams(dimension_semantics=("arbitrary",),
                                             vmem_limit_bytes=VMEM_LIMIT),
        name="out",
    )(sinks, x2, xn, aq, ak, ak, av, av, yb, yc, yd, *params)


def _rope_swap(w):
    half = MLA_ROPE_DIM // 2
    return jnp.concatenate([-w[..., half:], w[..., :half]], axis=-1)


def _prep_weights(norm_pre, w_in, conv_w, conv_b, mla_q_norm, mla_w_uq, mla_kv_norm, mla_w_ukv,
                  group_norm, w_out, norm_post):
    f32, bf16 = jnp.float32, jnp.bfloat16
    depth = w_in.shape[0]
    dqk = MLA_NOPE_DIM + MLA_ROPE_DIM
    uq = mla_w_uq.reshape(depth, MLA_Q_RANK, MLA_HEADS, dqk)
    wuqn = uq[..., :MLA_NOPE_DIM].reshape(depth, MLA_Q_RANK, -1).astype(bf16)
    wuqr = jnp.concatenate(
        [uq[..., MLA_NOPE_DIM:].reshape(depth, MLA_Q_RANK, -1),
         _rope_swap(uq[..., MLA_NOPE_DIM:]).reshape(depth, MLA_Q_RANK, -1)], axis=-1).astype(bf16)
    ukv = mla_w_ukv.reshape(depth, MLA_KV_RANK, MLA_HEADS, MLA_NOPE_DIM + MLA_V_DIM)
    own_head = jnp.eye(MLA_HEADS, dtype=f32)[None, :, None, :, None]
    uk_t = jnp.transpose(ukv[..., :MLA_NOPE_DIM], (0, 2, 3, 1))
    wabs = (uk_t[:, :, :, None, :] * own_head).reshape(
        depth, MLA_HEADS * MLA_NOPE_DIM, MLA_HEADS * MLA_KV_RANK).astype(bf16)
    uv = jnp.transpose(ukv[..., MLA_NOPE_DIM:], (0, 2, 1, 3))
    wuv = (uv[:, :, :, None, :] * own_head).reshape(
        depth, MLA_HEADS, MLA_KV_RANK, MLA_HEADS * MLA_V_DIM).astype(bf16)

    return dict(
        gpre=norm_pre, win=jnp.swapaxes(w_in, 1, 2), convw=conv_w, convb=conv_b,
        gcq=mla_q_norm, wuqn=wuqn, wabs=wabs, wuqr=wuqr, gckv=mla_kv_norm, wuv=wuv,
        ggrp=group_norm, wout=w_out, gpost=norm_post)


def kernel(x, positions, norm_pre, w_in, attn_sinks, conv_w, conv_b, mla_q_norm, mla_w_uq,
           mla_kv_norm, mla_w_ukv, group_norm, w_out, norm_post):
    batch, seq, _ = x.shape
    depth = w_in.shape[0]
    assert seq % max(PROJ_ROWS, MLA_TQ, SB_TQ, OUT_ROWS) == 0 and OUT_ROWS % BLOCK == 0
    assert MLA_TQ == MLA_TK and SB_TQ == SB_TK and SB_TQ & (SB_TQ - 1) == 0
    n = batch * seq
    x2 = x.reshape(n, D_MODEL)
    tri = jnp.asarray(np.tile(np.tril(np.ones((SB_TK, SB_TK), np.float32), -1), (2, 1)),
                      jnp.bfloat16)
    pw = _prep_weights(norm_pre, w_in, conv_w, conv_b, mla_q_norm, mla_w_uq, mla_kv_norm,
                       mla_w_ukv, group_norm, w_out, norm_post)
    rope = _rope_call(positions)
    for l in range(depth):
        p = _proj_call(x2, rope, pw, l, seq)
        yc, yd = _attn_call(p["qa"], p["kv"], p["dq"], p["dk"], p["dv"], tri, pw, l, batch, seq)
        x2 = _out_call(x2, p["xn"], attn_sinks, p["aq"], p["ak"], p["av"], p["yb"], yc, yd, pw, l,
                       seq)
    return x2.reshape(batch, seq, D_MODEL)
```

```python
import functools

import numpy as np
import jax
import jax.numpy as jnp
from jax import lax
from jax.experimental import pallas as pl
from jax.experimental.pallas import tpu as pltpu

D_MODEL = 1024
BLOCK = 128
NORM_EPS = 1e-6
SWA_HEAD_DIM = 64
CONV_WIDTH = 256
MLA_HEADS = 4
MLA_Q_RANK = 256
MLA_KV_RANK = 128
MLA_NOPE_DIM = 64
MLA_ROPE_DIM = 32
MLA_V_DIM = 64
ROPE_THETA = 10000.0
SB_HEAD_DIM = 64
GROUP_WIDTH = 256
N_GROUPS = 4
D_MIX = GROUP_WIDTH * N_GROUPS

LANES = 128
LAT_WIDTH = 256
ONES_LANE = 160
MLA_JUMP = 32.0
MLA_MIN_SUM = 2.0 ** -64
NEG_BIG = -1e30
SB_DEAD = -160.0

_IN_SIZES = (256, 128, 128, 256, 256, 256, 256, 128, 32, 256, 256, 256, 1024)
_IN_OFF = np.concatenate([[0], np.cumsum(_IN_SIZES)]).astype(int)
(_O_AQ, _O_AK, _O_AV, _O_BB, _O_BC, _O_BX, _O_CQ, _O_CKV, _O_CKR, _O_DQ, _O_DK, _O_DV, _O_GATE,
 _O_END) = [int(v) for v in _IN_OFF]

D_IN = _O_END
GATE_SLAB = D_IN // 2
assert D_IN % 2 == 0 and GATE_SLAB % 8 == 0 and GATE_SLAB <= _O_GATE
P_KR_END = _O_CKR + LANES
P_DQ, P_DK, P_DV, P_END = P_KR_END, P_KR_END + 256, P_KR_END + 512, P_KR_END + 768

ROPE_ROWS = 1024
PROJ_ROWS = 1024
MLA_TQ = 512
MLA_TK = 512
SB_TQ = 256
SB_TK = 256
OUT_ROWS = 1024
VMEM_LIMIT = 56 * 1024 * 1024


def _rms(v, g):
    return v * lax.rsqrt(jnp.mean(v * v, axis=-1, keepdims=True) + NORM_EPS) * g


def _dot(a, b):
    return jnp.dot(a, b, preferred_element_type=jnp.float32)


def _dot_nt(a, b):
    return lax.dot_general(a, b, (((1,), (1,)), ((), ())), preferred_element_type=jnp.float32)


def _layer_spec(a, l, single=False):
    if a.ndim == 2:
        return pl.BlockSpec(a.shape, lambda *_: (0, 0))
    zeros = (0,) * (a.ndim - 1)
    mode = dict(pipeline_mode=pl.Buffered(1)) if single else {}
    return pl.BlockSpec((None,) + a.shape[1:], lambda *_: (l,) + zeros, **mode)


def _row(ref, layer):
    return ref[layer:layer + 1, :]


def _rope_kernel(pos_ref, freq_ref, cos_ref, sin_ref):
    ang = freq_ref[...] * pos_ref[...]
    copies = LANES // ang.shape[0]
    for ref, table in ((cos_ref, jnp.cos(ang)), (sin_ref, jnp.sin(ang))):
        ref[...] = jnp.concatenate([table] * copies, axis=0).T


def _rope_call(positions):
    n = positions.size
    tm = ROPE_ROWS
    half = MLA_ROPE_DIM // 2
    freqs = (ROPE_THETA ** (-np.arange(half, dtype=np.float32) / half)).astype(np.float32)
    table = jax.ShapeDtypeStruct((n, LANES), jnp.float32)
    return pl.pallas_call(
        _rope_kernel,
        grid=(n // tm,),
        in_specs=[pl.BlockSpec((None, 1, tm), lambda i: (i, 0, 0)),
                  pl.BlockSpec((half, 1), lambda i: (0, 0))],
        out_specs=[pl.BlockSpec((tm, LANES), lambda i: (i, 0))] * 2,
        out_shape=[table, table],
        compiler_params=pltpu.CompilerParams(dimension_semantics=("arbitrary",)),
        name="rope",
    )(positions.astype(jnp.float32).reshape(n // tm, 1, tm), jnp.asarray(freqs.reshape(half, 1)))


def _proj_kernel(x_ref, cos_ref, sin_ref, gpre_ref, win_ref, convw_ref, convb_ref, gcq_ref,
                 wuqn_ref, wabs_ref, wuqr_ref, gckv_ref,
                 xn_ref, aq_ref, ak_ref, av_ref, yb_ref, qa_ref, kv_ref, dq_ref, dk_ref, dv_ref,
                 wbf_ref, ubuf_ref, *, layer, tiles_per_seq):
    tm = x_ref.shape[0]
    bf16 = jnp.bfloat16

    @pl.when(pl.program_id(0) == 0)
    def _():
        def put(dst, src, scale=None):
            blk = win_ref[src:src + 256, :].T
            blk = blk if scale is None else blk * scale
            wbf_ref[:, dst:dst + 256] = blk.astype(bf16)

        put(_O_AQ, _O_AQ, SWA_HEAD_DIM ** -0.5 * np.log2(np.e))
        for c in range(_O_AK, P_KR_END, 256):
            put(c, c)
        put(P_DQ, _O_DQ, SB_HEAD_DIM ** -0.5 * np.log2(np.e))
        put(P_DK, _O_DK)
        put(P_DV, _O_DV)

    xb = _rms(x_ref[...], _row(gpre_ref, layer)).astype(bf16)
    xn_ref[...] = xb

    def proj(lo, hi):
        return _dot(xb, wbf_ref[:, lo:hi])

    aq_ref[...] = proj(_O_AQ, _O_AK).astype(bf16)
    akv = proj(_O_AK, _O_BB)
    for ref, blk in ((ak_ref, akv[:, 0:LANES]), (av_ref, akv[:, LANES:2 * LANES])):
        ref[:, 0:LANES] = blk.astype(bf16)
        ref[:, LANES:2 * LANES] = pltpu.roll(blk, SWA_HEAD_DIM, axis=1).astype(bf16)

    dq_ref[...] = proj(P_DQ, P_DK).astype(bf16)
    dk_ref[...] = proj(P_DK, P_DV).astype(bf16)
    dv_ref[...] = proj(P_DV, P_END).astype(bf16)

    ckr = proj(_O_CKV, P_KR_END)

    u = proj(_O_BC, _O_BX) * proj(_O_BX, _O_CQ)

    @pl.when(pl.program_id(0) % tiles_per_seq == 0)
    def _():
        ubuf_ref[0:8, :] = jnp.zeros((8, CONV_WIDTH), jnp.float32)

    ubuf_ref[8:tm + 8, :] = u
    u1 = ubuf_ref[7:tm + 7, :]
    u2 = ubuf_ref[6:tm + 6, :]
    conv = (convw_ref[0:1, :] * u2 + convw_ref[1:2, :] * u1 + convw_ref[2:3, :] * u
            + _row(convb_ref, layer))
    yb_ref[...] = proj(_O_BB, _O_BC) * conv
    ubuf_ref[0:8, :] = ubuf_ref[tm:tm + 8, :]

    cosv = cos_ref[...]
    sinv = sin_ref[...]
    lane = lax.broadcasted_iota(jnp.int32, (tm, LANES), 1)
    in_rope = lane < MLA_ROPE_DIM
    qscale = (MLA_NOPE_DIM + MLA_ROPE_DIM) ** -0.5 * np.log2(np.e)

    cq = _rms(proj(_O_CQ, _O_CKV), _row(gcq_ref, layer)).astype(bf16)
    qnope = _dot(cq, wuqn_ref[...]).astype(bf16)
    qlat = _dot(qnope, wabs_ref[...])
    qr = _dot(cq, wuqr_ref[...])
    qroped = (qr[:, 0:LANES] * cosv + qr[:, LANES:2 * LANES] * sinv) * qscale
    for h in range(MLA_HEADS):
        lat_h = (qlat[:, h * LANES:(h + 1) * LANES] * qscale).astype(bf16)
        mine = qroped if h == 0 else pltpu.roll(qroped, LANES - h * MLA_ROPE_DIM, axis=1)
        rope_h = jnp.where(in_rope, mine, 0.0).astype(bf16)
        for t in range(tm // MLA_TQ):
            src = slice(t * MLA_TQ, (t + 1) * MLA_TQ)
            dst = slice((t * MLA_HEADS + h) * MLA_TQ, (t * MLA_HEADS + h + 1) * MLA_TQ)
            qa_ref[dst, 0:LANES] = lat_h[src]
            qa_ref[dst, LANES:2 * LANES] = rope_h[src]
    kv_ref[:, 0:LANES] = _rms(ckr[:, 0:LANES], _row(gckv_ref, layer)).astype(bf16)
    kr = ckr[:, LANES:2 * LANES]
    half = MLA_ROPE_DIM // 2
    partner = jnp.where(lane < half, -pltpu.roll(kr, LANES - half, axis=1),
                        pltpu.roll(kr, half, axis=1))
    krope = jnp.where(in_rope, kr * cosv + partner * sinv, 0.0)
    kv_ref[:, LANES:2 * LANES] = jnp.where(lane == ONES_LANE - LANES, 1.0, krope).astype(bf16)


def _proj_call(x2, rope, pw, l, seq):
    n = x2.shape[0]
    tm = PROJ_ROWS
    row = lambda w: pl.BlockSpec((tm, w), lambda i: (i, 0))
    bf16 = jnp.bfloat16
    outs = [("xn", 1, D_MODEL, bf16), ("aq", 1, 256, bf16), ("ak", 1, 256, bf16),
            ("av", 1, 256, bf16),
            ("yb", 1, 256, jnp.float32), ("qa", MLA_HEADS, LAT_WIDTH, bf16),
            ("kv", 1, LAT_WIDTH, bf16), ("dq", 1, 256, bf16), ("dk", 1, 256, bf16),
            ("dv", 1, 256, bf16)]
    params = [pw[k] for k in ("gpre", "win", "convw", "convb", "gcq", "wuqn", "wabs", "wuqr",
                              "gckv")]
    res = pl.pallas_call(
        functools.partial(_proj_kernel, layer=l, tiles_per_seq=seq // tm),
        grid=(n // tm,),
        in_specs=[row(D_MODEL), row(LANES), row(LANES)]
        + [_layer_spec(a, l, single=(k == "win")) for k, a in zip(
            ("gpre", "win", "convw", "convb", "gcq", "wuqn", "wabs", "wuqr", "gckv"), params)],
        out_specs=[pl.BlockSpec((r * tm, w), lambda i: (i, 0)) for _, r, w, _ in outs],
        out_shape=[jax.ShapeDtypeStruct((r * n, w), dt) for _, r, w, dt in outs],
        scratch_shapes=[pltpu.VMEM((D_MODEL, P_END), bf16),
                        pltpu.VMEM((tm + 8, CONV_WIDTH), jnp.float32)],
        compiler_params=pltpu.CompilerParams(dimension_semantics=("arbitrary",),
                                             vmem_limit_bytes=VMEM_LIMIT),
        name="proj",
    )(x2, *rope, *params)
    return {name: r for (name, _, _, _), r in zip(outs, res)}


def _swa_tile(sink_ref, q_ref, k_ref, kp_ref, v_ref, vp_ref, layer, first_tile):
    bf16 = jnp.bfloat16
    nsub = q_ref.shape[0] // BLOCK
    out = []
    row = lax.broadcasted_iota(jnp.int32, (BLOCK, 2 * BLOCK), 0)
    col = lax.broadcasted_iota(jnp.int32, (BLOCK, 2 * BLOCK), 1)
    band = (col > row) & (col <= row + BLOCK)
    band_first = band & ((col >= BLOCK) | jnp.logical_not(first_tile))
    upper = lax.broadcasted_iota(jnp.int32, (BLOCK, LANES), 1) >= SWA_HEAD_DIM
    for j in range(nsub):
        rows = slice(j * BLOCK, (j + 1) * BLOCK)
        if j == 0:
            kk = jnp.concatenate([kp_ref[...], k_ref[0:BLOCK, :]], axis=0)
            vv = jnp.concatenate([vp_ref[...], v_ref[0:BLOCK, :]], axis=0)
            mask = band_first
        else:
            kk = k_ref[(j - 1) * BLOCK:(j + 1) * BLOCK, :]
            vv = v_ref[(j - 1) * BLOCK:(j + 1) * BLOCK, :]
            mask = band
        blocks = []
        for lb in range(2):
            q2 = q_ref[rows, lb * LANES:(lb + 1) * LANES]
            halves = []
            for hh in range(2):
                sink = sink_ref[layer, 2 * lb + hh] * np.log2(np.e)
                qm = jnp.where(upper == (hh == 1), q2, jnp.zeros_like(q2))
                sel = slice(0, LANES) if lb == hh else slice(LANES, 2 * LANES)
                s = jnp.where(mask, _dot_nt(qm, kk[:, sel]), NEG_BIG)
                m = jnp.maximum(jnp.max(s, axis=1, keepdims=True), sink)
                p = jnp.exp2(s - m)
                den = jnp.sum(p, axis=1, keepdims=True) + jnp.exp2(sink - m)
                halves.append(_dot(p.astype(bf16), vv[:, sel]) / den)
            blocks.append(jnp.where(upper, halves[1], halves[0]))
        out.append(jnp.concatenate(blocks, axis=1))
    return jnp.concatenate(out, axis=0)


def _attn_kernel(q_ref, kv_ref, wuv_ref, dq_ref, dk_ref, dv_ref, tri_ref, o_ref, od_ref, m_ref,
                 acc_ref, slot_ref):
    tq, tk = MLA_TQ, MLA_TK
    bf16 = jnp.bfloat16
    rows = MLA_HEADS * tq
    i = pl.program_id(1)
    q = q_ref[...]

    def exact_update(off, masked):
        kv = kv_ref[pl.ds(off, tk), :]
        s = _dot_nt(q, kv)
        if masked:
            row = lax.broadcasted_iota(jnp.int32, (rows, tk), 0) & (tq - 1)
            col = lax.broadcasted_iota(jnp.int32, (rows, tk), 1)
            s = jnp.where(col <= row, s, NEG_BIG)
        m = m_ref[:, 0:1]
        m_new = jnp.maximum(m, jnp.max(s, axis=1, keepdims=True))
        p = jnp.exp2(s - m_new)
        slot = slot_ref[0]
        acc_ref[slot] = jnp.exp2(m - m_new) * acc_ref[slot] + _dot(p.astype(bf16), kv)
        m_ref[...] = jnp.broadcast_to(m_new, (rows, LANES))

    def lazy_update(off, width):
        kv = kv_ref[pl.ds(off, width), :]
        d = _dot_nt(q, kv) - jnp.concatenate([m_ref[...]] * (width // LANES), axis=1)
        slot = slot_ref[0]
        acc_ref[1 - slot] = acc_ref[slot] + _dot(jnp.exp2(d).astype(bf16), kv)
        safe = jnp.max(d) <= MLA_JUMP
        slot_ref[0] = jnp.where(safe, 1 - slot, slot)

        @pl.when(jnp.logical_not(safe))
        def _():
            def redo(j, carry):
                exact_update(pl.multiple_of(off + j * tk, tk), False)
                return carry
            lax.fori_loop(0, width // tk, redo, 0)

    off_d = pl.multiple_of(i * tk, tk)
    half = tk // 2
    kv_a = kv_ref[pl.ds(off_d, half), :]
    kv_b = kv_ref[pl.ds(pl.multiple_of(off_d + half, half), half), :]
    q_late = jnp.concatenate([q[h * tq + half:(h + 1) * tq] for h in range(MLA_HEADS)], axis=0)
    row_a = lax.broadcasted_iota(jnp.int32, (rows, half), 0) & (tq - 1)
    col_a = lax.broadcasted_iota(jnp.int32, (rows, half), 1)
    d_a = jnp.where(col_a <= row_a, _dot_nt(q, kv_a), NEG_BIG)
    row_b = lax.broadcasted_iota(jnp.int32, (rows // 2, half), 0) & (half - 1)
    col_b = lax.broadcasted_iota(jnp.int32, (rows // 2, half), 1)
    d_b = jnp.where(col_b <= row_b, _dot_nt(q_late, kv_b), NEG_BIG)
    acc_a = _dot(jnp.exp2(d_a).astype(bf16), kv_a)
    acc_b = _dot(jnp.exp2(d_b).astype(bf16), kv_b)
    ones_lane = lax.broadcasted_iota(jnp.int32, (half, LANES), 1) == ONES_LANE - LANES
    smallest = None
    for h in range(MLA_HEADS):
        early = acc_a[h * tq:h * tq + half]
        late = acc_a[h * tq + half:(h + 1) * tq] + acc_b[h * half:(h + 1) * half]
        acc_ref[0, h * tq:h * tq + half, :] = early
        acc_ref[0, h * tq + half:(h + 1) * tq, :] = late
        for part in (early, late):
            low = jnp.min(jnp.where(ones_lane, part[:, LANES:2 * LANES], 1.0))
            smallest = low if smallest is None else jnp.minimum(smallest, low)
    m_ref[...] = jnp.zeros((rows, LANES), jnp.float32)
    slot_ref[0] = 0
    fine = jnp.logical_and(jnp.maximum(jnp.max(d_a), jnp.max(d_b)) <= MLA_JUMP,
                           smallest >= MLA_MIN_SUM)

    sb_tiles = [_sb_open(dq_ref, j, dk_ref, dv_ref, tri_ref, (tq // SB_TQ) * i + j)
                for j in range(tq // SB_TQ)]

    @pl.when(jnp.logical_not(fine))
    def _():
        m_ref[...] = jnp.full((rows, LANES), NEG_BIG, jnp.float32)
        acc_ref[0] = jnp.zeros((rows, LAT_WIDTH), jnp.float32)
        exact_update(off_d, True)

    for j, opened in enumerate(sb_tiles):
        _sb_close(opened, (tq // SB_TQ) * i + j, od_ref, j)

    def quad(c, carry):
        lazy_update(pl.multiple_of(c * 4 * tk, tk), 4 * tk)
        return carry

    lax.fori_loop(0, i // 4, quad, 0)

    @pl.when(i % 4 >= 2)
    def _():
        lazy_update(pl.multiple_of((i // 4) * 4 * tk, tk), 2 * tk)

    @pl.when(i % 2 == 1)
    def _():
        lazy_update(pl.multiple_of((i - 1) * tk, tk), tk)

    acc = acc_ref[slot_ref[0]]
    lat = (acc[:, 0:LANES] / acc[:, ONES_LANE:ONES_LANE + 1]).astype(bf16)
    heads_on_lanes = jnp.concatenate([lat[h * tq:(h + 1) * tq, :] for h in range(MLA_HEADS)], axis=1)
    o_ref[...] = _dot(heads_on_lanes, wuv_ref[...].reshape(MLA_HEADS * MLA_KV_RANK, -1))


def _sb_open(q_ref, j, k_ref, v_ref, tri_ref, i):
    tq, tk = SB_TQ, SB_TK
    bf16 = jnp.bfloat16
    upper_q = lax.broadcasted_iota(jnp.int32, (tq, LANES), 1) >= SB_HEAD_DIM
    qs = []
    for lb in range(2):
        q2 = q_ref[j * tq:(j + 1) * tq, lb * LANES:(lb + 1) * LANES]
        zero = jnp.zeros_like(q2)
        qs.append(jnp.concatenate([jnp.where(upper_q, zero, q2), jnp.where(upper_q, q2, zero)],
                                  axis=0))
    row = lax.broadcasted_iota(jnp.int32, (2 * tq, tk), 0) & (tq - 1)
    col = lax.broadcasted_iota(jnp.int32, (2 * tq, tk), 1)
    strict = col < row
    tri = tri_ref[...]

    def step(c, state, masked):
        off = pl.multiple_of(c * tk, tk)
        new = []
        for lb in range(2):
            csum, acc = state[lb]
            k2 = k_ref[pl.ds(off, tk), lb * LANES:(lb + 1) * LANES]
            v2 = v_ref[pl.ds(off, tk), lb * LANES:(lb + 1) * LANES]
            z = _dot_nt(qs[lb], k2)
            if masked:
                z = jnp.where(strict, z, NEG_BIG)
            log_beta = jnp.minimum(z, 0.0) - jnp.log2(1.0 + jnp.exp2(-jnp.abs(z)))
            log_keep = log_beta - z
            hi = log_keep.astype(bf16)
            lo = (log_keep - hi.astype(jnp.float32)).astype(bf16)
            within = _dot(jnp.concatenate([hi, lo], axis=1), tri)
            a = jnp.exp2(log_beta + (csum + within))
            acc = acc + _dot(a.astype(bf16), v2)
            csum = csum + jnp.sum(log_keep, axis=1, keepdims=True)
            new.append((csum, acc))
        return tuple(new)

    init = tuple((jnp.zeros((2 * tq, 1), jnp.float32), jnp.zeros((2 * tq, LANES), jnp.float32))
                 for _ in range(2))
    state = step(i, init, True)
    gone = jnp.where(i > 0, 0.0, NEG_BIG)
    state = step(jnp.maximum(i - 1, 0), tuple((csum + gone, acc) for csum, acc in state), False)
    return step, state


def _sb_close(opened, i, o_ref, j):
    tq = SB_TQ
    step, state = opened

    def least_decayed(st):
        return jnp.max(jnp.maximum(st[0][0], st[1][0]))

    def cond(carry):
        t, worst, _ = carry
        return jnp.logical_and(t < i - 1, worst > SB_DEAD)

    def body(carry):
        t, _, st = carry
        st = step(i - 2 - t, st, False)
        return t + 1, least_decayed(st), st

    _, _, state = lax.while_loop(cond, body, (jnp.int32(0), least_decayed(state), state))
    upper_q = lax.broadcasted_iota(jnp.int32, (tq, LANES), 1) >= SB_HEAD_DIM
    for lb in range(2):
        acc = state[lb][1]
        o_ref[j * tq:(j + 1) * tq, lb * LANES:(lb + 1) * LANES] = jnp.where(
            upper_q, acc[tq:2 * tq], acc[0:tq])


def _attn_call(qa, kv, dq, dk, dv, tri, pw, l, batch, seq):
    tq = MLA_TQ
    assert tq & (tq - 1) == 0 and PROJ_ROWS % tq == 0 and tq % SB_TQ == 0
    nq = seq // tq
    rows = MLA_HEADS * tq
    per_seq = lambda a, w: a.reshape(batch, seq, w)
    whole = lambda w: pl.BlockSpec((None, seq, w), lambda b, i: (b, 0, 0))
    tile = lambda w: pl.BlockSpec((None, tq, w), lambda b, i: (b, i, 0))
    out_mla, out_sb = pl.pallas_call(
        _attn_kernel,
        grid=(batch, nq),
        in_specs=[pl.BlockSpec((rows, LAT_WIDTH), lambda b, i: (b * nq + i, 0)),
                  whole(LAT_WIDTH), _layer_spec(pw["wuv"], l),
                  tile(256), whole(256), whole(256),
                  pl.BlockSpec((2 * SB_TK, SB_TK), lambda b, i: (0, 0))],
        out_specs=[tile(256), tile(256)],
        out_shape=[jax.ShapeDtypeStruct((batch, seq, 256), jnp.float32)] * 2,
        scratch_shapes=[pltpu.VMEM((rows, LANES), jnp.float32),
                        pltpu.VMEM((2, rows, LAT_WIDTH), jnp.float32),
                        pltpu.SMEM((1,), jnp.int32)],
        compiler_params=pltpu.CompilerParams(
            dimension_semantics=("arbitrary", "arbitrary"),
            vmem_limit_bytes=VMEM_LIMIT),
        name="attn",
    )(qa, per_seq(kv, LAT_WIDTH), pw["wuv"], per_seq(dq, 256), per_seq(dk, 256),
      per_seq(dv, 256), tri)
    return out_mla.reshape(batch * seq, 256), out_sb.reshape(batch * seq, 256)


def _out_kernel(sink_ref, x_ref, xn_ref, aq_ref, ak_ref, akp_ref, av_ref, avp_ref, yb_ref, yc_ref,
                yd_ref, win_ref, ggrp_ref, wout_ref, gpost_ref, o_ref, wgate_ref, woutb_ref, *,
                layer, tiles_per_seq):
    bf16 = jnp.bfloat16

    @pl.when(pl.program_id(0) == 0)
    def _():
        for c in range(0, D_MIX, 256):
            lo = _O_GATE - GATE_SLAB + c
            wgate_ref[:, c:c + 256] = win_ref[lo:lo + 256, :].T.astype(bf16)
        woutb_ref[...] = wout_ref[...].astype(bf16)

    first_tile = (pl.program_id(0) % tiles_per_seq) == 0
    ya = _swa_tile(sink_ref, aq_ref, ak_ref, akp_ref, av_ref, avp_ref, layer, first_tile)
    gates = _dot(xn_ref[...], wgate_ref[...])
    ys = []
    for g, y_in in enumerate((ya, yb_ref, yc_ref, yd_ref)):
        sl = slice(g * GROUP_WIDTH, (g + 1) * GROUP_WIDTH)
        gate = gates[:, sl]
        y = _rms(y_in[...], ggrp_ref[layer:layer + 1, sl]) * (gate * jax.nn.sigmoid(gate))
        ys.append(y.astype(bf16))
    d = _dot(jnp.concatenate(ys, axis=1), woutb_ref[...])
    o_ref[...] = x_ref[...] + _rms(d, _row(gpost_ref, layer))


def _out_call(x2, xn, sinks, aq, ak, av, yb, yc, yd, pw, l, seq):
    n = x2.shape[0]
    tm = OUT_ROWS
    per = tm // BLOCK
    row = lambda w: pl.BlockSpec((tm, w), lambda i: (i, 0))
    prev = lambda w: pl.BlockSpec((BLOCK, w), lambda i: (jnp.maximum(i * per - 1, 0), 0))
    names = ("win", "ggrp", "wout", "gpost")
    params = [pw[k] for k in names]
    gate_slab = pl.BlockSpec((None, D_IN - GATE_SLAB, D_MODEL), lambda i: (l, 1, 0),
                             pipeline_mode=pl.Buffered(1))
    return pl.pallas_call(
        functools.partial(_out_kernel, layer=l, tiles_per_seq=seq // tm),
        grid=(n // tm,),
        in_specs=[pl.BlockSpec(memory_space=pltpu.SMEM), row(D_MODEL), row(D_MODEL), row(256),
                  row(256), prev(256), row(256), prev(256), row(256), row(256), row(256)]
        + [gate_slab if k == "win" else _layer_spec(a, l, single=(k == "wout"))
           for k, a in zip(names, params)],
        out_specs=row(D_MODEL),
        out_shape=jax.ShapeDtypeStruct((n, D_MODEL), jnp.float32),
        scratch_shapes=[pltpu.VMEM((D_MODEL, D_MIX), jnp.bfloat16),
                        pltpu.VMEM((D_MIX, D_MODEL), jnp.bfloat16)],
        compiler_params=pltpu.CompilerParams(dimension_semantics=("arbitrary",),
                                             vmem_limit_bytes=VMEM_LIMIT),
        name="out",
    )(sinks, x2, xn, aq, ak, ak, av, av, yb, yc, yd, *params)


def _rope_swap(w):
    half = MLA_ROPE_DIM // 2
    return jnp.concatenate([-w[..., half:], w[..., :half]], axis=-1)


def _prep_weights(norm_pre, w_in, conv_w, conv_b, mla_q_norm, mla_w_uq, mla_kv_norm, mla_w_ukv,
                  group_norm, w_out, norm_post):
    f32, bf16 = jnp.float32, jnp.bfloat16
    depth = w_in.shape[0]
    dqk = MLA_NOPE_DIM + MLA_ROPE_DIM
    uq = mla_w_uq.reshape(depth, MLA_Q_RANK, MLA_HEADS, dqk)
    wuqn = uq[..., :MLA_NOPE_DIM].reshape(depth, MLA_Q_RANK, -1).astype(bf16)
    wuqr = jnp.concatenate(
        [uq[..., MLA_NOPE_DIM:].reshape(depth, MLA_Q_RANK, -1),
         _rope_swap(uq[..., MLA_NOPE_DIM:]).reshape(depth, MLA_Q_RANK, -1)], axis=-1).astype(bf16)
    ukv = mla_w_ukv.reshape(depth, MLA_KV_RANK, MLA_HEADS, MLA_NOPE_DIM + MLA_V_DIM)
    own_head = jnp.eye(MLA_HEADS, dtype=f32)[None, :, None, :, None]
    uk_t = jnp.transpose(ukv[..., :MLA_NOPE_DIM], (0, 2, 3, 1))
    wabs = (uk_t[:, :, :, None, :] * own_head).reshape(
        depth, MLA_HEADS * MLA_NOPE_DIM, MLA_HEADS * MLA_KV_RANK).astype(bf16)
    uv = jnp.transpose(ukv[..., MLA_NOPE_DIM:], (0, 2, 1, 3))
    wuv = (uv[:, :, :, None, :] * own_head).reshape(
        depth, MLA_HEADS, MLA_KV_RANK, MLA_HEADS * MLA_V_DIM).astype(bf16)

    return dict(
        gpre=norm_pre, win=jnp.swapaxes(w_in, 1, 2), convw=conv_w, convb=conv_b,
        gcq=mla_q_norm, wuqn=wuqn, wabs=wabs, wuqr=wuqr, gckv=mla_kv_norm, wuv=wuv,
        ggrp=group_norm, wout=w_out, gpost=norm_post)


def kernel(x, positions, norm_pre, w_in, attn_sinks, conv_w, conv_b, mla_q_norm, mla_w_uq,
           mla_kv_norm, mla_w_ukv, group_norm, w_out, norm_post):
    batch, seq, _ = x.shape
    depth = w_in.shape[0]
    assert seq % max(PROJ_ROWS, MLA_TQ, SB_TQ, OUT_ROWS) == 0 and OUT_ROWS % BLOCK == 0
    assert MLA_TQ == MLA_TK and SB_TQ == SB_TK and SB_TQ & (SB_TQ - 1) == 0
    n = batch * seq
    x2 = x.reshape(n, D_MODEL)
    tri = jnp.asarray(np.tile(np.tril(np.ones((SB_TK, SB_TK), np.float32), -1), (2, 1)),
                      jnp.bfloat16)
    pw = _prep_weights(norm_pre, w_in, conv_w, conv_b, mla_q_norm, mla_w_uq, mla_kv_norm,
                       mla_w_ukv, group_norm, w_out, norm_post)
    rope = _rope_call(positions)
    for l in range(depth):
        p = _proj_call(x2, rope, pw, l, seq)
        yc, yd = _attn_call(p["qa"], p["kv"], p["dq"], p["dk"], p["dv"], tri, pw, l, batch, seq)
        x2 = _out_call(x2, p["xn"], attn_sinks, p["aq"], p["ak"], p["av"], p["yb"], yc, yd, pw, l,
                       seq)
    return x2.reshape(batch, seq, D_MODEL)
```

```python
import functools

import numpy as np
import jax
import jax.numpy as jnp
from jax import lax
from jax.experimental import pallas as pl
from jax.experimental.pallas import tpu as pltpu

D_MODEL = 1024
BLOCK = 128
NORM_EPS = 1e-6
SWA_HEAD_DIM = 64
CONV_WIDTH = 256
MLA_HEADS = 4
MLA_Q_RANK = 256
MLA_KV_RANK = 128
MLA_NOPE_DIM = 64
MLA_ROPE_DIM = 32
MLA_V_DIM = 64
ROPE_THETA = 10000.0
SB_HEAD_DIM = 64
GROUP_WIDTH = 256
N_GROUPS = 4
D_MIX = GROUP_WIDTH * N_GROUPS

LANES = 128
LAT_WIDTH = 256
ONES_LANE = 160
MLA_JUMP = 32.0
MLA_MIN_SUM = 2.0 ** -64
NEG_BIG = -1e30
SB_DEAD = -160.0

_IN_SIZES = (256, 128, 128, 256, 256, 256, 256, 128, 32, 256, 256, 256, 1024)
_IN_OFF = np.concatenate([[0], np.cumsum(_IN_SIZES)]).astype(int)
(_O_AQ, _O_AK, _O_AV, _O_BB, _O_BC, _O_BX, _O_CQ, _O_CKV, _O_CKR, _O_DQ, _O_DK, _O_DV, _O_GATE,
 _O_END) = [int(v) for v in _IN_OFF]

D_IN = _O_END
GATE_SLAB = D_IN // 2
assert D_IN % 2 == 0 and GATE_SLAB % 8 == 0 and GATE_SLAB <= _O_GATE
P_KR_END = _O_CKR + LANES
P_DQ, P_DK, P_DV, P_END = P_KR_END, P_KR_END + 256, P_KR_END + 512, P_KR_END + 768

ROPE_ROWS = 1024
PROJ_ROWS = 1024
MLA_TQ = 512
MLA_TK = 512
SB_TQ = 256
SB_TK = 256
OUT_ROWS = 1024
VMEM_LIMIT = 56 * 1024 * 1024


def _rms(v, g):
    return v * lax.rsqrt(jnp.mean(v * v, axis=-1, keepdims=True) + NORM_EPS) * g


def _dot(a, b):
    return jnp.dot(a, b, preferred_element_type=jnp.float32)


def _dot_nt(a, b):
    return lax.dot_general(a, b, (((1,), (1,)), ((), ())), preferred_element_type=jnp.float32)


def _layer_spec(a, l, single=False):
    if a.ndim == 2:
        return pl.BlockSpec(a.shape, lambda *_: (0, 0))
    zeros = (0,) * (a.ndim - 1)
    mode = dict(pipeline_mode=pl.Buffered(1)) if single else {}
    return pl.BlockSpec((None,) + a.shape[1:], lambda *_: (l,) + zeros, **mode)


def _row(ref, layer):
    return ref[layer:layer + 1, :]


def _rope_kernel(pos_ref, freq_ref, cos_ref, sin_ref):
    ang = freq_ref[...] * pos_ref[...]
    copies = LANES // ang.shape[0]
    for ref, table in ((cos_ref, jnp.cos(ang)), (sin_ref, jnp.sin(ang))):
        ref[...] = jnp.concatenate([table] * copies, axis=0).T


def _rope_call(positions):
    n = positions.size
    tm = ROPE_ROWS
    half = MLA_ROPE_DIM // 2
    freqs = (ROPE_THETA ** (-np.arange(half, dtype=np.float32) / half)).astype(np.float32)
    table = jax.ShapeDtypeStruct((n, LANES), jnp.float32)
    return pl.pallas_call(
        _rope_kernel,
        grid=(n // tm,),
        in_specs=[pl.BlockSpec((None, 1, tm), lambda i: (i, 0, 0)),
                  pl.BlockSpec((half, 1), lambda i: (0, 0))],
        out_specs=[pl.BlockSpec((tm, LANES), lambda i: (i, 0))] * 2,
        out_shape=[table, table],
        compiler_params=pltpu.CompilerParams(dimension_semantics=("arbitrary",)),
        name="rope",
    )(positions.astype(jnp.float32).reshape(n // tm, 1, tm), jnp.asarray(freqs.reshape(half, 1)))


def _proj_kernel(x_ref, cos_ref, sin_ref, gpre_ref, win_ref, convw_ref, convb_ref, gcq_ref,
                 wuqn_ref, wabs_ref, wuqr_ref, gckv_ref,
                 xn_ref, aq_ref, ak_ref, av_ref, yb_ref, qa_ref, kv_ref, dq_ref, dk_ref, dv_ref,
                 wbf_ref, ubuf_ref, *, layer, tiles_per_seq):
    tm = x_ref.shape[0]
    bf16 = jnp.bfloat16

    @pl.when(pl.program_id(0) == 0)
    def _():
        def put(dst, src, scale=None):
            blk = win_ref[src:src + 256, :].T
            blk = blk if scale is None else blk * scale
            wbf_ref[:, dst:dst + 256] = blk.astype(bf16)

        put(_O_AQ, _O_AQ, SWA_HEAD_DIM ** -0.5 * np.log2(np.e))
        for c in range(_O_AK, P_KR_END, 256):
            put(c, c)
        put(P_DQ, _O_DQ, SB_HEAD_DIM ** -0.5 * np.log2(np.e))
        put(P_DK, _O_DK)
        put(P_DV, _O_DV)

    xb = _rms(x_ref[...], _row(gpre_ref, layer)).astype(bf16)
    xn_ref[...] = xb

    def proj(lo, hi):
        return _dot(xb, wbf_ref[:, lo:hi])

    aq_ref[...] = proj(_O_AQ, _O_AK).astype(bf16)
    akv = proj(_O_AK, _O_BB)
    for ref, blk in ((ak_ref, akv[:, 0:LANES]), (av_ref, akv[:, LANES:2 * LANES])):
        ref[:, 0:LANES] = blk.astype(bf16)
        ref[:, LANES:2 * LANES] = pltpu.roll(blk, SWA_HEAD_DIM, axis=1).astype(bf16)

    dq_ref[...] = proj(P_DQ, P_DK).astype(bf16)
    dk_ref[...] = proj(P_DK, P_DV).astype(bf16)
    dv_ref[...] = proj(P_DV, P_END).astype(bf16)

    ckr = proj(_O_CKV, P_KR_END)

    u = proj(_O_BC, _O_BX) * proj(_O_BX, _O_CQ)

    @pl.when(pl.program_id(0) % tiles_per_seq == 0)
    def _():
        ubuf_ref[0:8, :] = jnp.zeros((8, CONV_WIDTH), jnp.float32)

    ubuf_ref[8:tm + 8, :] = u
    u1 = ubuf_ref[7:tm + 7, :]
    u2 = ubuf_ref[6:tm + 6, :]
    conv = (convw_ref[0:1, :] * u2 + convw_ref[1:2, :] * u1 + convw_ref[2:3, :] * u
            + _row(convb_ref, layer))
    yb_ref[...] = proj(_O_BB, _O_BC) * conv
    ubuf_ref[0:8, :] = ubuf_ref[tm:tm + 8, :]

    cosv = cos_ref[...]
    sinv = sin_ref[...]
    lane = lax.broadcasted_iota(jnp.int32, (tm, LANES), 1)
    in_rope = lane < MLA_ROPE_DIM
    qscale = (MLA_NOPE_DIM + MLA_ROPE_DIM) ** -0.5 * np.log2(np.e)

    cq = _rms(proj(_O_CQ, _O_CKV), _row(gcq_ref, layer)).astype(bf16)
    qnope = _dot(cq, wuqn_ref[...]).astype(bf16)
    qlat = _dot(qnope, wabs_ref[...])
    qr = _dot(cq, wuqr_ref[...])
    qroped = (qr[:, 0:LANES] * cosv + qr[:, LANES:2 * LANES] * sinv) * qscale
    for h in range(MLA_HEADS):
        lat_h = (qlat[:, h * LANES:(h + 1) * LANES] * qscale).astype(bf16)
        mine = qroped if h == 0 else pltpu.roll(qroped, LANES - h * MLA_ROPE_DIM, axis=1)
        rope_h = jnp.where(in_rope, mine, 0.0).astype(bf16)
        for t in range(tm // MLA_TQ):
            src = slice(t * MLA_TQ, (t + 1) * MLA_TQ)
            dst = slice((t * MLA_HEADS + h) * MLA_TQ, (t * MLA_HEADS + h + 1) * MLA_TQ)
            qa_ref[dst, 0:LANES] = lat_h[src]
            qa_ref[dst, LANES:2 * LANES] = rope_h[src]
    kv_ref[:, 0:LANES] = _rms(ckr[:, 0:LANES], _row(gckv_ref, layer)).astype(bf16)
    kr = ckr[:, LANES:2 * LANES]
    half = MLA_ROPE_DIM // 2
    partner = jnp.where(lane < half, -pltpu.roll(kr, LANES - half, axis=1),
                        pltpu.roll(kr, half, axis=1))
    krope = jnp.where(in_rope, kr * cosv + partner * sinv, 0.0)
    kv_ref[:, LANES:2 * LANES] = jnp.where(lane == ONES_LANE - LANES, 1.0, krope).astype(bf16)


def _proj_call(x2, rope, pw, l, seq):
    n = x2.shape[0]
    tm = PROJ_ROWS
    row = lambda w: pl.BlockSpec((tm, w), lambda i: (i, 0))
    bf16 = jnp.bfloat16
    outs = [("xn", 1, D_MODEL, bf16), ("aq", 1, 256, bf16), ("ak", 1, 256, bf16),
            ("av", 1, 256, bf16),
            ("yb", 1, 256, jnp.float32), ("qa", MLA_HEADS, LAT_WIDTH, bf16),
            ("kv", 1, LAT_WIDTH, bf16), ("dq", 1, 256, bf16), ("dk", 1, 256, bf16),
            ("dv", 1, 256, bf16)]
    params = [pw[k] for k in ("gpre", "win", "convw", "convb", "gcq", "wuqn", "wabs", "wuqr",
                              "gckv")]
    res = pl.pallas_call(
        functools.partial(_proj_kernel, layer=l, tiles_per_seq=seq // tm),
        grid=(n // tm,),
        in_specs=[row(D_MODEL), row(LANES), row(LANES)]
        + [_layer_spec(a, l, single=(k == "win")) for k, a in zip(
            ("gpre", "win", "convw", "convb", "gcq", "wuqn", "wabs", "wuqr", "gckv"), params)],
        out_specs=[pl.BlockSpec((r * tm, w), lambda i: (i, 0)) for _, r, w, _ in outs],
        out_shape=[jax.ShapeDtypeStruct((r * n, w), dt) for _, r, w, dt in outs],
        scratch_shapes=[pltpu.VMEM((D_MODEL, P_END), bf16),
                        pltpu.VMEM((tm + 8, CONV_WIDTH), jnp.float32)],
        compiler_params=pltpu.CompilerParams(dimension_semantics=("arbitrary",),
                                             vmem_limit_bytes=VMEM_LIMIT),
        name="proj",
    )(x2, *rope, *params)
    return {name: r for (name, _, _, _), r in zip(outs, res)}


def _swa_tile(sink_ref, q_ref, k_ref, kp_ref, v_ref, vp_ref, layer, first_tile):
    bf16 = jnp.bfloat16
    nsub = q_ref.shape[0] // BLOCK
    out = []
    row = lax.broadcasted_iota(jnp.int32, (BLOCK, 2 * BLOCK), 0)
    col = lax.broadcasted_iota(jnp.int32, (BLOCK, 2 * BLOCK), 1)
    band = (col > row) & (col <= row + BLOCK)
    band_first = band & ((col >= BLOCK) | jnp.logical_not(first_tile))
    upper = lax.broadcasted_iota(jnp.int32, (BLOCK, LANES), 1) >= SWA_HEAD_DIM
    for j in range(nsub):
        rows = slice(j * BLOCK, (j + 1) * BLOCK)
        if j == 0:
            kk = jnp.concatenate([kp_ref[...], k_ref[0:BLOCK, :]], axis=0)
            vv = jnp.concatenate([vp_ref[...], v_ref[0:BLOCK, :]], axis=0)
            mask = band_first
        else:
            kk = k_ref[(j - 1) * BLOCK:(j + 1) * BLOCK, :]
            vv = v_ref[(j - 1) * BLOCK:(j + 1) * BLOCK, :]
            mask = band
        blocks = []
        for lb in range(2):
            q2 = q_ref[rows, lb * LANES:(lb + 1) * LANES]
            halves = []
            for hh in range(2):
                sink = sink_ref[layer, 2 * lb + hh] * np.log2(np.e)
                qm = jnp.where(upper == (hh == 1), q2, jnp.zeros_like(q2))
                sel = slice(0, LANES) if lb == hh else slice(LANES, 2 * LANES)
                s = jnp.where(mask, _dot_nt(qm, kk[:, sel]), NEG_BIG)
                m = jnp.maximum(jnp.max(s, axis=1, keepdims=True), sink)
                p = jnp.exp2(s - m)
                den = jnp.sum(p, axis=1, keepdims=True) + jnp.exp2(sink - m)
                halves.append(_dot(p.astype(bf16), vv[:, sel]) / den)
            blocks.append(jnp.where(upper, halves[1], halves[0]))
        out.append(jnp.concatenate(blocks, axis=1))
    return jnp.concatenate(out, axis=0)


def _attn_kernel(q_ref, kv_ref, wuv_ref, dq_ref, dk_ref, dv_ref, tri_ref, o_ref, od_ref, m_ref,
                 acc_ref):
    tq, tk = MLA_TQ, MLA_TK
    bf16 = jnp.bfloat16
    rows = MLA_HEADS * tq
    i = pl.program_id(1)
    q = q_ref[...]

    def exact_update(off, masked):
        kv = kv_ref[pl.ds(off, tk), :]
        s = _dot_nt(q, kv)
        if masked:
            row = lax.broadcasted_iota(jnp.int32, (rows, tk), 0) & (tq - 1)
            col = lax.broadcasted_iota(jnp.int32, (rows, tk), 1)
            s = jnp.where(col <= row, s, NEG_BIG)
        m = m_ref[:, 0:1]
        m_new = jnp.maximum(m, jnp.max(s, axis=1, keepdims=True))
        p = jnp.exp2(s - m_new)
        acc_ref[...] = jnp.exp2(m - m_new) * acc_ref[...] + _dot(p.astype(bf16), kv)
        m_ref[...] = jnp.broadcast_to(m_new, (rows, LANES))

    def lazy_update(off, width):
        kv = kv_ref[pl.ds(off, width), :]
        d = _dot_nt(q, kv) - jnp.concatenate([m_ref[...]] * (width // LANES), axis=1)
        acc_new = acc_ref[...] + _dot(jnp.exp2(d).astype(bf16), kv)
        safe = jnp.max(d) <= MLA_JUMP

        @pl.when(safe)
        def _():
            acc_ref[...] = acc_new

        @pl.when(jnp.logical_not(safe))
        def _():
            def redo(j, carry):
                exact_update(pl.multiple_of(off + j * tk, tk), False)
                return carry
            lax.fori_loop(0, width // tk, redo, 0)

    off_d = pl.multiple_of(i * tk, tk)
    half = tk // 2
    kv_a = kv_ref[pl.ds(off_d, half), :]
    kv_b = kv_ref[pl.ds(pl.multiple_of(off_d + half, half), half), :]
    q_late = jnp.concatenate([q[h * tq + half:(h + 1) * tq] for h in range(MLA_HEADS)], axis=0)
    row_a = lax.broadcasted_iota(jnp.int32, (rows, half), 0) & (tq - 1)
    col_a = lax.broadcasted_iota(jnp.int32, (rows, half), 1)
    d_a = jnp.where(col_a <= row_a, _dot_nt(q, kv_a), NEG_BIG)
    row_b = lax.broadcasted_iota(jnp.int32, (rows // 2, half), 0) & (half - 1)
    col_b = lax.broadcasted_iota(jnp.int32, (rows // 2, half), 1)
    d_b = jnp.where(col_b <= row_b, _dot_nt(q_late, kv_b), NEG_BIG)
    acc_a = _dot(jnp.exp2(d_a).astype(bf16), kv_a)
    acc_b = _dot(jnp.exp2(d_b).astype(bf16), kv_b)
    ones_lane = lax.broadcasted_iota(jnp.int32, (half, LANES), 1) == ONES_LANE - LANES
    smallest = None
    for h in range(MLA_HEADS):
        early = acc_a[h * tq:h * tq + half]
        late = acc_a[h * tq + half:(h + 1) * tq] + acc_b[h * half:(h + 1) * half]
        acc_ref[h * tq:h * tq + half, :] = early
        acc_ref[h * tq + half:(h + 1) * tq, :] = late
        for part in (early, late):
            low = jnp.min(jnp.where(ones_lane, part[:, LANES:2 * LANES], 1.0))
            smallest = low if smallest is None else jnp.minimum(smallest, low)
    m_ref[...] = jnp.zeros((rows, LANES), jnp.float32)
    fine = jnp.logical_and(jnp.maximum(jnp.max(d_a), jnp.max(d_b)) <= MLA_JUMP,
                           smallest >= MLA_MIN_SUM)

    sb_tiles = [_sb_open(dq_ref, j, dk_ref, dv_ref, tri_ref, (tq // SB_TQ) * i + j, od_ref)
                for j in range(tq // SB_TQ)]

    @pl.when(jnp.logical_not(fine))
    def _():
        m_ref[...] = jnp.full((rows, LANES), NEG_BIG, jnp.float32)
        acc_ref[...] = jnp.zeros((rows, LAT_WIDTH), jnp.float32)
        exact_update(off_d, True)

    for j, opened in enumerate(sb_tiles):
        _sb_close(opened, (tq // SB_TQ) * i + j, od_ref, j)

    def quad(c, carry):
        lazy_update(pl.multiple_of(c * 4 * tk, tk), 4 * tk)
        return carry

    lax.fori_loop(0, i // 4, quad, 0)

    @pl.when(i % 4 >= 2)
    def _():
        lazy_update(pl.multiple_of((i // 4) * 4 * tk, tk), 2 * tk)

    @pl.when(i % 2 == 1)
    def _():
        lazy_update(pl.multiple_of((i - 1) * tk, tk), tk)

    acc = acc_ref[...]
    lat = (acc[:, 0:LANES] / acc[:, ONES_LANE:ONES_LANE + 1]).astype(bf16)
    heads_on_lanes = jnp.concatenate([lat[h * tq:(h + 1) * tq, :] for h in range(MLA_HEADS)], axis=1)
    o_ref[...] = _dot(heads_on_lanes, wuv_ref[...].reshape(MLA_HEADS * MLA_KV_RANK, -1))


def _sb_open(q_ref, j, k_ref, v_ref, tri_ref, i, o_ref):
    tq, tk = SB_TQ, SB_TK
    bf16 = jnp.bfloat16
    upper_q = lax.broadcasted_iota(jnp.int32, (tq, LANES), 1) >= SB_HEAD_DIM
    qs = []
    for lb in range(2):
        q2 = q_ref[j * tq:(j + 1) * tq, lb * LANES:(lb + 1) * LANES]
        zero = jnp.zeros_like(q2)
        qs.append(jnp.concatenate([jnp.where(upper_q, zero, q2), jnp.where(upper_q, q2, zero)],
                                  axis=0))
    row = lax.broadcasted_iota(jnp.int32, (2 * tq, tk), 0) & (tq - 1)
    col = lax.broadcasted_iota(jnp.int32, (2 * tq, tk), 1)
    strict = col < row
    tri = tri_ref[...]

    def step(c, state, masked):
        off = pl.multiple_of(c * tk, tk)
        new = []
        for lb in range(2):
            csum, acc = state[lb]
            k2 = k_ref[pl.ds(off, tk), lb * LANES:(lb + 1) * LANES]
            v2 = v_ref[pl.ds(off, tk), lb * LANES:(lb + 1) * LANES]
            z = _dot_nt(qs[lb], k2)
            if masked:
                z = jnp.where(strict, z, NEG_BIG)
            log_beta = jnp.minimum(z, 0.0) - jnp.log2(1.0 + jnp.exp2(-jnp.abs(z)))
            log_keep = log_beta - z
            hi = log_keep.astype(bf16)
            lo = (log_keep - hi.astype(jnp.float32)).astype(bf16)
            within = _dot(jnp.concatenate([hi, lo], axis=1), tri)
            a = jnp.exp2(log_beta + (csum + within))
            acc = acc + _dot(a.astype(bf16), v2)
            csum = csum + jnp.sum(log_keep, axis=1, keepdims=True)
            new.append((csum, acc))
        return tuple(new)

    init = tuple((jnp.zeros((2 * tq, 1), jnp.float32), jnp.zeros((2 * tq, LANES), jnp.float32))
                 for _ in range(2))
    state = step(i, init, True)
    gone = jnp.where(i > 0, 0.0, NEG_BIG)
    state = step(jnp.maximum(i - 1, 0), tuple((csum + gone, acc) for csum, acc in state), False)
    _sb_write(state, o_ref, j)
    return step, state, _sb_least_decayed(state)


def _sb_least_decayed(state):
    return jnp.max(jnp.maximum(state[0][0], state[1][0]))


def _sb_write(state, o_ref, j):
    tq = SB_TQ
    upper_q = lax.broadcasted_iota(jnp.int32, (tq, LANES), 1) >= SB_HEAD_DIM
    for lb in range(2):
        acc = state[lb][1]
        o_ref[j * tq:(j + 1) * tq, lb * LANES:(lb + 1) * LANES] = jnp.where(
            upper_q, acc[tq:2 * tq], acc[0:tq])


def _sb_close(opened, i, o_ref, j):
    step, state, worst = opened

    def cond(carry):
        t, worst, _ = carry
        return jnp.logical_and(t < i - 1, worst > SB_DEAD)

    def body(carry):
        t, _, st = carry
        st = step(i - 2 - t, st, False)
        return t + 1, _sb_least_decayed(st), st

    @pl.when(cond((jnp.int32(0), worst, None)))
    def _():
        _, _, walked = lax.while_loop(cond, body, (jnp.int32(0), worst, state))
        _sb_write(walked, o_ref, j)


def _attn_call(qa, kv, dq, dk, dv, tri, pw, l, batch, seq):
    tq = MLA_TQ
    assert tq & (tq - 1) == 0 and PROJ_ROWS % tq == 0 and tq % SB_TQ == 0
    nq = seq // tq
    rows = MLA_HEADS * tq
    per_seq = lambda a, w: a.reshape(batch, seq, w)
    whole = lambda w: pl.BlockSpec((None, seq, w), lambda b, i: (b, 0, 0))
    tile = lambda w: pl.BlockSpec((None, tq, w), lambda b, i: (b, i, 0))
    out_mla, out_sb = pl.pallas_call(
        _attn_kernel,
        grid=(batch, nq),
        in_specs=[pl.BlockSpec((rows, LAT_WIDTH), lambda b, i: (b * nq + i, 0)),
                  whole(LAT_WIDTH), _layer_spec(pw["wuv"], l),
                  tile(256), whole(256), whole(256),
                  pl.BlockSpec((2 * SB_TK, SB_TK), lambda b, i: (0, 0))],
        out_specs=[tile(256), tile(256)],
        out_shape=[jax.ShapeDtypeStruct((batch, seq, 256), jnp.float32)] * 2,
        scratch_shapes=[pltpu.VMEM((rows, LANES), jnp.float32),
                        pltpu.VMEM((rows, LAT_WIDTH), jnp.float32)],
        compiler_params=pltpu.CompilerParams(
            dimension_semantics=("arbitrary", "arbitrary"),
            vmem_limit_bytes=VMEM_LIMIT),
        name="attn",
    )(qa, per_seq(kv, LAT_WIDTH), pw["wuv"], per_seq(dq, 256), per_seq(dk, 256),
      per_seq(dv, 256), tri)
    return out_mla.reshape(batch * seq, 256), out_sb.reshape(batch * seq, 256)


def _out_kernel(sink_ref, x_ref, xn_ref, aq_ref, ak_ref, akp_ref, av_ref, avp_ref, yb_ref, yc_ref,
                yd_ref, win_ref, ggrp_ref, wout_ref, gpost_ref, o_ref, wgate_ref, woutb_ref, *,
                layer, tiles_per_seq):
    bf16 = jnp.bfloat16

    @pl.when(pl.program_id(0) == 0)
    def _():
        for c in range(0, D_MIX, 256):
            lo = _O_GATE - GATE_SLAB + c
            wgate_ref[:, c:c + 256] = win_ref[lo:lo + 256, :].T.astype(bf16)
        woutb_ref[...] = wout_ref[...].astype(bf16)

    first_tile = (pl.program_id(0) % tiles_per_seq) == 0
    ya = _swa_tile(sink_ref, aq_ref, ak_ref, akp_ref, av_ref, avp_ref, layer, first_tile)
    gates = _dot(xn_ref[...], wgate_ref[...])
    ys = []
    for g, y_in in enumerate((ya, yb_ref, yc_ref, yd_ref)):
        sl = slice(g * GROUP_WIDTH, (g + 1) * GROUP_WIDTH)
        gate = gates[:, sl]
        y = _rms(y_in[...], ggrp_ref[layer:layer + 1, sl]) * (gate * jax.nn.sigmoid(gate))
        ys.append(y.astype(bf16))
    d = _dot(jnp.concatenate(ys, axis=1), woutb_ref[...])
    o_ref[...] = x_ref[...] + _rms(d, _row(gpost_ref, layer))


def _out_call(x2, xn, sinks, aq, ak, av, yb, yc, yd, pw, l, seq):
    n = x2.shape[0]
    tm = OUT_ROWS
    per = tm // BLOCK
    row = lambda w: pl.BlockSpec((tm, w), lambda i: (i, 0))
    prev = lambda w: pl.BlockSpec((BLOCK, w), lambda i: (jnp.maximum(i * per - 1, 0), 0))
    names = ("win", "ggrp", "wout", "gpost")
    params = [pw[k] for k in names]
    gate_slab = pl.BlockSpec((None, D_IN - GATE_SLAB, D_MODEL), lambda i: (l, 1, 0),
                             pipeline_mode=pl.Buffered(1))
    return pl.pallas_call(
        functools.partial(_out_kernel, layer=l, tiles_per_seq=seq // tm),
        grid=(n // tm,),
        in_specs=[pl.BlockSpec(memory_space=pltpu.SMEM), row(D_MODEL), row(D_MODEL), row(256),
                  row(256), prev(256), row(256), prev(256), row(256), row(256), row(256)]
        + [gate_slab if k == "win" else _layer_spec(a, l, single=(k == "wout"))
           for k, a in zip(names, params)],
        out_specs=row(D_MODEL),
        out_shape=jax.ShapeDtypeStruct((n, D_MODEL), jnp.float32),
        scratch_shapes=[pltpu.VMEM((D_MODEL, D_MIX), jnp.bfloat16),
                        pltpu.VMEM((D_MIX, D_MODEL), jnp.bfloat16)],
        compiler_params=pltpu.CompilerParams(dimension_semantics=("arbitrary",),
                                             vmem_limit_bytes=VMEM_LIMIT),
        name="out",
    )(sinks, x2, xn, aq, ak, ak, av, av, yb, yc, yd, *params)


def _rope_swap(w):
    half = MLA_ROPE_DIM // 2
    return jnp.concatenate([-w[..., half:], w[..., :half]], axis=-1)


def _prep_weights(norm_pre, w_in, conv_w, conv_b, mla_q_norm, mla_w_uq, mla_kv_norm, mla_w_ukv,
                  group_norm, w_out, norm_post):
    f32, bf16 = jnp.float32, jnp.bfloat16
    depth = w_in.shape[0]
    dqk = MLA_NOPE_DIM + MLA_ROPE_DIM
    uq = mla_w_uq.reshape(depth, MLA_Q_RANK, MLA_HEADS, dqk)
    wuqn = uq[..., :MLA_NOPE_DIM].reshape(depth, MLA_Q_RANK, -1).astype(bf16)
    wuqr = jnp.concatenate(
        [uq[..., MLA_NOPE_DIM:].reshape(depth, MLA_Q_RANK, -1),
         _rope_swap(uq[..., MLA_NOPE_DIM:]).reshape(depth, MLA_Q_RANK, -1)], axis=-1).astype(bf16)
    ukv = mla_w_ukv.reshape(depth, MLA_KV_RANK, MLA_HEADS, MLA_NOPE_DIM + MLA_V_DIM)
    own_head = jnp.eye(MLA_HEADS, dtype=f32)[None, :, None, :, None]
    uk_t = jnp.transpose(ukv[..., :MLA_NOPE_DIM], (0, 2, 3, 1))
    wabs = (uk_t[:, :, :, None, :] * own_head).reshape(
        depth, MLA_HEADS * MLA_NOPE_DIM, MLA_HEADS * MLA_KV_RANK).astype(bf16)
    uv = jnp.transpose(ukv[..., MLA_NOPE_DIM:], (0, 2, 1, 3))
    wuv = (uv[:, :, :, None, :] * own_head).reshape(
        depth, MLA_HEADS, MLA_KV_RANK, MLA_HEADS * MLA_V_DIM).astype(bf16)

    return dict(
        gpre=norm_pre, win=jnp.swapaxes(w_in, 1, 2), convw=conv_w, convb=conv_b,
        gcq=mla_q_norm, wuqn=wuqn, wabs=wabs, wuqr=wuqr, gckv=mla_kv_norm, wuv=wuv,
        ggrp=group_norm, wout=w_out, gpost=norm_post)


def kernel(x, positions, norm_pre, w_in, attn_sinks, conv_w, conv_b, mla_q_norm, mla_w_uq,
           mla_kv_norm, mla_w_ukv, group_norm, w_out, norm_post):
    batch, seq, _ = x.shape
    depth = w_in.shape[0]
    assert seq % max(PROJ_ROWS, MLA_TQ, SB_TQ, OUT_ROWS) == 0 and OUT_ROWS % BLOCK == 0
    assert MLA_TQ == MLA_TK and SB_TQ == SB_TK and SB_TQ & (SB_TQ - 1) == 0
    n = batch * seq
    x2 = x.reshape(n, D_MODEL)
    tri = jnp.asarray(np.tile(np.tril(np.ones((SB_TK, SB_TK), np.float32), -1), (2, 1)),
                      jnp.bfloat16)
    pw = _prep_weights(norm_pre, w_in, conv_w, conv_b, mla_q_norm, mla_w_uq, mla_kv_norm,
                       mla_w_ukv, group_norm, w_out, norm_post)
    rope = _rope_call(positions)
    for l in range(depth):
        p = _proj_call(x2, rope, pw, l, seq)
        yc, yd = _attn_call(p["qa"], p["kv"], p["dq"], p["dk"], p["dv"], tri, pw, l, batch, seq)
        x2 = _out_call(x2, p["xn"], attn_sinks, p["aq"], p["ak"], p["av"], p["yb"], yc, yd, pw, l,
                       seq)
    return x2.reshape(batch, seq, D_MODEL)
```

```python
import functools

import numpy as np
import jax
import jax.numpy as jnp
from jax import lax
from jax.experimental import pallas as pl
from jax.experimental.pallas import tpu as pltpu

D_MODEL = 1024
BLOCK = 128
NORM_EPS = 1e-6
SWA_HEAD_DIM = 64
CONV_WIDTH = 256
MLA_HEADS = 4
MLA_Q_RANK = 256
MLA_KV_RANK = 128
MLA_NOPE_DIM = 64
MLA_ROPE_DIM = 32
MLA_V_DIM = 64
ROPE_THETA = 10000.0
SB_HEAD_DIM = 64
GROUP_WIDTH = 256
N_GROUPS = 4
D_MIX = GROUP_WIDTH * N_GROUPS

LANES = 128
LAT_WIDTH = 256
ONES_LANE = 160
MLA_JUMP = 32.0
MLA_MIN_SUM = 2.0 ** -64
NEG_BIG = -1e30
SB_DEAD = -160.0

_IN_SIZES = (256, 128, 128, 256, 256, 256, 256, 128, 32, 256, 256, 256, 1024)
_IN_OFF = np.concatenate([[0], np.cumsum(_IN_SIZES)]).astype(int)
(_O_AQ, _O_AK, _O_AV, _O_BB, _O_BC, _O_BX, _O_CQ, _O_CKV, _O_CKR, _O_DQ, _O_DK, _O_DV, _O_GATE,
 _O_END) = [int(v) for v in _IN_OFF]

D_IN = _O_END
GATE_SLAB = D_IN // 2
assert D_IN % 2 == 0 and GATE_SLAB % 8 == 0 and GATE_SLAB <= _O_GATE
P_KR_END = _O_CKR + LANES
P_DQ, P_DK, P_DV, P_END = P_KR_END, P_KR_END + 256, P_KR_END + 512, P_KR_END + 768

ROPE_ROWS = 1024
PROJ_ROWS = 1024
MLA_TQ = 512
MLA_TK = 512
SB_TQ = 256
SB_TK = 256
OUT_ROWS = 1024
VMEM_LIMIT = 56 * 1024 * 1024


def _rms(v, g):
    return v * lax.rsqrt(jnp.mean(v * v, axis=-1, keepdims=True) + NORM_EPS) * g


def _dot(a, b):
    return jnp.dot(a, b, preferred_element_type=jnp.float32)


def _dot_nt(a, b):
    return lax.dot_general(a, b, (((1,), (1,)), ((), ())), preferred_element_type=jnp.float32)


def _layer_spec(a, l, single=False):
    if a.ndim == 2:
        return pl.BlockSpec(a.shape, lambda *_: (0, 0))
    zeros = (0,) * (a.ndim - 1)
    mode = dict(pipeline_mode=pl.Buffered(1)) if single else {}
    return pl.BlockSpec((None,) + a.shape[1:], lambda *_: (l,) + zeros, **mode)


def _row(ref, layer):
    return ref[layer:layer + 1, :]


def _rope_kernel(pos_ref, freq_ref, cos_ref, sin_ref):
    ang = freq_ref[...] * pos_ref[...]
    copies = LANES // ang.shape[0]
    for ref, table in ((cos_ref, jnp.cos(ang)), (sin_ref, jnp.sin(ang))):
        ref[...] = jnp.concatenate([table] * copies, axis=0).T


def _rope_call(positions):
    n = positions.size
    tm = ROPE_ROWS
    half = MLA_ROPE_DIM // 2
    freqs = (ROPE_THETA ** (-np.arange(half, dtype=np.float32) / half)).astype(np.float32)
    table = jax.ShapeDtypeStruct((n, LANES), jnp.float32)
    return pl.pallas_call(
        _rope_kernel,
        grid=(n // tm,),
        in_specs=[pl.BlockSpec((None, 1, tm), lambda i: (i, 0, 0)),
                  pl.BlockSpec((half, 1), lambda i: (0, 0))],
        out_specs=[pl.BlockSpec((tm, LANES), lambda i: (i, 0))] * 2,
        out_shape=[table, table],
        compiler_params=pltpu.CompilerParams(dimension_semantics=("arbitrary",)),
        name="rope",
    )(positions.astype(jnp.float32).reshape(n // tm, 1, tm), jnp.asarray(freqs.reshape(half, 1)))


def _proj_kernel(x_ref, cos_ref, sin_ref, gpre_ref, win_ref, convw_ref, convb_ref, gcq_ref,
                 wuqn_ref, wabs_ref, wuqr_ref, gckv_ref,
                 xn_ref, aq_ref, ak_ref, av_ref, yb_ref, qa_ref, kv_ref, dq_ref, dk_ref, dv_ref,
                 wbf_ref, ubuf_ref, *, layer, tiles_per_seq):
    tm = x_ref.shape[0]
    bf16 = jnp.bfloat16

    @pl.when(pl.program_id(0) == 0)
    def _():
        def put(dst, src, scale=None):
            blk = win_ref[src:src + 256, :].T
            blk = blk if scale is None else blk * scale
            wbf_ref[:, dst:dst + 256] = blk.astype(bf16)

        put(_O_AQ, _O_AQ, SWA_HEAD_DIM ** -0.5 * np.log2(np.e))
        for c in range(_O_AK, P_KR_END, 256):
            put(c, c)
        put(P_DQ, _O_DQ, SB_HEAD_DIM ** -0.5 * np.log2(np.e))
        put(P_DK, _O_DK)
        put(P_DV, _O_DV)

    xb = _rms(x_ref[...], _row(gpre_ref, layer)).astype(bf16)
    xn_ref[...] = xb

    def proj(lo, hi):
        return _dot(xb, wbf_ref[:, lo:hi])

    aq_ref[...] = proj(_O_AQ, _O_AK).astype(bf16)
    akv = proj(_O_AK, _O_BB)
    for ref, blk in ((ak_ref, akv[:, 0:LANES]), (av_ref, akv[:, LANES:2 * LANES])):
        ref[:, 0:LANES] = blk.astype(bf16)
        ref[:, LANES:2 * LANES] = pltpu.roll(blk, SWA_HEAD_DIM, axis=1).astype(bf16)

    dq_ref[...] = proj(P_DQ, P_DK).astype(bf16)
    dk_ref[...] = proj(P_DK, P_DV).astype(bf16)
    dv_ref[...] = proj(P_DV, P_END).astype(bf16)

    ckr = proj(_O_CKV, P_KR_END)

    u = proj(_O_BC, _O_BX) * proj(_O_BX, _O_CQ)

    @pl.when(pl.program_id(0) % tiles_per_seq == 0)
    def _():
        ubuf_ref[0:8, :] = jnp.zeros((8, CONV_WIDTH), jnp.float32)

    ubuf_ref[8:tm + 8, :] = u
    u1 = ubuf_ref[7:tm + 7, :]
    u2 = ubuf_ref[6:tm + 6, :]
    conv = (convw_ref[0:1, :] * u2 + convw_ref[1:2, :] * u1 + convw_ref[2:3, :] * u
            + _row(convb_ref, layer))
    yb_ref[...] = proj(_O_BB, _O_BC) * conv
    ubuf_ref[0:8, :] = ubuf_ref[tm:tm + 8, :]

    cosv = cos_ref[...]
    sinv = sin_ref[...]
    lane = lax.broadcasted_iota(jnp.int32, (tm, LANES), 1)
    in_rope = lane < MLA_ROPE_DIM
    qscale = (MLA_NOPE_DIM + MLA_ROPE_DIM) ** -0.5 * np.log2(np.e)

    cq = _rms(proj(_O_CQ, _O_CKV), _row(gcq_ref, layer)).astype(bf16)
    qnope = _dot(cq, wuqn_ref[...]).astype(bf16)
    qlat = _dot(qnope, wabs_ref[...])
    qr = _dot(cq, wuqr_ref[...])
    qroped = (qr[:, 0:LANES] * cosv + qr[:, LANES:2 * LANES] * sinv) * qscale
    for h in range(MLA_HEADS):
        lat_h = (qlat[:, h * LANES:(h + 1) * LANES] * qscale).astype(bf16)
        mine = qroped if h == 0 else pltpu.roll(qroped, LANES - h * MLA_ROPE_DIM, axis=1)
        rope_h = jnp.where(in_rope, mine, 0.0).astype(bf16)
        for t in range(tm // MLA_TQ):
            src = slice(t * MLA_TQ, (t + 1) * MLA_TQ)
            dst = slice((t * MLA_HEADS + h) * MLA_TQ, (t * MLA_HEADS + h + 1) * MLA_TQ)
            qa_ref[dst, 0:LANES] = lat_h[src]
            qa_ref[dst, LANES:2 * LANES] = rope_h[src]
    kv_ref[:, 0:LANES] = _rms(ckr[:, 0:LANES], _row(gckv_ref, layer)).astype(bf16)
    kr = ckr[:, LANES:2 * LANES]
    half = MLA_ROPE_DIM // 2
    partner = jnp.where(lane < half, -pltpu.roll(kr, LANES - half, axis=1),
                        pltpu.roll(kr, half, axis=1))
    krope = jnp.where(in_rope, kr * cosv + partner * sinv, 0.0)
    kv_ref[:, LANES:2 * LANES] = jnp.where(lane == ONES_LANE - LANES, 1.0, krope).astype(bf16)


def _proj_call(x2, rope, pw, l, seq):
    n = x2.shape[0]
    tm = PROJ_ROWS
    row = lambda w: pl.BlockSpec((tm, w), lambda i: (i, 0))
    bf16 = jnp.bfloat16
    outs = [("xn", 1, D_MODEL, bf16), ("aq", 1, 256, bf16), ("ak", 1, 256, bf16),
            ("av", 1, 256, bf16),
            ("yb", 1, 256, jnp.float32), ("qa", MLA_HEADS, LAT_WIDTH, bf16),
            ("kv", 1, LAT_WIDTH, bf16), ("dq", 1, 256, bf16), ("dk", 1, 256, bf16),
            ("dv", 1, 256, bf16)]
    params = [pw[k] for k in ("gpre", "win", "convw", "convb", "gcq", "wuqn", "wabs", "wuqr",
                              "gckv")]
    res = pl.pallas_call(
        functools.partial(_proj_kernel, layer=l, tiles_per_seq=seq // tm),
        grid=(n // tm,),
        in_specs=[row(D_MODEL), row(LANES), row(LANES)]
        + [_layer_spec(a, l, single=(k == "win")) for k, a in zip(
            ("gpre", "win", "convw", "convb", "gcq", "wuqn", "wabs", "wuqr", "gckv"), params)],
        out_specs=[pl.BlockSpec((r * tm, w), lambda i: (i, 0)) for _, r, w, _ in outs],
        out_shape=[jax.ShapeDtypeStruct((r * n, w), dt) for _, r, w, dt in outs],
        scratch_shapes=[pltpu.VMEM((D_MODEL, P_END), bf16),
                        pltpu.VMEM((tm + 8, CONV_WIDTH), jnp.float32)],
        compiler_params=pltpu.CompilerParams(dimension_semantics=("arbitrary",),
                                             vmem_limit_bytes=VMEM_LIMIT),
        name="proj",
    )(x2, *rope, *params)
    return {name: r for (name, _, _, _), r in zip(outs, res)}


def _swa_tile(sink_ref, q_ref, k_ref, kp_ref, v_ref, vp_ref, layer, first_tile):
    bf16 = jnp.bfloat16
    nsub = q_ref.shape[0] // BLOCK
    out = []
    row = lax.broadcasted_iota(jnp.int32, (BLOCK, 2 * BLOCK), 0)
    col = lax.broadcasted_iota(jnp.int32, (BLOCK, 2 * BLOCK), 1)
    band = (col > row) & (col <= row + BLOCK)
    band_first = band & ((col >= BLOCK) | jnp.logical_not(first_tile))
    upper = lax.broadcasted_iota(jnp.int32, (BLOCK, LANES), 1) >= SWA_HEAD_DIM
    for j in range(nsub):
        rows = slice(j * BLOCK, (j + 1) * BLOCK)
        if j == 0:
            kk = jnp.concatenate([kp_ref[...], k_ref[0:BLOCK, :]], axis=0)
            vv = jnp.concatenate([vp_ref[...], v_ref[0:BLOCK, :]], axis=0)
            mask = band_first
        else:
            kk = k_ref[(j - 1) * BLOCK:(j + 1) * BLOCK, :]
            vv = v_ref[(j - 1) * BLOCK:(j + 1) * BLOCK, :]
            mask = band
        blocks = []
        for lb in range(2):
            q2 = q_ref[rows, lb * LANES:(lb + 1) * LANES]
            halves = []
            for hh in range(2):
                sink = sink_ref[layer, 2 * lb + hh] * np.log2(np.e)
                qm = jnp.where(upper == (hh == 1), q2, jnp.zeros_like(q2))
                sel = slice(0, LANES) if lb == hh else slice(LANES, 2 * LANES)
                s = jnp.where(mask, _dot_nt(qm, kk[:, sel]), NEG_BIG)
                m = jnp.maximum(jnp.max(s, axis=1, keepdims=True), sink)
                p = jnp.exp2(s - m)
                den = jnp.sum(p, axis=1, keepdims=True) + jnp.exp2(sink - m)
                halves.append(_dot(p.astype(bf16), vv[:, sel]) / den)
            blocks.append(jnp.where(upper, halves[1], halves[0]))
        out.append(jnp.concatenate(blocks, axis=1))
    return jnp.concatenate(out, axis=0)


def _attn_kernel(q_ref, kv_ref, wuv_ref, dq_ref, dk_ref, dv_ref, tri_ref, o_ref, od_ref, m_ref,
                 acc_ref):
    tq, tk = MLA_TQ, MLA_TK
    bf16 = jnp.bfloat16
    rows = MLA_HEADS * tq
    i = pl.program_id(1)
    q = q_ref[...]

    def exact_update(off, masked):
        kv = kv_ref[pl.ds(off, tk), :]
        s = _dot_nt(q, kv)
        if masked:
            row = lax.broadcasted_iota(jnp.int32, (rows, tk), 0) & (tq - 1)
            col = lax.broadcasted_iota(jnp.int32, (rows, tk), 1)
            s = jnp.where(col <= row, s, NEG_BIG)
        m = m_ref[:, 0:1]
        m_new = jnp.maximum(m, jnp.max(s, axis=1, keepdims=True))
        p = jnp.exp2(s - m_new)
        acc_ref[...] = jnp.exp2(m - m_new) * acc_ref[...] + _dot(p.astype(bf16), kv)
        m_ref[...] = jnp.broadcast_to(m_new, (rows, LANES))

    def lazy_update(off, width):
        kv = kv_ref[pl.ds(off, width), :]
        d = _dot_nt(q, kv) - jnp.concatenate([m_ref[...]] * (width // LANES), axis=1)
        acc_new = acc_ref[...] + _dot(jnp.exp2(d).astype(bf16), kv)
        safe = jnp.max(d) <= MLA_JUMP

        @pl.when(safe)
        def _():
            acc_ref[...] = acc_new

        @pl.when(jnp.logical_not(safe))
        def _():
            def redo(j, carry):
                exact_update(pl.multiple_of(off + j * tk, tk), False)
                return carry
            lax.fori_loop(0, width // tk, redo, 0)

    off_d = pl.multiple_of(i * tk, tk)
    half = tk // 2
    kv_a = kv_ref[pl.ds(off_d, half), :]
    kv_b = kv_ref[pl.ds(pl.multiple_of(off_d + half, half), half), :]
    q_late = jnp.concatenate([q[h * tq + half:(h + 1) * tq] for h in range(MLA_HEADS)], axis=0)
    row_a = lax.broadcasted_iota(jnp.int32, (rows, half), 0) & (tq - 1)
    col_a = lax.broadcasted_iota(jnp.int32, (rows, half), 1)
    d_a = jnp.where(col_a <= row_a, _dot_nt(q, kv_a), NEG_BIG)
    row_b = lax.broadcasted_iota(jnp.int32, (rows // 2, half), 0) & (half - 1)
    col_b = lax.broadcasted_iota(jnp.int32, (rows // 2, half), 1)
    d_b = jnp.where(col_b <= row_b, _dot_nt(q_late, kv_b), NEG_BIG)
    acc_a = _dot(jnp.exp2(d_a).astype(bf16), kv_a)
    acc_b = _dot(jnp.exp2(d_b).astype(bf16), kv_b)
    ones_lane = lax.broadcasted_iota(jnp.int32, (half, LANES), 1) == ONES_LANE - LANES
    smallest = None
    for h in range(MLA_HEADS):
        early = acc_a[h * tq:h * tq + half]
        late = acc_a[h * tq + half:(h + 1) * tq] + acc_b[h * half:(h + 1) * half]
        acc_ref[h * tq:h * tq + half, :] = early
        acc_ref[h * tq + half:(h + 1) * tq, :] = late
        for part in (early, late):
            low = jnp.min(jnp.where(ones_lane, part[:, LANES:2 * LANES], 1.0))
            smallest = low if smallest is None else jnp.minimum(smallest, low)
    m_ref[...] = jnp.zeros((rows, LANES), jnp.float32)
    fine = jnp.logical_and(jnp.maximum(jnp.max(d_a), jnp.max(d_b)) <= MLA_JUMP,
                           smallest >= MLA_MIN_SUM)

    sb_tiles = [_sb_open(dq_ref, j, dk_ref, dv_ref, tri_ref, (tq // SB_TQ) * i + j, od_ref)
                for j in range(tq // SB_TQ)]

    @pl.when(jnp.logical_not(fine))
    def _():
        m_ref[...] = jnp.full((rows, LANES), NEG_BIG, jnp.float32)
        acc_ref[...] = jnp.zeros((rows, LAT_WIDTH), jnp.float32)
        exact_update(off_d, True)

    for j, opened in enumerate(sb_tiles):
        _sb_close(opened, (tq // SB_TQ) * i + j, od_ref, j)

    def quad(c, carry):
        lazy_update(pl.multiple_of(c * 4 * tk, tk), 4 * tk)
        return carry

    lax.fori_loop(0, i // 4, quad, 0)

    for left in (1, 2, 3):
        @pl.when(i % 4 == left)
        def _():
            lazy_update(pl.multiple_of((i // 4) * 4 * tk, tk), left * tk)

    acc = acc_ref[...]
    lat = (acc[:, 0:LANES] / acc[:, ONES_LANE:ONES_LANE + 1]).astype(bf16)
    heads_on_lanes = jnp.concatenate([lat[h * tq:(h + 1) * tq, :] for h in range(MLA_HEADS)], axis=1)
    o_ref[...] = _dot(heads_on_lanes, wuv_ref[...].reshape(MLA_HEADS * MLA_KV_RANK, -1))


def _sb_open(q_ref, j, k_ref, v_ref, tri_ref, i, o_ref):
    tq, tk = SB_TQ, SB_TK
    bf16 = jnp.bfloat16
    upper_q = lax.broadcasted_iota(jnp.int32, (tq, LANES), 1) >= SB_HEAD_DIM
    qs = []
    for lb in range(2):
        q2 = q_ref[j * tq:(j + 1) * tq, lb * LANES:(lb + 1) * LANES]
        zero = jnp.zeros_like(q2)
        qs.append(jnp.concatenate([jnp.where(upper_q, zero, q2), jnp.where(upper_q, q2, zero)],
                                  axis=0))
    row = lax.broadcasted_iota(jnp.int32, (2 * tq, tk), 0) & (tq - 1)
    col = lax.broadcasted_iota(jnp.int32, (2 * tq, tk), 1)
    strict = col < row
    tri = tri_ref[...]

    def step(c, state, masked):
        off = pl.multiple_of(c * tk, tk)
        new = []
        for lb in range(2):
            csum, acc = state[lb]
            k2 = k_ref[pl.ds(off, tk), lb * LANES:(lb + 1) * LANES]
            v2 = v_ref[pl.ds(off, tk), lb * LANES:(lb + 1) * LANES]
            z = _dot_nt(qs[lb], k2)
            if masked:
                z = jnp.where(strict, z, NEG_BIG)
            log_beta = jnp.minimum(z, 0.0) - jnp.log2(1.0 + jnp.exp2(-jnp.abs(z)))
            log_keep = log_beta - z
            hi = log_keep.astype(bf16)
            lo = (log_keep - hi.astype(jnp.float32)).astype(bf16)
            within = _dot(jnp.concatenate([hi, lo], axis=1), tri)
            a = jnp.exp2(log_beta + (csum + within))
            acc = acc + _dot(a.astype(bf16), v2)
            csum = csum + jnp.sum(log_keep, axis=1, keepdims=True)
            new.append((csum, acc))
        return tuple(new)

    init = tuple((jnp.zeros((2 * tq, 1), jnp.float32), jnp.zeros((2 * tq, LANES), jnp.float32))
                 for _ in range(2))
    state = step(i, init, True)
    gone = jnp.where(i > 0, 0.0, NEG_BIG)
    state = step(jnp.maximum(i - 1, 0), tuple((csum + gone, acc) for csum, acc in state), False)
    _sb_write(state, o_ref, j)
    return step, state, _sb_least_decayed(state)


def _sb_least_decayed(state):
    return jnp.max(jnp.maximum(state[0][0], state[1][0]))


def _sb_write(state, o_ref, j):
    tq = SB_TQ
    upper_q = lax.broadcasted_iota(jnp.int32, (tq, LANES), 1) >= SB_HEAD_DIM
    for lb in range(2):
        acc = state[lb][1]
        o_ref[j * tq:(j + 1) * tq, lb * LANES:(lb + 1) * LANES] = jnp.where(
            upper_q, acc[tq:2 * tq], acc[0:tq])


def _sb_close(opened, i, o_ref, j):
    step, state, worst = opened

    def cond(carry):
        t, worst, _ = carry
        return jnp.logical_and(t < i - 1, worst > SB_DEAD)

    def body(carry):
        t, _, st = carry
        st = step(i - 2 - t, st, False)
        return t + 1, _sb_least_decayed(st), st

    @pl.when(cond((jnp.int32(0), worst, None)))
    def _():
        _, _, walked = lax.while_loop(cond, body, (jnp.int32(0), worst, state))
        _sb_write(walked, o_ref, j)


def _attn_call(qa, kv, dq, dk, dv, tri, pw, l, batch, seq):
    tq = MLA_TQ
    assert tq & (tq - 1) == 0 and PROJ_ROWS % tq == 0 and tq % SB_TQ == 0
    nq = seq // tq
    rows = MLA_HEADS * tq
    per_seq = lambda a, w: a.reshape(batch, seq, w)
    whole = lambda w: pl.BlockSpec((None, seq, w), lambda b, i: (b, 0, 0))
    tile = lambda w: pl.BlockSpec((None, tq, w), lambda b, i: (b, i, 0))
    out_mla, out_sb = pl.pallas_call(
        _attn_kernel,
        grid=(batch, nq),
        in_specs=[pl.BlockSpec((rows, LAT_WIDTH), lambda b, i: (b * nq + i, 0)),
                  whole(LAT_WIDTH), _layer_spec(pw["wuv"], l),
                  tile(256), whole(256), whole(256),
                  pl.BlockSpec((2 * SB_TK, SB_TK), lambda b, i: (0, 0))],
        out_specs=[tile(256), tile(256)],
        out_shape=[jax.ShapeDtypeStruct((batch, seq, 256), jnp.float32)] * 2,
        scratch_shapes=[pltpu.VMEM((rows, LANES), jnp.float32),
                        pltpu.VMEM((rows, LAT_WIDTH), jnp.float32)],
        compiler_params=pltpu.CompilerParams(
            dimension_semantics=("arbitrary", "arbitrary"),
            vmem_limit_bytes=VMEM_LIMIT),
        name="attn",
    )(qa, per_seq(kv, LAT_WIDTH), pw["wuv"], per_seq(dq, 256), per_seq(dk, 256),
      per_seq(dv, 256), tri)
    return out_mla.reshape(batch * seq, 256), out_sb.reshape(batch * seq, 256)


def _out_kernel(sink_ref, x_ref, xn_ref, aq_ref, ak_ref, akp_ref, av_ref, avp_ref, yb_ref, yc_ref,
                yd_ref, win_ref, ggrp_ref, wout_ref, gpost_ref, o_ref, wgate_ref, woutb_ref, *,
                layer, tiles_per_seq):
    bf16 = jnp.bfloat16

    @pl.when(pl.program_id(0) == 0)
    def _():
        for c in range(0, D_MIX, 256):
            lo = _O_GATE - GATE_SLAB + c
            wgate_ref[:, c:c + 256] = win_ref[lo:lo + 256, :].T.astype(bf16)
        woutb_ref[...] = wout_ref[...].astype(bf16)

    first_tile = (pl.program_id(0) % tiles_per_seq) == 0
    ya = _swa_tile(sink_ref, aq_ref, ak_ref, akp_ref, av_ref, avp_ref, layer, first_tile)
    gates = _dot(xn_ref[...], wgate_ref[...])
    ys = []
    for g, y_in in enumerate((ya, yb_ref, yc_ref, yd_ref)):
        sl = slice(g * GROUP_WIDTH, (g + 1) * GROUP_WIDTH)
        gate = gates[:, sl]
        y = _rms(y_in[...], ggrp_ref[layer:layer + 1, sl]) * (gate * jax.nn.sigmoid(gate))
        ys.append(y.astype(bf16))
    d = _dot(jnp.concatenate(ys, axis=1), woutb_ref[...])
    o_ref[...] = x_ref[...] + _rms(d, _row(gpost_ref, layer))


def _out_call(x2, xn, sinks, aq, ak, av, yb, yc, yd, pw, l, seq):
    n = x2.shape[0]
    tm = OUT_ROWS
    per = tm // BLOCK
    row = lambda w: pl.BlockSpec((tm, w), lambda i: (i, 0))
    prev = lambda w: pl.BlockSpec((BLOCK, w), lambda i: (jnp.maximum(i * per - 1, 0), 0))
    names = ("win", "ggrp", "wout", "gpost")
    params = [pw[k] for k in names]
    gate_slab = pl.BlockSpec((None, D_IN - GATE_SLAB, D_MODEL), lambda i: (l, 1, 0),
                             pipeline_mode=pl.Buffered(1))
    return pl.pallas_call(
        functools.partial(_out_kernel, layer=l, tiles_per_seq=seq // tm),
        grid=(n // tm,),
        in_specs=[pl.BlockSpec(memory_space=pltpu.SMEM), row(D_MODEL), row(D_MODEL), row(256),
                  row(256), prev(256), row(256), prev(256), row(256), row(256), row(256)]
        + [gate_slab if k == "win" else _layer_spec(a, l, single=(k == "wout"))
           for k, a in zip(names, params)],
        out_specs=row(D_MODEL),
        out_shape=jax.ShapeDtypeStruct((n, D_MODEL), jnp.float32),
        scratch_shapes=[pltpu.VMEM((D_MODEL, D_MIX), jnp.bfloat16),
                        pltpu.VMEM((D_MIX, D_MODEL), jnp.bfloat16)],
        compiler_params=pltpu.CompilerParams(dimension_semantics=("arbitrary",),
                                             vmem_limit_bytes=VMEM_LIMIT),
        name="out",
    )(sinks, x2, xn, aq, ak, ak, av, av, yb, yc, yd, *params)


def _rope_swap(w):
    half = MLA_ROPE_DIM // 2
    return jnp.concatenate([-w[..., half:], w[..., :half]], axis=-1)


def _prep_weights(norm_pre, w_in, conv_w, conv_b, mla_q_norm, mla_w_uq, mla_kv_norm, mla_w_ukv,
                  group_norm, w_out, norm_post):
    f32, bf16 = jnp.float32, jnp.bfloat16
    depth = w_in.shape[0]
    dqk = MLA_NOPE_DIM + MLA_ROPE_DIM
    uq = mla_w_uq.reshape(depth, MLA_Q_RANK, MLA_HEADS, dqk)
    wuqn = uq[..., :MLA_NOPE_DIM].reshape(depth, MLA_Q_RANK, -1).astype(bf16)
    wuqr = jnp.concatenate(
        [uq[..., MLA_NOPE_DIM:].reshape(depth, MLA_Q_RANK, -1),
         _rope_swap(uq[..., MLA_NOPE_DIM:]).reshape(depth, MLA_Q_RANK, -1)], axis=-1).astype(bf16)
    ukv = mla_w_ukv.reshape(depth, MLA_KV_RANK, MLA_HEADS, MLA_NOPE_DIM + MLA_V_DIM)
    own_head = jnp.eye(MLA_HEADS, dtype=f32)[None, :, None, :, None]
    uk_t = jnp.transpose(ukv[..., :MLA_NOPE_DIM], (0, 2, 3, 1))
    wabs = (uk_t[:, :, :, None, :] * own_head).reshape(
        depth, MLA_HEADS * MLA_NOPE_DIM, MLA_HEADS * MLA_KV_RANK).astype(bf16)
    uv = jnp.transpose(ukv[..., MLA_NOPE_DIM:], (0, 2, 1, 3))
    wuv = (uv[:, :, :, None, :] * own_head).reshape(
        depth, MLA_HEADS, MLA_KV_RANK, MLA_HEADS * MLA_V_DIM).astype(bf16)

    return dict(
        gpre=norm_pre, win=jnp.swapaxes(w_in, 1, 2), convw=conv_w, convb=conv_b,
        gcq=mla_q_norm, wuqn=wuqn, wabs=wabs, wuqr=wuqr, gckv=mla_kv_norm, wuv=wuv,
        ggrp=group_norm, wout=w_out, gpost=norm_post)


def kernel(x, positions, norm_pre, w_in, attn_sinks, conv_w, conv_b, mla_q_norm, mla_w_uq,
           mla_kv_norm, mla_w_ukv, group_norm, w_out, norm_post):
    batch, seq, _ = x.shape
    depth = w_in.shape[0]
    assert seq % max(PROJ_ROWS, MLA_TQ, SB_TQ, OUT_ROWS) == 0 and OUT_ROWS % BLOCK == 0
    assert MLA_TQ == MLA_TK and SB_TQ == SB_TK and SB_TQ & (SB_TQ - 1) == 0
    n = batch * seq
    x2 = x.reshape(n, D_MODEL)
    tri = jnp.asarray(np.tile(np.tril(np.ones((SB_TK, SB_TK), np.float32), -1), (2, 1)),
                      jnp.bfloat16)
    pw = _prep_weights(norm_pre, w_in, conv_w, conv_b, mla_q_norm, mla_w_uq, mla_kv_norm,
                       mla_w_ukv, group_norm, w_out, norm_post)
    rope = _rope_call(positions)
    for l in range(depth):
        p = _proj_call(x2, rope, pw, l, seq)
        yc, yd = _attn_call(p["qa"], p["kv"], p["dq"], p["dk"], p["dv"], tri, pw, l, batch, seq)
        x2 = _out_call(x2, p["xn"], attn_sinks, p["aq"], p["ak"], p["av"], p["yb"], yc, yd, pw, l,
                       seq)
    return x2.reshape(batch, seq, D_MODEL)
```
